```python
import jax, jax.numpy as jnp
from jax import lax
import numpy as np

D_MODEL = 2048
BATCH = 2
SEQ = 8192
DEPTH = 1

GRID_W = 64
D_MIX = D_MODEL
NH_A = 4
DH_A = D_MIX // 2 // NH_A
W_A = NH_A * DH_A
NH_B = 8
DH_B = (D_MIX - W_A) // NH_B
W_B = NH_B * DH_B
CHUNK = 64
WIN_ROWS_MAX = 8
WIN_COLS = 16
D_FF = ((8 * D_MODEL // 3 + 255) // 256) * 256
CONV_W = 3
N_MOD = 6
N_GATES = 4 * NH_A
IN_SIZES = (W_A, W_A, W_A, W_A, N_GATES, W_B, W_B, W_B)
IN_COLS = sum(IN_SIZES)
IN_SPLITS = tuple(int(s) for s in np.cumsum(IN_SIZES)[:-1])
EPS = 1e-6

kernel_name = 'hybrid_mlstm_natten_convglu_adaln_block'


def rms_norm(x, g):
    xf = x.astype(jnp.float32)
    y = xf * lax.rsqrt(jnp.mean(xf * xf, axis=-1, keepdims=True) + EPS)
    return (y * g.astype(jnp.float32)).astype(x.dtype)


def modulate(h, shift, scale):
    return h * (1 + scale[:, None, :]) + shift[:, None, :]


def to_heads(t, n_heads):
    b, s, _ = t.shape
    return t.reshape(b, s, n_heads, -1).transpose(0, 2, 1, 3)


def from_heads(t):
    b, h, s, d = t.shape
    return t.transpose(0, 2, 1, 3).reshape(b, s, h * d)


def mlstm_scan(q, k, v, log_i, log_f):
    b_, h_, s_, d_ = q.shape
    nc = s_ // CHUNK

    def chunks(t):
        return jnp.moveaxis(t.reshape(t.shape[:2] + (nc, CHUNK) + t.shape[3:]), 2, 0)

    xs = (chunks(q), chunks(k), chunks(v), chunks(log_i), chunks(log_f))
    lower = jnp.tril(jnp.ones((CHUNK, CHUNK), dtype=bool))

    def step(carry, inp):
        C, n, m = carry
        qc, kc, vc, li, lf = inp
        bcum = jnp.cumsum(lf, axis=-1)
        gsum = bcum[..., -1]
        log_d = jnp.where(lower, bcum[..., :, None] - bcum[..., None, :] + li[..., None, :], -jnp.inf)
        m_inter = bcum + m[..., None]
        m_t = jnp.maximum(m_inter, jnp.max(log_d, axis=-1))
        s = jnp.einsum('bhtd,bhsd->bhts', qc, kc) * jnp.exp(log_d - m_t[..., None])
        a = jnp.exp(m_inter - m_t)
        num = jnp.einsum('bhts,bhse->bhte', s, vc) + a[..., None] * jnp.einsum('bhtd,bhde->bhte', qc, C)
        den = jnp.sum(s, axis=-1) + a * jnp.einsum('bhtd,bhd->bht', qc, n)
        h = num / jnp.maximum(jnp.abs(den), jnp.exp(-m_t))[..., None]
        log_w = gsum[..., None] - bcum + li
        m_new = jnp.maximum(gsum + m, jnp.max(log_w, axis=-1))
        decay = jnp.exp(gsum + m - m_new)
        w = jnp.exp(log_w - m_new[..., None])
        C = decay[..., None, None] * C + jnp.einsum('bhs,bhsd,bhse->bhde', w, kc, vc)
        n = decay[..., None] * n + jnp.einsum('bhs,bhsd->bhd', w, kc)
        return (C, n, m_new), h

    init = (jnp.zeros((b_, h_, d_, d_), jnp.float32),
            jnp.zeros((b_, h_, d_), jnp.float32),
            jnp.full((b_, h_), -1e30, jnp.float32))
    _, hs = lax.scan(step, init, xs)
    return jnp.moveaxis(hs, 0, 2).reshape(b_, h_, s_, d_)


def mlstm_bidirectional(q, k, v, gates):
    i_f, f_f, i_b, f_b = jnp.split(gates, 4, axis=1)
    h_fwd = mlstm_scan(q, k, v, i_f, jax.nn.log_sigmoid(f_f))
    flip = lambda t: jnp.flip(t, axis=2)
    h_bwd = flip(mlstm_scan(flip(q), flip(k), flip(v), flip(i_b), flip(jax.nn.log_sigmoid(f_b))))
    return h_fwd + h_bwd


def neighborhood_attention(q, k, v, rpb):
    b_, h_, s_, d_ = q.shape
    rows = s_ // GRID_W
    kr = min(WIN_ROWS_MAX, rows)
    kc = WIN_COLS
    grid = lambda t: t.reshape(b_, h_, rows, GRID_W, d_)
    q, k, v = grid(q), grid(k), grid(v)
    row_start = jnp.clip(jnp.arange(rows) - kr // 2, 0, rows - kr)
    cols = jnp.arange(GRID_W)
    col_idx = jnp.clip(cols - kc // 2, 0, GRID_W - kc)[:, None] + jnp.arange(kc)
    dc = col_idx - cols[:, None] + (WIN_COLS - 1)
    scale = d_ ** -0.5

    def row_block(r):
        rs = row_start[r]
        qr = lax.dynamic_index_in_dim(q, r, axis=2, keepdims=False)
        kb = lax.dynamic_slice_in_dim(k, rs, kr, axis=2)[:, :, :, col_idx]
        vb = lax.dynamic_slice_in_dim(v, rs, kr, axis=2)[:, :, :, col_idx]
        dr = rs + jnp.arange(kr) - r + (WIN_ROWS_MAX - 1)
        bias = rpb[:, dr[None, :, None], dc[:, None, :]]
        s = jnp.einsum('bhcd,bhrcjd->bhcrj', qr, kb).astype(jnp.float32) * scale + bias.astype(jnp.float32)
        p = jax.nn.softmax(s.reshape(b_, h_, GRID_W, kr * kc), axis=-1).reshape(s.shape)
        return jnp.einsum('bhcrj,bhrcjd->bhcd', p.astype(v.dtype), vb)

    out = lax.map(row_block, jnp.arange(rows))
    return jnp.moveaxis(out, 0, 2).reshape(b_, h_, s_, d_)


def conv_glu(h, w_up, conv_w, conv_b, w_down):
    u, g = jnp.split(h @ w_up, 2, axis=-1)
    s_ = g.shape[1]
    pad = CONV_W // 2
    gp = jnp.pad(g, ((0, 0), (pad, pad), (0, 0)))
    g = sum(gp[:, j:j + s_] * conv_w[j] for j in range(CONV_W)) + conv_b
    return (jax.nn.gelu(g, approximate=False) * u) @ w_down


def setup_inputs(seed: int = 0) -> dict:
    key = jax.random.key(seed)
    ks = jax.random.split(key, 20)
    nrm = jax.random.normal
    f32 = jnp.float32
    x = nrm(ks[0], (BATCH, SEQ, D_MODEL), f32)
    c = nrm(ks[1], (BATCH, D_MODEL), f32)
    w_ada = nrm(ks[2], (DEPTH, D_MODEL, N_MOD * D_MODEL), f32) * (0.5 * D_MODEL ** -0.5)
    b_ada = nrm(ks[3], (DEPTH, N_MOD * D_MODEL), f32) * 0.02
    g_norm1 = 1.0 + 0.02 * nrm(ks[4], (DEPTH, D_MODEL), f32)
    w_in = nrm(ks[5], (DEPTH, D_MODEL, IN_COLS), f32) * D_MODEL ** -0.5
    gk = jax.random.split(ks[6], 4)
    b_gates = jnp.concatenate([
        0.1 * nrm(gk[0], (DEPTH, NH_A), f32),
        3.0 + 0.5 * nrm(gk[1], (DEPTH, NH_A), f32),
        0.1 * nrm(gk[2], (DEPTH, NH_A), f32),
        3.0 + 0.5 * nrm(gk[3], (DEPTH, NH_A), f32),
    ], axis=-1)
    g_head_a = 1.0 + 0.02 * nrm(ks[7], (DEPTH, W_A), f32)
    rpb = 0.1 * nrm(ks[8], (DEPTH, NH_B, 2 * WIN_ROWS_MAX - 1, 2 * WIN_COLS - 1), f32)
    w_out = nrm(ks[9], (DEPTH, D_MIX, D_MODEL), f32) * D_MIX ** -0.5
    g_norm2 = 1.0 + 0.02 * nrm(ks[10], (DEPTH, D_MODEL), f32)
    w_up = nrm(ks[11], (DEPTH, D_MODEL, 2 * D_FF), f32) * D_MODEL ** -0.5
    conv_w = nrm(ks[12], (DEPTH, CONV_W, D_FF), f32) * CONV_W ** -0.5
    conv_b = 0.02 * nrm(ks[13], (DEPTH, D_FF), f32)
    w_down = nrm(ks[14], (DEPTH, D_FF, D_MODEL), f32) * D_FF ** -0.5
    g_final = 1.0 + 0.02 * nrm(ks[15], (D_MODEL,), f32)
    return {'x': x, 'c': c, 'w_ada': w_ada, 'b_ada': b_ada, 'g_norm1': g_norm1, 'w_in': w_in,
            'b_gates': b_gates, 'g_head_a': g_head_a, 'rpb': rpb, 'w_out': w_out, 'g_norm2': g_norm2,
            'w_up': w_up, 'conv_w': conv_w, 'conv_b': conv_b, 'w_down': w_down, 'g_final': g_final}


def reference(x, c, w_ada, b_ada, g_norm1, w_in, b_gates, g_head_a, rpb, w_out, g_norm2,
              w_up, conv_w, conv_b, w_down, g_final):
    f32 = jnp.float32
    for l in range(DEPTH):
        mod = jax.nn.silu(c) @ w_ada[l] + b_ada[l]
        sh1, sc1, ga1, sh2, sc2, ga2 = jnp.split(mod, N_MOD, axis=-1)

        h = modulate(rms_norm(x, g_norm1[l]), sh1, sc1)
        qa, ka, va, oa, gates, qb, kb, vb = jnp.split(h @ w_in[l], IN_SPLITS, axis=-1)

        q_a = to_heads(qa, NH_A).astype(f32)
        k_a = to_heads(ka, NH_A).astype(f32) * (DH_A ** -0.5)
        v_a = to_heads(va, NH_A).astype(f32)
        g_a = (gates.astype(f32) + b_gates[l].astype(f32)).transpose(0, 2, 1)
        h_a = mlstm_bidirectional(q_a, k_a, v_a, g_a)
        h_a = h_a * lax.rsqrt(jnp.mean(h_a * h_a, axis=-1, keepdims=True) + EPS)
        h_a = (from_heads(h_a) * g_head_a[l] * jax.nn.sigmoid(oa.astype(f32))).astype(x.dtype)

        h_b = from_heads(neighborhood_attention(to_heads(qb, NH_B), to_heads(kb, NH_B),
                                                to_heads(vb, NH_B), rpb[l])).astype(x.dtype)

        mixed = jnp.concatenate([h_a, h_b], axis=-1) @ w_out[l]
        x = x + ga1[:, None, :] * mixed

        h = modulate(rms_norm(x, g_norm2[l]), sh2, sc2)
        x = x + ga2[:, None, :] * conv_glu(h, w_up[l], conv_w[l], conv_b[l], w_down[l])
    return rms_norm(x, g_final)
```

```python
import functools

import jax
import jax.numpy as jnp
from jax import lax
from jax.experimental import pallas as pl
from jax.experimental.pallas import tpu as pltpu

F32 = jnp.float32
BF16 = jnp.bfloat16

D_MODEL = 2048
GRID_W = 64
NH_A = 4
DH_A = 256
W_A = NH_A * DH_A
NH_B = 8
DH_B = 128
W_B = NH_B * DH_B
WIN_ROWS = 8
WIN_COLS = 16
D_FF = 5632
N_MOD = 6
EPS = 1e-6
NEG_BIG = -1e30

CHUNK_A = 256
LANE_COLA, LANE_A, LANE_E, LANE_W, LANE_DEC = 0, 8, 16, 24, 32

TM_IN = 512
TN_IN = 1024
TM_OUT = 256
TM_UP = 512
TN_UP = 512
HALO = 16
TM_DOWN = 512
TK_DOWN = 512
NA_GROUP = 8

V7X_VMEM_BYTES = 64 * 1024 * 1024
VMEM_TEMP_ALLOWANCE = 12 * 1024 * 1024


def _vmem_limit(block_bytes, scratch_bytes=0):
    est = 2 * block_bytes + scratch_bytes + VMEM_TEMP_ALLOWANCE
    return int(min(est, V7X_VMEM_BYTES - 4 * 1024 * 1024))


def _nbytes(shape, dtype):
    n = 1
    for s in shape:
        n *= s
    return n * jnp.dtype(dtype).itemsize


def _params(sem, blocks, scratch=()):
    bb = sum(_nbytes(s, d) for s, d in blocks)
    sb = sum(_nbytes(s, d) for s, d in scratch)
    return pltpu.CompilerParams(dimension_semantics=sem, vmem_limit_bytes=_vmem_limit(bb, sb))


def _rms_mod(x, g, shift, scale):
    y = x * lax.rsqrt(jnp.mean(x * x, axis=-1, keepdims=True) + EPS) * g
    return y * (1.0 + scale) + shift


def _ada_kernel(c_ref, w_ref, b_ref, o_ref):
    c = c_ref[...]
    s = c * (1.0 / (1.0 + jnp.exp(-c)))
    o_ref[...] = jnp.dot(s, w_ref[...], preferred_element_type=F32,
                         precision=lax.Precision.HIGHEST) + b_ref[...]


def _ada(c8, w_ada, b_ada):
    n = w_ada.shape[1]
    tn = 1024
    return pl.pallas_call(
        _ada_kernel,
        grid=(n // tn,),
        in_specs=[pl.BlockSpec((8, D_MODEL), lambda j: (0, 0)),
                  pl.BlockSpec((D_MODEL, tn), lambda j: (0, j)),
                  pl.BlockSpec((1, tn), lambda j: (0, j))],
        out_specs=pl.BlockSpec((8, tn), lambda j: (0, j)),
        out_shape=jax.ShapeDtypeStruct((8, n), F32),
        compiler_params=_params(("arbitrary",), [((D_MODEL, tn), F32), ((8, D_MODEL), F32)]),
        name="ada",
    )(c8, w_ada, b_ada)


def _in_kernel(x_ref, mod_ref, g_ref, w_ref, wg_ref, bg_ref, p_ref, gates_ref, h_ref):
    @pl.when(pl.program_id(1) == 0)
    def _():
        h = _rms_mod(x_ref[...], g_ref[...], mod_ref[0, 0:1, :], mod_ref[0, 1:2, :])
        hb = h.astype(BF16)
        h_ref[...] = hb
        gates_ref[...] = jnp.dot(hb, wg_ref[...], preferred_element_type=F32) + bg_ref[...]

    p_ref[...] = jnp.dot(h_ref[...], w_ref[...], preferred_element_type=F32).astype(BF16)


def _in_proj(x2, mod, g1, w_main, w_g, b_g, seq):
    t = x2.shape[0]
    n = w_main.shape[1]
    tiles_per_seq = seq // TM_IN
    return pl.pallas_call(
        _in_kernel,
        grid=(t // TM_IN, n // TN_IN),
        in_specs=[pl.BlockSpec((TM_IN, D_MODEL), lambda i, j: (i, 0)),
                  pl.BlockSpec((1, N_MOD, D_MODEL), lambda i, j: (i // tiles_per_seq, 0, 0)),
                  pl.BlockSpec((1, D_MODEL), lambda i, j: (0, 0)),
                  pl.BlockSpec((D_MODEL, TN_IN), lambda i, j: (0, j)),
                  pl.BlockSpec((D_MODEL, 256), lambda i, j: (0, 0)),
                  pl.BlockSpec((1, 256), lambda i, j: (0, 0))],
        out_specs=[pl.BlockSpec((TM_IN, TN_IN), lambda i, j: (i, j)),
                   pl.BlockSpec((TM_IN, 256), lambda i, j: (i, 0))],
        out_shape=[jax.ShapeDtypeStruct((t, n), BF16),
                   jax.ShapeDtypeStruct((t, 256), F32)],
        scratch_shapes=[pltpu.VMEM((TM_IN, D_MODEL), BF16)],
        compiler_params=_params(
            ("parallel", "arbitrary"),
            [((TM_IN, D_MODEL), F32), ((D_MODEL, TN_IN), BF16), ((D_MODEL, 256), BF16),
             ((TM_IN, TN_IN), BF16), ((TM_IN, 256), F32)],
            [((TM_IN, D_MODEL), BF16)]),
        name="in_proj",
    )(x2, mod, g1, w_main, w_g, b_g)


def _seg_scan(x, row, op, fill, reverse):
    n = x.shape[0]
    d = 1
    while d < n:
        if reverse:
            y = pltpu.roll(x, n - d, axis=0)
            x = op(x, jnp.where(row < n - d, y, fill))
        else:
            y = pltpu.roll(x, d, axis=0)
            x = op(x, jnp.where(row >= d, y, fill))
        d *= 2
    return x


def _gate_direction(g_ref, m_ref, cols_ref, rowb_ref, reverse):
    n = CHUNK_A
    gi = g_ref[:, 0:128]
    gf = g_ref[:, 128:256]
    lf = jnp.minimum(gf, 0.0) - jnp.log1p(jnp.exp(-jnp.abs(gf)))
    row = lax.broadcasted_iota(jnp.int32, (n, 128), 0)
    lane = lax.broadcasted_iota(jnp.int32, (n, 128), 1)
    bc = _seg_scan(lf, row, jnp.add, 0.0, reverse)
    rb = gi - bc
    cm = _seg_scan(rb, row, jnp.maximum, -jnp.inf, reverse)
    last = 0 if reverse else n - 1
    gsum = bc[last:last + 1, :]
    m_loc = gsum + cm[last:last + 1, :]
    m = m_ref[...]
    m_inter = bc + m
    m_t = jnp.maximum(m_inter, bc + cm)
    m_new = jnp.maximum(gsum + m, m_loc)
    m_ref[...] = m_new
    col_a = bc - m_t
    a = jnp.exp(m_inter - m_t)
    e = jnp.exp(-m_t)
    w = jnp.exp(gsum + rb - m_new)
    dec = jnp.broadcast_to(jnp.exp(gsum + m - m_new), (n, 128))
    keep = lane < 8
    packed = jnp.where(keep, col_a, 0.0)
    for val, off in ((a, LANE_A), (e, LANE_E), (w, LANE_W), (dec, LANE_DEC)):
        packed = packed + pltpu.roll(jnp.where(keep, val, 0.0), off, axis=1)
    cols_ref[...] = packed
    rowb_ref[0] = rb.T[0:8, :]


def _gates_kernel(gf_ref, gb_ref, colsf_ref, colsb_ref, rowbf_ref, rowbb_ref, mf_ref, mb_ref):
    @pl.when(pl.program_id(1) == 0)
    def _():
        mf_ref[...] = jnp.full((1, 128), NEG_BIG, F32)
        mb_ref[...] = jnp.full((1, 128), NEG_BIG, F32)

    _gate_direction(gf_ref, mf_ref, colsf_ref, rowbf_ref, reverse=False)
    _gate_direction(gb_ref, mb_ref, colsb_ref, rowbb_ref, reverse=True)


def _gate_vectors(gates, batch, seq):
    nc = seq // CHUNK_A
    t = gates.shape[0]
    fwd = lambda b, k: (b * nc + k, 0)
    bwd = lambda b, k: (b * nc + nc - 1 - k, 0)
    return pl.pallas_call(
        _gates_kernel,
        grid=(batch, nc),
        in_specs=[pl.BlockSpec((CHUNK_A, 256), fwd), pl.BlockSpec((CHUNK_A, 256), bwd)],
        out_specs=[pl.BlockSpec((CHUNK_A, 128), fwd), pl.BlockSpec((CHUNK_A, 128), bwd),
                   pl.BlockSpec((1, 8, CHUNK_A), lambda b, k: (b, 0, k)),
                   pl.BlockSpec((1, 8, CHUNK_A), lambda b, k: (b, 0, nc - 1 - k))],
        out_shape=[jax.ShapeDtypeStruct((t, 128), F32), jax.ShapeDtypeStruct((t, 128), F32),
                   jax.ShapeDtypeStruct((batch, 8, seq), F32),
                   jax.ShapeDtypeStruct((batch, 8, seq), F32)],
        scratch_shapes=[pltpu.VMEM((1, 128), F32), pltpu.VMEM((1, 128), F32)],
        compiler_params=_params(("parallel", "arbitrary"),
                                [((CHUNK_A, 256), F32)] * 2 + [((CHUNK_A, 128), F32)] * 2),
        name="gate_vectors",
    )(gates, gates)


def _mlstm_chunk(q_ref, k_ref, v_ref, cols_ref, rowb_ref, h_ref, c_ref, n_ref, head, seq_id, lane,
                 mask):
    sl = slice(head * DH_A, (head + 1) * DH_A)
    q = q_ref[:, sl]
    k = k_ref[:, sl] * (DH_A ** -0.5)
    v = v_ref[:, sl]
    col_a = cols_ref[:, LANE_COLA + lane:LANE_COLA + lane + 1]
    a = cols_ref[:, LANE_A + lane:LANE_A + lane + 1]
    e = cols_ref[:, LANE_E + lane:LANE_E + lane + 1]
    w = cols_ref[:, LANE_W + lane:LANE_W + lane + 1]
    dec = cols_ref[0:1, LANE_DEC + lane:LANE_DEC + lane + 1]
    row_b = rowb_ref[0, lane:lane + 1, :]
    dmat = jnp.exp(jnp.where(mask, col_a + row_b, -jnp.inf))
    s = lax.dot_general(q, k, (((1,), (1,)), ((), ())), preferred_element_type=F32) * dmat
    c_old = c_ref[seq_id]
    n_old = n_ref[seq_id:seq_id + 1, :]
    num = jnp.dot(s.astype(BF16), v, preferred_element_type=F32)
    num = num + a * jnp.dot(q, c_old.astype(BF16), preferred_element_type=F32)
    den = jnp.sum(s, axis=-1, keepdims=True)
    den = den + a * jnp.sum(q.astype(F32) * n_old, axis=-1, keepdims=True)
    h_ref[:, sl] = num * (1.0 / jnp.maximum(jnp.abs(den), e))
    kw = w * k.astype(F32)
    c_ref[seq_id] = dec * c_old + jnp.dot(kw.T.astype(BF16), v, preferred_element_type=F32)
    n_ref[seq_id:seq_id + 1, :] = dec * n_old + jnp.sum(kw, axis=0, keepdims=True)


def _mlstm_kernel(qf, kf, vf, qb, kb, vb, colsf, colsb, rowbf, rowbb, hf, hb, c_ref, n_ref):
    @pl.when(pl.program_id(1) == 0)
    def _():
        c_ref[...] = jnp.zeros(c_ref.shape, F32)
        n_ref[...] = jnp.zeros(n_ref.shape, F32)

    r = lax.broadcasted_iota(jnp.int32, (CHUNK_A, CHUNK_A), 0)
    c = lax.broadcasted_iota(jnp.int32, (CHUNK_A, CHUNK_A), 1)
    lower = r >= c
    upper = r <= c
    for head in range(NH_A):
        _mlstm_chunk(qf, kf, vf, colsf, rowbf, hf, c_ref, n_ref, head, head, head, lower)
        _mlstm_chunk(qb, kb, vb, colsb, rowbb, hb, c_ref, n_ref, head, NH_A + head, NH_A + head,
                     upper)


def _mlstm(p, colsf, colsb, rowbf, rowbb, batch, seq):
    nc = seq // CHUNK_A
    t = p.shape[0]

    def blk(col, rev):
        if rev:
            return pl.BlockSpec((CHUNK_A, W_A), lambda b, k: (b * nc + nc - 1 - k, col))
        return pl.BlockSpec((CHUNK_A, W_A), lambda b, k: (b * nc + k, col))

    colspec = lambda rev: pl.BlockSpec(
        (CHUNK_A, 128), (lambda b, k: (b * nc + nc - 1 - k, 0)) if rev else (lambda b, k: (b * nc + k, 0)))
    rowspec = lambda rev: pl.BlockSpec(
        (1, 8, CHUNK_A), (lambda b, k: (b, 0, nc - 1 - k)) if rev else (lambda b, k: (b, 0, k)))
    return pl.pallas_call(
        _mlstm_kernel,
        grid=(batch, nc),
        in_specs=[blk(0, False), blk(1, False), blk(2, False),
                  blk(0, True), blk(1, True), blk(2, True),
                  colspec(False), colspec(True), rowspec(False), rowspec(True)],
        out_specs=[blk(0, False), blk(0, True)],
        out_shape=[jax.ShapeDtypeStruct((t, W_A), F32), jax.ShapeDtypeStruct((t, W_A), F32)],
        scratch_shapes=[pltpu.VMEM((2 * NH_A, DH_A, DH_A), F32), pltpu.VMEM((2 * NH_A, DH_A), F32)],
        compiler_params=_params(
            ("parallel", "arbitrary"),
            [((CHUNK_A, W_A), BF16)] * 6 + [((CHUNK_A, 128), F32)] * 2 + [((CHUNK_A, W_A), F32)] * 2,
            [((2 * NH_A, DH_A, DH_A), F32)]),
        name="mlstm",
    )(p, p, p, p, p, p, colsf, colsb, rowbf, rowbb)


def _natten_kernel(q_ref, kp_ref, kc_ref, kn_ref, vp_ref, vc_ref, vn_ref, bias_ref, o_ref,
                   kbuf, vbuf, *, rows):
    g = pl.program_id(1)
    blk = NA_GROUP * GRID_W
    kbuf[0:blk, :] = kp_ref[...]
    kbuf[blk:2 * blk, :] = kc_ref[...]
    kbuf[2 * blk:3 * blk, :] = kn_ref[...]
    vbuf[0:blk, :] = vp_ref[...]
    vbuf[blk:2 * blk, :] = vc_ref[...]
    vbuf[2 * blk:3 * blk, :] = vn_ref[...]
    scale = DH_B ** -0.5
    for i in range(NA_GROUP):
        r = g * NA_GROUP + i
        rs = jnp.clip(r - WIN_ROWS // 2, 0, rows - WIN_ROWS)
        start = pl.multiple_of((rs - (g - 1) * NA_GROUP) * GRID_W, GRID_W)
        variant = rs - r + (WIN_ROWS - 1)
        for h in range(NH_B):
            hs = slice(h * DH_B, (h + 1) * DH_B)
            q = q_ref[i * GRID_W:(i + 1) * GRID_W, hs]
            kw = kbuf[pl.ds(start, WIN_ROWS * GRID_W), hs]
            vw = vbuf[pl.ds(start, WIN_ROWS * GRID_W), hs]
            s = lax.dot_general(q, kw, (((1,), (1,)), ((), ())), preferred_element_type=F32)
            s = s * scale + bias_ref[variant, h]
            m = jnp.max(s, axis=-1, keepdims=True)
            p = jnp.exp(s - m)
            l = jnp.sum(p, axis=-1, keepdims=True)
            o = jnp.dot(p.astype(BF16), vw, preferred_element_type=F32)
            o_ref[i * GRID_W:(i + 1) * GRID_W, hs] = (o * (1.0 / l)).astype(BF16)


def _natten(p, bias, batch, seq):
    rows = seq // GRID_W
    ng = rows // NA_GROUP
    blk = NA_GROUP * GRID_W
    t = p.shape[0]
    qcol, kcol, vcol = 4, 5, 6

    def spec(col, shift):
        def imap(b, g):
            return (b * ng + jnp.clip(g + shift, 0, ng - 1), col)
        return pl.BlockSpec((blk, W_B), imap)

    return pl.pallas_call(
        functools.partial(_natten_kernel, rows=rows),
        grid=(batch, ng),
        in_specs=[spec(qcol, 0),
                  spec(kcol, -1), spec(kcol, 0), spec(kcol, 1),
                  spec(vcol, -1), spec(vcol, 0), spec(vcol, 1),
                  pl.BlockSpec(bias.shape, lambda b, g: (0, 0, 0, 0))],
        out_specs=pl.BlockSpec((blk, W_B), lambda b, g: (b * ng + g, 0)),
        out_shape=jax.ShapeDtypeStruct((t, W_B), BF16),
        scratch_shapes=[pltpu.VMEM((3 * blk, W_B), BF16), pltpu.VMEM((3 * blk, W_B), BF16)],
        compiler_params=_params(
            ("parallel", "arbitrary"),
            [((blk, W_B), BF16)] * 8 + [(bias.shape, F32)],
            [((3 * blk, W_B), BF16)] * 2),
        name="natten",
    )(p, p, p, p, p, p, p, bias)


def _natten_bias(rpb):
    cols = jnp.arange(GRID_W)
    cstart = jnp.clip(cols - WIN_COLS // 2, 0, GRID_W - WIN_COLS)
    kc = jnp.arange(GRID_W)
    inside = (kc[None, :] >= cstart[:, None]) & (kc[None, :] < cstart[:, None] + WIN_COLS)
    dc = jnp.clip(kc[None, :] - cols[:, None] + (WIN_COLS - 1), 0, 2 * WIN_COLS - 2)
    variant = jnp.arange(WIN_ROWS)
    kr = jnp.arange(WIN_ROWS)
    dr = variant[:, None] + kr[None, :]
    vals = rpb[:, dr[:, :, None, None], dc[None, None, :, :]]
    vals = jnp.where(inside[None, None, None], vals, NEG_BIG)
    vals = vals.transpose(1, 0, 3, 2, 4)
    return vals.reshape(WIN_ROWS, NH_B, GRID_W, WIN_ROWS * GRID_W).astype(F32)


def _out_kernel(hf_ref, hb_ref, oa_ref, hbt_ref, x_ref, mod_ref, gh_ref, wa_ref, wb_ref, o_ref):
    hs = hf_ref[...] + hb_ref[...]
    parts = []
    for head in range(NH_A):
        sl = slice(head * DH_A, (head + 1) * DH_A)
        hh = hs[:, sl]
        parts.append(hh * lax.rsqrt(jnp.mean(hh * hh, axis=-1, keepdims=True) + EPS))
    hn = jnp.concatenate(parts, axis=-1) * gh_ref[...]
    oa = oa_ref[...].astype(F32)
    ha = (hn * (1.0 / (1.0 + jnp.exp(-oa)))).astype(BF16)
    mixed = jnp.dot(ha, wa_ref[...], preferred_element_type=F32)
    mixed = mixed + jnp.dot(hbt_ref[...], wb_ref[...], preferred_element_type=F32)
    o_ref[...] = x_ref[...] + mod_ref[0, 2:3, :] * mixed


def _out_proj(hf, hb, p, hbt, x2, mod, gh, w_a, w_b, seq):
    t = x2.shape[0]
    tiles_per_seq = seq // TM_OUT
    row = lambda i: (i, 0)
    return pl.pallas_call(
        _out_kernel,
        grid=(t // TM_OUT,),
        in_specs=[pl.BlockSpec((TM_OUT, W_A), row), pl.BlockSpec((TM_OUT, W_A), row),
                  pl.BlockSpec((TM_OUT, W_A), lambda i: (i, 3)),
                  pl.BlockSpec((TM_OUT, W_B), row),
                  pl.BlockSpec((TM_OUT, D_MODEL), row),
                  pl.BlockSpec((1, N_MOD, D_MODEL), lambda i: (i // tiles_per_seq, 0, 0)),
                  pl.BlockSpec((1, W_A), lambda i: (0, 0)),
                  pl.BlockSpec((W_A, D_MODEL), lambda i: (0, 0)),
                  pl.BlockSpec((W_B, D_MODEL), lambda i: (0, 0))],
        out_specs=pl.BlockSpec((TM_OUT, D_MODEL), row),
        out_shape=jax.ShapeDtypeStruct((t, D_MODEL), F32),
        compiler_params=_params(
            ("parallel",),
            [((TM_OUT, W_A), F32)] * 2 + [((TM_OUT, W_A), BF16)] * 2 + [((TM_OUT, D_MODEL), F32)] * 2
            + [((W_A, D_MODEL), BF16)] * 2),
        name="out_proj",
    )(hf, hb, p, hbt, x2, mod, gh, w_a, w_b)


def _up_kernel(xp_ref, x_ref, xn_ref, mod_ref, g_ref, wu_ref, wg_ref, cw_ref, cb_ref, act_ref,
               h_ref, *, tiles_per_seq):
    i = pl.program_id(0)

    @pl.when(pl.program_id(1) == 0)
    def _():
        shift = mod_ref[0, 3:4, :]
        scale = mod_ref[0, 4:5, :]
        gn = g_ref[...]
        h_ref[0:HALO, :] = _rms_mod(xp_ref[...], gn, shift, scale).astype(BF16)
        h_ref[HALO:HALO + TM_UP, :] = _rms_mod(x_ref[...], gn, shift, scale).astype(BF16)
        h_ref[HALO + TM_UP:, :] = _rms_mod(xn_ref[...], gn, shift, scale).astype(BF16)

    u = jnp.dot(h_ref[HALO:HALO + TM_UP, :], wu_ref[...], preferred_element_type=F32)
    g = jnp.dot(h_ref[...], wg_ref[...], preferred_element_type=F32)
    n = TM_UP + 2 * HALO
    row = lax.broadcasted_iota(jnp.int32, (n, 1), 0)
    pos = i % tiles_per_seq
    lo = jnp.where(pos > 0, 0, HALO)
    hi = jnp.where(pos < tiles_per_seq - 1, n, HALO + TM_UP)
    g = jnp.where(jnp.logical_and(row >= lo, row < hi), g, 0.0)
    g_prev = pltpu.roll(g, 1, axis=0)[HALO:HALO + TM_UP, :]
    g_next = pltpu.roll(g, n - 1, axis=0)[HALO:HALO + TM_UP, :]
    gc = (g_prev * cw_ref[0:1, :] + g[HALO:HALO + TM_UP, :] * cw_ref[1:2, :]
          + g_next * cw_ref[2:3, :] + cb_ref[...])
    gelu = 0.5 * gc * (1.0 + lax.erf(gc * (2.0 ** -0.5)))
    act_ref[...] = (gelu * u).astype(BF16)


def _up_proj(x1, mod, g2, w_u, w_g, conv_w, conv_b, seq):
    t = x1.shape[0]
    tiles_per_seq = seq // TM_UP
    hb = TM_UP // HALO
    nh = t // HALO
    return pl.pallas_call(
        functools.partial(_up_kernel, tiles_per_seq=tiles_per_seq),
        grid=(t // TM_UP, D_FF // TN_UP),
        in_specs=[pl.BlockSpec((HALO, D_MODEL), lambda i, j: (jnp.maximum(i * hb - 1, 0), 0)),
                  pl.BlockSpec((TM_UP, D_MODEL), lambda i, j: (i, 0)),
                  pl.BlockSpec((HALO, D_MODEL), lambda i, j: (jnp.minimum((i + 1) * hb, nh - 1), 0)),
                  pl.BlockSpec((1, N_MOD, D_MODEL), lambda i, j: (i // tiles_per_seq, 0, 0)),
                  pl.BlockSpec((1, D_MODEL), lambda i, j: (0, 0)),
                  pl.BlockSpec((D_MODEL, TN_UP), lambda i, j: (0, j)),
                  pl.BlockSpec((D_MODEL, TN_UP), lambda i, j: (0, j)),
                  pl.BlockSpec((8, TN_UP), lambda i, j: (0, j)),
                  pl.BlockSpec((1, TN_UP), lambda i, j: (0, j))],
        out_specs=pl.BlockSpec((TM_UP, TN_UP), lambda i, j: (i, j)),
        out_shape=jax.ShapeDtypeStruct((t, D_FF), BF16),
        scratch_shapes=[pltpu.VMEM((TM_UP + 2 * HALO, D_MODEL), BF16)],
        compiler_params=_params(
            ("parallel", "arbitrary"),
            [((TM_UP + 2 * HALO, D_MODEL), F32)] + [((D_MODEL, TN_UP), BF16)] * 2
            + [((TM_UP, TN_UP), BF16)],
            [((TM_UP + 2 * HALO, D_MODEL), BF16), ((TM_UP + 2 * HALO, TN_UP), F32)]),
        name="up_proj",
    )(x1, x1, x1, mod, g2, w_u, w_g, conv_w, conv_b)


def _down_kernel(a_ref, w_ref, x_ref, mod_ref, g_ref, o_ref, acc_ref):
    k = pl.program_id(1)

    @pl.when(k == 0)
    def _():
        acc_ref[...] = jnp.zeros(acc_ref.shape, F32)

    acc_ref[...] += jnp.dot(a_ref[...], w_ref[...], preferred_element_type=F32)

    @pl.when(k == pl.num_programs(1) - 1)
    def _():
        y = x_ref[...] + mod_ref[0, 5:6, :] * acc_ref[...]
        o_ref[...] = y * lax.rsqrt(jnp.mean(y * y, axis=-1, keepdims=True) + EPS) * g_ref[...]


def _down_proj(act, w_d, x1, mod, gfin, seq):
    t = x1.shape[0]
    tiles_per_seq = seq // TM_DOWN
    return pl.pallas_call(
        _down_kernel,
        grid=(t // TM_DOWN, D_FF // TK_DOWN),
        in_specs=[pl.BlockSpec((TM_DOWN, TK_DOWN), lambda i, k: (i, k)),
                  pl.BlockSpec((TK_DOWN, D_MODEL), lambda i, k: (k, 0)),
                  pl.BlockSpec((TM_DOWN, D_MODEL), lambda i, k: (i, 0)),
                  pl.BlockSpec((1, N_MOD, D_MODEL), lambda i, k: (i // tiles_per_seq, 0, 0)),
                  pl.BlockSpec((1, D_MODEL), lambda i, k: (0, 0))],
        out_specs=pl.BlockSpec((TM_DOWN, D_MODEL), lambda i, k: (i, 0)),
        out_shape=jax.ShapeDtypeStruct((t, D_MODEL), F32),
        scratch_shapes=[pltpu.VMEM((TM_DOWN, D_MODEL), F32)],
        compiler_params=_params(
            ("parallel", "arbitrary"),
            [((TM_DOWN, TK_DOWN), BF16), ((TK_DOWN, D_MODEL), BF16)] + [((TM_DOWN, D_MODEL), F32)] * 2,
            [((TM_DOWN, D_MODEL), F32)]),
        name="down_proj",
    )(act, w_d, x1, mod, gfin)


def kernel(x, c, w_ada, b_ada, g_norm1, w_in, b_gates, g_head_a, rpb, w_out, g_norm2, w_up, conv_w,
           conv_b, w_down, g_final):
    batch, seq, d = x.shape
    assert w_ada.shape[0] == 1
    assert d == D_MODEL and seq % CHUNK_A == 0 and seq % (NA_GROUP * GRID_W) == 0
    x2 = x.reshape(batch * seq, d)
    c8 = jnp.zeros((8, d), F32).at[:batch].set(c)
    mod = _ada(c8, w_ada[0], b_ada)[:batch].reshape(batch, N_MOD, d)

    gate_lo = 4 * W_A
    gate_hi = gate_lo + 4 * NH_A
    w_main = jnp.concatenate([w_in[0][:, :gate_lo], w_in[0][:, gate_hi:]], axis=1).astype(BF16)
    wg = w_in[0][:, gate_lo:gate_hi]
    bg = b_gates[0]
    w_g = jnp.zeros((d, 256), F32)
    w_g = w_g.at[:, 0:4].set(wg[:, 0:4]).at[:, 4:8].set(wg[:, 8:12])
    w_g = w_g.at[:, 128:132].set(wg[:, 4:8]).at[:, 132:136].set(wg[:, 12:16]).astype(BF16)
    b_g = jnp.zeros((1, 256), F32)
    b_g = b_g.at[0, 0:4].set(bg[0:4]).at[0, 4:8].set(bg[8:12])
    b_g = b_g.at[0, 128:132].set(bg[4:8]).at[0, 132:136].set(bg[12:16])

    p, gates = _in_proj(x2, mod, g_norm1, w_main, w_g, b_g, seq)
    colsf, colsb, rowbf, rowbb = _gate_vectors(gates, batch, seq)
    hf, hb = _mlstm(p, colsf, colsb, rowbf, rowbb, batch, seq)
    hbt = _natten(p, _natten_bias(rpb[0]), batch, seq)
    w_o = w_out[0].astype(BF16)
    x1 = _out_proj(hf, hb, p, hbt, x2, mod, g_head_a, w_o[:W_A], w_o[W_A:], seq)

    w_u = w_up[0][:, :D_FF].astype(BF16)
    w_gt = w_up[0][:, D_FF:].astype(BF16)
    cw8 = jnp.zeros((8, D_FF), F32).at[:3].set(conv_w[0])
    act = _up_proj(x1, mod, g_norm2, w_u, w_gt, cw8, conv_b, seq)
    out = _down_proj(act, w_down[0].astype(BF16), x1, mod, g_final[None, :], seq)
    return out.reshape(batch, seq, d)
```

```python
import functools

import jax
import jax.numpy as jnp
import numpy as np
from jax import lax
from jax.experimental import pallas as pl
from jax.experimental.pallas import tpu as pltpu

F32 = jnp.float32
BF16 = jnp.bfloat16

D_MODEL = 2048
GRID_W = 64
NH_A = 4
DH_A = 256
W_A = NH_A * DH_A
NH_B = 8
DH_B = 128
W_B = NH_B * DH_B
WIN_ROWS = 8
WIN_COLS = 16
D_FF = 5632
N_MOD = 6
EPS = 1e-6
NEG_BIG = -1e30

CHUNK_A = 256
LANE_COLA, LANE_A, LANE_E, LANE_W, LANE_DEC = 0, 8, 16, 24, 32

TM_IN = 512
TN_IN = 1024
TM_OUT = 256
TM_UP = 512
TN_UP = 512
HALO = 16
TM_DOWN = 512
TK_DOWN = 512
NA_GROUP = 8
NA_LOOKAHEAD = 4

V7X_VMEM_BYTES = 64 * 1024 * 1024
VMEM_TEMP_ALLOWANCE = 12 * 1024 * 1024


def _vmem_limit(block_bytes, scratch_bytes=0):
    est = 2 * block_bytes + scratch_bytes + VMEM_TEMP_ALLOWANCE
    return int(min(est, V7X_VMEM_BYTES - 4 * 1024 * 1024))


def _nbytes(shape, dtype):
    n = 1
    for s in shape:
        n *= s
    return n * jnp.dtype(dtype).itemsize


def _params(sem, blocks, scratch=()):
    bb = sum(_nbytes(s, d) for s, d in blocks)
    sb = sum(_nbytes(s, d) for s, d in scratch)
    return pltpu.CompilerParams(dimension_semantics=sem, vmem_limit_bytes=_vmem_limit(bb, sb))


def _rms_mod(x, g, shift, scale):
    y = x * lax.rsqrt(jnp.mean(x * x, axis=-1, keepdims=True) + EPS) * g
    return y * (1.0 + scale) + shift


def _ada_kernel(c_ref, w_ref, b_ref, o_ref):
    c = c_ref[...]
    s = c * (1.0 / (1.0 + jnp.exp(-c)))
    o_ref[...] = jnp.dot(s, w_ref[...], preferred_element_type=F32,
                         precision=lax.Precision.HIGHEST) + b_ref[...]


def _ada(c8, w_ada, b_ada):
    n = w_ada.shape[1]
    tn = 1024
    return pl.pallas_call(
        _ada_kernel,
        grid=(n // tn,),
        in_specs=[pl.BlockSpec((8, D_MODEL), lambda j: (0, 0)),
                  pl.BlockSpec((D_MODEL, tn), lambda j: (0, j)),
                  pl.BlockSpec((1, tn), lambda j: (0, j))],
        out_specs=pl.BlockSpec((8, tn), lambda j: (0, j)),
        out_shape=jax.ShapeDtypeStruct((8, n), F32),
        compiler_params=_params(("arbitrary",), [((D_MODEL, tn), F32), ((8, D_MODEL), F32)]),
        name="ada",
    )(c8, w_ada, b_ada)


def _in_kernel(x_ref, mod_ref, g_ref, w_ref, wg_ref, bg_ref, p_ref, gates_ref, h_ref):
    @pl.when(pl.program_id(1) == 0)
    def _():
        h = _rms_mod(x_ref[...], g_ref[...], mod_ref[0, 0:1, :], mod_ref[0, 1:2, :])
        hb = h.astype(BF16)
        h_ref[...] = hb
        gates_ref[...] = jnp.dot(hb, wg_ref[...], preferred_element_type=F32) + bg_ref[...]

    p_ref[...] = jnp.dot(h_ref[...], w_ref[...], preferred_element_type=F32).astype(BF16)


def _in_proj(x2, mod, g1, w_main, w_g, b_g, seq):
    t = x2.shape[0]
    n = w_main.shape[1]
    tiles_per_seq = seq // TM_IN
    return pl.pallas_call(
        _in_kernel,
        grid=(t // TM_IN, n // TN_IN),
        in_specs=[pl.BlockSpec((TM_IN, D_MODEL), lambda i, j: (i, 0)),
                  pl.BlockSpec((1, N_MOD, D_MODEL), lambda i, j: (i // tiles_per_seq, 0, 0)),
                  pl.BlockSpec((1, D_MODEL), lambda i, j: (0, 0)),
                  pl.BlockSpec((D_MODEL, TN_IN), lambda i, j: (0, j)),
                  pl.BlockSpec((D_MODEL, 256), lambda i, j: (0, 0)),
                  pl.BlockSpec((1, 256), lambda i, j: (0, 0))],
        out_specs=[pl.BlockSpec((TM_IN, TN_IN), lambda i, j: (i, j)),
                   pl.BlockSpec((TM_IN, 256), lambda i, j: (i, 0))],
        out_shape=[jax.ShapeDtypeStruct((t, n), BF16),
                   jax.ShapeDtypeStruct((t, 256), F32)],
        scratch_shapes=[pltpu.VMEM((TM_IN, D_MODEL), BF16)],
        compiler_params=_params(
            ("parallel", "arbitrary"),
            [((TM_IN, D_MODEL), F32), ((D_MODEL, TN_IN), BF16), ((D_MODEL, 256), BF16),
             ((TM_IN, TN_IN), BF16), ((TM_IN, 256), F32)],
            [((TM_IN, D_MODEL), BF16)]),
        name="in_proj",
    )(x2, mod, g1, w_main, w_g, b_g)


def _seg_scan(x, row, op, fill, reverse):
    n = x.shape[0]
    d = 1
    while d < n:
        if reverse:
            y = pltpu.roll(x, n - d, axis=0)
            x = op(x, jnp.where(row < n - d, y, fill))
        else:
            y = pltpu.roll(x, d, axis=0)
            x = op(x, jnp.where(row >= d, y, fill))
        d *= 2
    return x


def _gate_direction(g_ref, m_ref, cols_ref, rowb_ref, reverse):
    n = CHUNK_A
    gi = g_ref[:, 0:128]
    gf = g_ref[:, 128:256]
    lf = jnp.minimum(gf, 0.0) - jnp.log1p(jnp.exp(-jnp.abs(gf)))
    row = lax.broadcasted_iota(jnp.int32, (n, 128), 0)
    lane = lax.broadcasted_iota(jnp.int32, (n, 128), 1)
    bc = _seg_scan(lf, row, jnp.add, 0.0, reverse)
    rb = gi - bc
    cm = _seg_scan(rb, row, jnp.maximum, -jnp.inf, reverse)
    last = 0 if reverse else n - 1
    gsum = bc[last:last + 1, :]
    m_loc = gsum + cm[last:last + 1, :]
    m = m_ref[...]
    m_inter = bc + m
    m_t = jnp.maximum(m_inter, bc + cm)
    m_new = jnp.maximum(gsum + m, m_loc)
    m_ref[...] = m_new
    col_a = bc - m_t
    a = jnp.exp(m_inter - m_t)
    e = jnp.exp(-m_t)
    w = jnp.exp(gsum + rb - m_new)
    dec = jnp.broadcast_to(jnp.exp(gsum + m - m_new), (n, 128))
    keep = lane < 8
    packed = jnp.where(keep, col_a, 0.0)
    for val, off in ((a, LANE_A), (e, LANE_E), (w, LANE_W), (dec, LANE_DEC)):
        packed = packed + pltpu.roll(jnp.where(keep, val, 0.0), off, axis=1)
    cols_ref[...] = packed
    rowb_ref[0] = rb.T[0:8, :]


def _gates_kernel(gf_ref, gb_ref, colsf_ref, colsb_ref, rowbf_ref, rowbb_ref, mf_ref, mb_ref):
    @pl.when(pl.program_id(1) == 0)
    def _():
        mf_ref[...] = jnp.full((1, 128), NEG_BIG, F32)
        mb_ref[...] = jnp.full((1, 128), NEG_BIG, F32)

    _gate_direction(gf_ref, mf_ref, colsf_ref, rowbf_ref, reverse=False)
    _gate_direction(gb_ref, mb_ref, colsb_ref, rowbb_ref, reverse=True)


def _gate_vectors(gates, batch, seq):
    nc = seq // CHUNK_A
    t = gates.shape[0]
    fwd = lambda b, k: (b * nc + k, 0)
    bwd = lambda b, k: (b * nc + nc - 1 - k, 0)
    return pl.pallas_call(
        _gates_kernel,
        grid=(batch, nc),
        in_specs=[pl.BlockSpec((CHUNK_A, 256), fwd), pl.BlockSpec((CHUNK_A, 256), bwd)],
        out_specs=[pl.BlockSpec((CHUNK_A, 128), fwd), pl.BlockSpec((CHUNK_A, 128), bwd),
                   pl.BlockSpec((1, 8, CHUNK_A), lambda b, k: (b, 0, k)),
                   pl.BlockSpec((1, 8, CHUNK_A), lambda b, k: (b, 0, nc - 1 - k))],
        out_shape=[jax.ShapeDtypeStruct((t, 128), F32), jax.ShapeDtypeStruct((t, 128), F32),
                   jax.ShapeDtypeStruct((batch, 8, seq), F32),
                   jax.ShapeDtypeStruct((batch, 8, seq), F32)],
        scratch_shapes=[pltpu.VMEM((1, 128), F32), pltpu.VMEM((1, 128), F32)],
        compiler_params=_params(("parallel", "arbitrary"),
                                [((CHUNK_A, 256), F32)] * 2 + [((CHUNK_A, 128), F32)] * 2),
        name="gate_vectors",
    )(gates, gates)


def _mlstm_kernel(qf, kf, vf, qb, kb, vb, colsf, colsb, rowbf, rowbb, hf, hb, ct_ref):
    @pl.when(pl.program_id(1) == 0)
    def _():
        ct_ref[...] = jnp.zeros(ct_ref.shape, F32)

    n = CHUNK_A
    r = lax.broadcasted_iota(jnp.int32, (n, n), 0)
    c = lax.broadcasted_iota(jnp.int32, (n, n), 1)
    ones_cols = jnp.ones((n, 128), BF16)
    ones_rows = jnp.ones((128, n), BF16)
    seqs = []
    for head in range(NH_A):
        seqs.append((qf, kf, vf, colsf, rowbf, hf, head, head, r >= c))
        seqs.append((qb, kb, vb, colsb, rowbb, hb, head, NH_A + head, r <= c))

    loaded = []
    for q_ref, k_ref, v_ref, _, _, _, head, sid, _ in seqs:
        sl = slice(head * DH_A, (head + 1) * DH_A)
        q = q_ref[:, sl]
        k = k_ref[:, sl] * (DH_A ** -0.5)
        v = v_ref[:, sl]
        ct_old = ct_ref[sid]
        qk = lax.dot_general(q, k, (((1,), (1,)), ((), ())), preferred_element_type=F32)
        qc = lax.dot_general(q, ct_old.astype(BF16), (((1,), (1,)), ((), ())),
                             preferred_element_type=F32)
        loaded.append((k, v, ct_old, qk, qc))

    for (_, _, _, cols_ref, rowb_ref, h_ref, head, sid, mask), (k, v, ct_old, qk, qc) in zip(seqs, loaded):
        sl = slice(head * DH_A, (head + 1) * DH_A)
        col_a = cols_ref[:, LANE_COLA + sid:LANE_COLA + sid + 1]
        a = cols_ref[:, LANE_A + sid:LANE_A + sid + 1]
        e = cols_ref[:, LANE_E + sid:LANE_E + sid + 1]
        row_b = rowb_ref[0, sid:sid + 1, :]
        s = qk * jnp.exp(jnp.where(mask, col_a + row_b, -jnp.inf))
        v_ext = jnp.concatenate([v, ones_cols], axis=1)
        sv = jnp.dot(s.astype(BF16), v_ext, preferred_element_type=F32) + a * qc
        inv = 1.0 / jnp.maximum(jnp.abs(sv[:, DH_A:]), e)
        h_ref[:, sl] = sv[:, :DH_A] * jnp.concatenate([inv, inv], axis=1)

    for (_, _, _, cols_ref, _, _, _, sid, _), (k, v, ct_old, _, _) in zip(seqs, loaded):
        w = cols_ref[:, LANE_W + sid:LANE_W + sid + 1]
        dec = cols_ref[0:1, LANE_DEC + sid:LANE_DEC + sid + 1]
        kw = (w * k.astype(F32)).astype(BF16)
        vt_ext = jnp.concatenate([v.T, ones_rows], axis=0)
        ct_ref[sid] = dec * ct_old + jnp.dot(vt_ext, kw, preferred_element_type=F32)


def _mlstm(p, colsf, colsb, rowbf, rowbb, batch, seq):
    nc = seq // CHUNK_A
    t = p.shape[0]

    def blk(col, rev):
        if rev:
            return pl.BlockSpec((CHUNK_A, W_A), lambda b, k: (b * nc + nc - 1 - k, col))
        return pl.BlockSpec((CHUNK_A, W_A), lambda b, k: (b * nc + k, col))

    colspec = lambda rev: pl.BlockSpec(
        (CHUNK_A, 128), (lambda b, k: (b * nc + nc - 1 - k, 0)) if rev else (lambda b, k: (b * nc + k, 0)))
    rowspec = lambda rev: pl.BlockSpec(
        (1, 8, CHUNK_A), (lambda b, k: (b, 0, nc - 1 - k)) if rev else (lambda b, k: (b, 0, k)))
    return pl.pallas_call(
        _mlstm_kernel,
        grid=(batch, nc),
        in_specs=[blk(0, False), blk(1, False), blk(2, False),
                  blk(0, True), blk(1, True), blk(2, True),
                  colspec(False), colspec(True), rowspec(False), rowspec(True)],
        out_specs=[blk(0, False), blk(0, True)],
        out_shape=[jax.ShapeDtypeStruct((t, W_A), F32), jax.ShapeDtypeStruct((t, W_A), F32)],
        scratch_shapes=[pltpu.VMEM((2 * NH_A, DH_A + 128, DH_A), F32)],
        compiler_params=_params(
            ("parallel", "arbitrary"),
            [((CHUNK_A, W_A), BF16)] * 6 + [((CHUNK_A, 128), F32)] * 2 + [((CHUNK_A, W_A), F32)] * 2,
            [((2 * NH_A, DH_A + 128, DH_A), F32)]),
        name="mlstm",
    )(p, p, p, p, p, p, colsf, colsb, rowbf, rowbb)


def _natten_kernel(q_ref, kp_ref, kc_ref, kn_ref, vp_ref, vc_ref, vn_ref, bias_ref, o_ref,
                   kbuf, vbuf, *, rows):
    g = pl.program_id(1)
    blk = NA_GROUP * GRID_W
    kbuf[0:blk, :] = kp_ref[...]
    kbuf[blk:2 * blk, :] = kc_ref[...]
    kbuf[2 * blk:3 * blk, :] = kn_ref[...]
    vbuf[0:blk, :] = vp_ref[...]
    vbuf[blk:2 * blk, :] = vc_ref[...]
    vbuf[2 * blk:3 * blk, :] = vn_ref[...]
    scale = DH_B ** -0.5
    starts, variants = [], []
    for i in range(NA_GROUP):
        r = g * NA_GROUP + i
        rs = jnp.clip(r - WIN_ROWS // 2, 0, rows - WIN_ROWS)
        starts.append(pl.multiple_of((rs - (g - 1) * NA_GROUP) * GRID_W, GRID_W))
        variants.append(rs - r + (WIN_ROWS - 1))

    def scores(i, h):
        hs = slice(h * DH_B, (h + 1) * DH_B)
        q = q_ref[i * GRID_W:(i + 1) * GRID_W, hs]
        kw = kbuf[pl.ds(starts[i], WIN_ROWS * GRID_W), hs]
        s = lax.dot_general(q, kw, (((1,), (1,)), ((), ())), preferred_element_type=F32)
        s = s * scale + bias_ref[variants[i], h]
        p = jnp.exp(s - jnp.max(s, axis=-1, keepdims=True))
        return p.astype(BF16), 1.0 / jnp.sum(p, axis=-1, keepdims=True)

    def weighted_sum(i, h, p, inv_l):
        hs = slice(h * DH_B, (h + 1) * DH_B)
        vw = vbuf[pl.ds(starts[i], WIN_ROWS * GRID_W), hs]
        o = jnp.dot(p, vw, preferred_element_type=F32)
        o_ref[i * GRID_W:(i + 1) * GRID_W, hs] = (o * inv_l).astype(BF16)

    tiles = [(i, h) for i in range(NA_GROUP) for h in range(NH_B)]
    pending = []
    for idx in range(len(tiles) + NA_LOOKAHEAD):
        if idx < len(tiles):
            pending.append(scores(*tiles[idx]))
        if idx >= NA_LOOKAHEAD:
            weighted_sum(*tiles[idx - NA_LOOKAHEAD], *pending[idx - NA_LOOKAHEAD])
            pending[idx - NA_LOOKAHEAD] = None


def _natten(p, bias, batch, seq):
    rows = seq // GRID_W
    ng = rows // NA_GROUP
    blk = NA_GROUP * GRID_W
    t = p.shape[0]
    qcol, kcol, vcol = 4, 5, 6

    def spec(col, shift):
        def imap(b, g):
            return (b * ng + jnp.clip(g + shift, 0, ng - 1), col)
        return pl.BlockSpec((blk, W_B), imap)

    return pl.pallas_call(
        functools.partial(_natten_kernel, rows=rows),
        grid=(batch, ng),
        in_specs=[spec(qcol, 0),
                  spec(kcol, -1), spec(kcol, 0), spec(kcol, 1),
                  spec(vcol, -1), spec(vcol, 0), spec(vcol, 1),
                  pl.BlockSpec(bias.shape, lambda b, g: (0, 0, 0, 0))],
        out_specs=pl.BlockSpec((blk, W_B), lambda b, g: (b * ng + g, 0)),
        out_shape=jax.ShapeDtypeStruct((t, W_B), BF16),
        scratch_shapes=[pltpu.VMEM((3 * blk, W_B), BF16), pltpu.VMEM((3 * blk, W_B), BF16)],
        compiler_params=_params(
            ("parallel", "arbitrary"),
            [((blk, W_B), BF16)] * 8 + [(bias.shape, F32)],
            [((3 * blk, W_B), BF16)] * 2),
        name="natten",
    )(p, p, p, p, p, p, p, bias)


def _natten_bias(rpb):
    cols = np.arange(GRID_W)
    cstart = np.clip(cols - WIN_COLS // 2, 0, GRID_W - WIN_COLS)
    inside = (cols[None, :] >= cstart[:, None]) & (cols[None, :] < cstart[:, None] + WIN_COLS)
    lpad = GRID_W - WIN_COLS
    padded = jnp.pad(rpb.astype(F32), ((0, 0), (0, 0), (lpad, lpad)))
    toep = jnp.stack([padded[:, :, GRID_W - 1 - c:2 * GRID_W - 1 - c] for c in range(GRID_W)], axis=2)
    toep = jnp.where(inside[None, None], toep, NEG_BIG)
    tiles = []
    for variant in range(WIN_ROWS):
        t = toep[:, variant:variant + WIN_ROWS].transpose(0, 2, 1, 3)
        tiles.append(t.reshape(NH_B, GRID_W, WIN_ROWS * GRID_W))
    return jnp.stack(tiles, axis=0)


def _out_kernel(hf_ref, hb_ref, oa_ref, hbt_ref, x_ref, mod_ref, gh_ref, wa_ref, wb_ref, o_ref):
    hs = hf_ref[...] + hb_ref[...]
    parts = []
    for head in range(NH_A):
        sl = slice(head * DH_A, (head + 1) * DH_A)
        hh = hs[:, sl]
        parts.append(hh * lax.rsqrt(jnp.mean(hh * hh, axis=-1, keepdims=True) + EPS))
    hn = jnp.concatenate(parts, axis=-1) * gh_ref[...]
    oa = oa_ref[...].astype(F32)
    ha = (hn * (1.0 / (1.0 + jnp.exp(-oa)))).astype(BF16)
    mixed = jnp.dot(ha, wa_ref[...], preferred_element_type=F32)
    mixed = mixed + jnp.dot(hbt_ref[...], wb_ref[...], preferred_element_type=F32)
    o_ref[...] = x_ref[...] + mod_ref[0, 2:3, :] * mixed


def _out_proj(hf, hb, p, hbt, x2, mod, gh, w_a, w_b, seq):
    t = x2.shape[0]
    tiles_per_seq = seq // TM_OUT
    row = lambda i: (i, 0)
    return pl.pallas_call(
        _out_kernel,
        grid=(t // TM_OUT,),
        in_specs=[pl.BlockSpec((TM_OUT, W_A), row), pl.BlockSpec((TM_OUT, W_A), row),
                  pl.BlockSpec((TM_OUT, W_A), lambda i: (i, 3)),
                  pl.BlockSpec((TM_OUT, W_B), row),
                  pl.BlockSpec((TM_OUT, D_MODEL), row),
                  pl.BlockSpec((1, N_MOD, D_MODEL), lambda i: (i // tiles_per_seq, 0, 0)),
                  pl.BlockSpec((1, W_A), lambda i: (0, 0)),
                  pl.BlockSpec((W_A, D_MODEL), lambda i: (0, 0)),
                  pl.BlockSpec((W_B, D_MODEL), lambda i: (0, 0))],
        out_specs=pl.BlockSpec((TM_OUT, D_MODEL), row),
        out_shape=jax.ShapeDtypeStruct((t, D_MODEL), F32),
        compiler_params=_params(
            ("parallel",),
            [((TM_OUT, W_A), F32)] * 2 + [((TM_OUT, W_A), BF16)] * 2 + [((TM_OUT, D_MODEL), F32)] * 2
            + [((W_A, D_MODEL), BF16)] * 2),
        name="out_proj",
    )(hf, hb, p, hbt, x2, mod, gh, w_a, w_b)


def _up_kernel(xp_ref, x_ref, xn_ref, mod_ref, g_ref, wu_ref, wg_ref, cw_ref, cb_ref, act_ref,
               h_ref, *, tiles_per_seq):
    i = pl.program_id(0)

    @pl.when(pl.program_id(1) == 0)
    def _():
        shift = mod_ref[0, 3:4, :]
        scale = mod_ref[0, 4:5, :]
        gn = g_ref[...]
        h_ref[0:HALO, :] = _rms_mod(xp_ref[...], gn, shift, scale).astype(BF16)
        h_ref[HALO:HALO + TM_UP, :] = _rms_mod(x_ref[...], gn, shift, scale).astype(BF16)
        h_ref[HALO + TM_UP:, :] = _rms_mod(xn_ref[...], gn, shift, scale).astype(BF16)

    u = jnp.dot(h_ref[HALO:HALO + TM_UP, :], wu_ref[...], preferred_element_type=F32)
    g = jnp.dot(h_ref[...], wg_ref[...], preferred_element_type=F32)
    n = TM_UP + 2 * HALO
    row = lax.broadcasted_iota(jnp.int32, (n, 1), 0)
    pos = i % tiles_per_seq
    lo = jnp.where(pos > 0, 0, HALO)
    hi = jnp.where(pos < tiles_per_seq - 1, n, HALO + TM_UP)
    g = jnp.where(jnp.logical_and(row >= lo, row < hi), g, 0.0)
    g_prev = pltpu.roll(g, 1, axis=0)[HALO:HALO + TM_UP, :]
    g_next = pltpu.roll(g, n - 1, axis=0)[HALO:HALO + TM_UP, :]
    gc = (g_prev * cw_ref[0:1, :] + g[HALO:HALO + TM_UP, :] * cw_ref[1:2, :]
          + g_next * cw_ref[2:3, :] + cb_ref[...])
    gelu = 0.5 * gc * (1.0 + lax.erf(gc * (2.0 ** -0.5)))
    act_ref[...] = (gelu * u).astype(BF16)


def _up_proj(x1, mod, g2, w_u, w_g, conv_w, conv_b, seq):
    t = x1.shape[0]
    tiles_per_seq = seq // TM_UP
    hb = TM_UP // HALO
    nh = t // HALO
    return pl.pallas_call(
        functools.partial(_up_kernel, tiles_per_seq=tiles_per_seq),
        grid=(t // TM_UP, D_FF // TN_UP),
        in_specs=[pl.BlockSpec((HALO, D_MODEL), lambda i, j: (jnp.maximum(i * hb - 1, 0), 0)),
                  pl.BlockSpec((TM_UP, D_MODEL), lambda i, j: (i, 0)),
                  pl.BlockSpec((HALO, D_MODEL), lambda i, j: (jnp.minimum((i + 1) * hb, nh - 1), 0)),
                  pl.BlockSpec((1, N_MOD, D_MODEL), lambda i, j: (i // tiles_per_seq, 0, 0)),
                  pl.BlockSpec((1, D_MODEL), lambda i, j: (0, 0)),
                  pl.BlockSpec((D_MODEL, TN_UP), lambda i, j: (0, j)),
                  pl.BlockSpec((D_MODEL, TN_UP), lambda i, j: (0, j)),
                  pl.BlockSpec((8, TN_UP), lambda i, j: (0, j)),
                  pl.BlockSpec((1, TN_UP), lambda i, j: (0, j))],
        out_specs=pl.BlockSpec((TM_UP, TN_UP), lambda i, j: (i, j)),
        out_shape=jax.ShapeDtypeStruct((t, D_FF), BF16),
        scratch_shapes=[pltpu.VMEM((TM_UP + 2 * HALO, D_MODEL), BF16)],
        compiler_params=_params(
            ("parallel", "arbitrary"),
            [((TM_UP + 2 * HALO, D_MODEL), F32)] + [((D_MODEL, TN_UP), BF16)] * 2
            + [((TM_UP, TN_UP), BF16)],
            [((TM_UP + 2 * HALO, D_MODEL), BF16), ((TM_UP + 2 * HALO, TN_UP), F32)]),
        name="up_proj",
    )(x1, x1, x1, mod, g2, w_u, w_g, conv_w, conv_b)


def _down_kernel(a_ref, w_ref, x_ref, mod_ref, g_ref, o_ref, acc_ref):
    k = pl.program_id(1)

    @pl.when(k == 0)
    def _():
        acc_ref[...] = jnp.zeros(acc_ref.shape, F32)

    acc_ref[...] += jnp.dot(a_ref[...], w_ref[...], preferred_element_type=F32)

    @pl.when(k == pl.num_programs(1) - 1)
    def _():
        y = x_ref[...] + mod_ref[0, 5:6, :] * acc_ref[...]
        o_ref[...] = y * lax.rsqrt(jnp.mean(y * y, axis=-1, keepdims=True) + EPS) * g_ref[...]


def _down_proj(act, w_d, x1, mod, gfin, seq):
    t = x1.shape[0]
    tiles_per_seq = seq // TM_DOWN
    return pl.pallas_call(
        _down_kernel,
        grid=(t // TM_DOWN, D_FF // TK_DOWN),
        in_specs=[pl.BlockSpec((TM_DOWN, TK_DOWN), lambda i, k: (i, k)),
                  pl.BlockSpec((TK_DOWN, D_MODEL), lambda i, k: (k, 0)),
                  pl.BlockSpec((TM_DOWN, D_MODEL), lambda i, k: (i, 0)),
                  pl.BlockSpec((1, N_MOD, D_MODEL), lambda i, k: (i // tiles_per_seq, 0, 0)),
                  pl.BlockSpec((1, D_MODEL), lambda i, k: (0, 0))],
        out_specs=pl.BlockSpec((TM_DOWN, D_MODEL), lambda i, k: (i, 0)),
        out_shape=jax.ShapeDtypeStruct((t, D_MODEL), F32),
        scratch_shapes=[pltpu.VMEM((TM_DOWN, D_MODEL), F32)],
        compiler_params=_params(
            ("parallel", "arbitrary"),
            [((TM_DOWN, TK_DOWN), BF16), ((TK_DOWN, D_MODEL), BF16)] + [((TM_DOWN, D_MODEL), F32)] * 2,
            [((TM_DOWN, D_MODEL), F32)]),
        name="down_proj",
    )(act, w_d, x1, mod, gfin)


def kernel(x, c, w_ada, b_ada, g_norm1, w_in, b_gates, g_head_a, rpb, w_out, g_norm2, w_up, conv_w,
           conv_b, w_down, g_final):
    batch, seq, d = x.shape
    assert w_ada.shape[0] == 1
    assert d == D_MODEL and seq % CHUNK_A == 0 and seq % (NA_GROUP * GRID_W) == 0
    x2 = x.reshape(batch * seq, d)
    c8 = jnp.zeros((8, d), F32).at[:batch].set(c)
    mod = _ada(c8, w_ada[0], b_ada)[:batch].reshape(batch, N_MOD, d)

    gate_lo = 4 * W_A
    gate_hi = gate_lo + 4 * NH_A
    w_main = jnp.concatenate([w_in[0][:, :gate_lo], w_in[0][:, gate_hi:]], axis=1).astype(BF16)
    wg = w_in[0][:, gate_lo:gate_hi]
    bg = b_gates[0]
    w_g = jnp.zeros((d, 256), F32)
    w_g = w_g.at[:, 0:4].set(wg[:, 0:4]).at[:, 4:8].set(wg[:, 8:12])
    w_g = w_g.at[:, 128:132].set(wg[:, 4:8]).at[:, 132:136].set(wg[:, 12:16]).astype(BF16)
    b_g = jnp.zeros((1, 256), F32)
    b_g = b_g.at[0, 0:4].set(bg[0:4]).at[0, 4:8].set(bg[8:12])
    b_g = b_g.at[0, 128:132].set(bg[4:8]).at[0, 132:136].set(bg[12:16])

    p, gates = _in_proj(x2, mod, g_norm1, w_main, w_g, b_g, seq)
    colsf, colsb, rowbf, rowbb = _gate_vectors(gates, batch, seq)
    hf, hb = _mlstm(p, colsf, colsb, rowbf, rowbb, batch, seq)
    hbt = _natten(p, _natten_bias(rpb[0]), batch, seq)
    w_o = w_out[0].astype(BF16)
    x1 = _out_proj(hf, hb, p, hbt, x2, mod, g_head_a, w_o[:W_A], w_o[W_A:], seq)

    w_u = w_up[0][:, :D_FF].astype(BF16)
    w_gt = w_up[0][:, D_FF:].astype(BF16)
    cw8 = jnp.zeros((8, D_FF), F32).at[:3].set(conv_w[0])
    act = _up_proj(x1, mod, g_norm2, w_u, w_gt, cw8, conv_b, seq)
    out = _down_proj(act, w_down[0].astype(BF16), x1, mod, g_final[None, :], seq)
    return out.reshape(batch, seq, d)
```

```python
import functools

import jax
import jax.numpy as jnp
import numpy as np
from jax import lax
from jax.experimental import pallas as pl
from jax.experimental.pallas import tpu as pltpu

F32 = jnp.float32
BF16 = jnp.bfloat16

D_MODEL = 2048
GRID_W = 64
NH_A = 4
DH_A = 256
W_A = NH_A * DH_A
NH_B = 8
DH_B = 128
W_B = NH_B * DH_B
WIN_ROWS = 8
WIN_COLS = 16
D_FF = 5632
N_MOD = 6
EPS = 1e-6
NEG_BIG = -1e30

CHUNK_A = 256
LANE_COLA, LANE_A, LANE_E, LANE_W, LANE_DEC = 0, 8, 16, 24, 32

TM_IN = 1024
TN_IN = 1024
TM_OUT = 512
ROW_CHUNK = 256
TM_UP = 512
TN_UP = D_FF // 2
COL_CHUNK = 256
HALO = 16
TM_DOWN = 512
NA_GROUP = 8
NA_LOOKAHEAD = 4

V7X_VMEM_BYTES = 64 * 1024 * 1024
VMEM_TEMP_ALLOWANCE = 12 * 1024 * 1024
DOWN_TEMP_ALLOWANCE = 6 * 1024 * 1024


def _vmem_limit(block_bytes, scratch_bytes=0):
    est = 2 * block_bytes + scratch_bytes + VMEM_TEMP_ALLOWANCE
    return int(min(est, V7X_VMEM_BYTES - 4 * 1024 * 1024))


def _nbytes(shape, dtype):
    n = 1
    for s in shape:
        n *= s
    return n * jnp.dtype(dtype).itemsize


def _params(sem, blocks, scratch=()):
    bb = sum(_nbytes(s, d) for s, d in blocks)
    sb = sum(_nbytes(s, d) for s, d in scratch)
    return pltpu.CompilerParams(dimension_semantics=sem, vmem_limit_bytes=_vmem_limit(bb, sb))


def _rms_mod(x, g, shift, scale):
    y = x * lax.rsqrt(jnp.mean(x * x, axis=-1, keepdims=True) + EPS) * g
    return y * (1.0 + scale) + shift


def _ada_kernel(c_ref, w_ref, b_ref, o_ref):
    c = c_ref[...]
    s = c * (1.0 / (1.0 + jnp.exp(-c)))
    o_ref[...] = jnp.dot(s, w_ref[...], preferred_element_type=F32,
                         precision=lax.Precision.HIGHEST) + b_ref[...]


def _ada(c8, w_ada, b_ada):
    n = w_ada.shape[1]
    tn = 1024
    return pl.pallas_call(
        _ada_kernel,
        grid=(n // tn,),
        in_specs=[pl.BlockSpec((8, D_MODEL), lambda j: (0, 0)),
                  pl.BlockSpec((D_MODEL, tn), lambda j: (0, j)),
                  pl.BlockSpec((1, tn), lambda j: (0, j))],
        out_specs=pl.BlockSpec((8, tn), lambda j: (0, j)),
        out_shape=jax.ShapeDtypeStruct((8, n), F32),
        compiler_params=_params(("arbitrary",), [((D_MODEL, tn), F32), ((8, D_MODEL), F32)]),
        name="ada",
    )(c8, w_ada, b_ada)


def _in_kernel(x_ref, mod_ref, g_ref, w_ref, wg_ref, bg_ref, p_ref, gates_ref, h_ref):
    @pl.when(pl.program_id(1) == 0)
    def _():
        h = _rms_mod(x_ref[...], g_ref[...], mod_ref[0, 0:1, :], mod_ref[0, 1:2, :])
        hb = h.astype(BF16)
        h_ref[...] = hb
        gates_ref[...] = jnp.dot(hb, wg_ref[...], preferred_element_type=F32) + bg_ref[...]

    p_ref[...] = jnp.dot(h_ref[...], w_ref[...], preferred_element_type=F32).astype(BF16)


def _in_proj(x2, mod, g1, w_main, w_g, b_g, seq):
    t = x2.shape[0]
    n = w_main.shape[1]
    tiles_per_seq = seq // TM_IN
    return pl.pallas_call(
        _in_kernel,
        grid=(t // TM_IN, n // TN_IN),
        in_specs=[pl.BlockSpec((TM_IN, D_MODEL), lambda i, j: (i, 0)),
                  pl.BlockSpec((1, N_MOD, D_MODEL), lambda i, j: (i // tiles_per_seq, 0, 0)),
                  pl.BlockSpec((1, D_MODEL), lambda i, j: (0, 0)),
                  pl.BlockSpec((D_MODEL, TN_IN), lambda i, j: (0, j)),
                  pl.BlockSpec((D_MODEL, 256), lambda i, j: (0, 0)),
                  pl.BlockSpec((1, 256), lambda i, j: (0, 0))],
        out_specs=[pl.BlockSpec((TM_IN, TN_IN), lambda i, j: (i, j)),
                   pl.BlockSpec((TM_IN, 256), lambda i, j: (i, 0))],
        out_shape=[jax.ShapeDtypeStruct((t, n), BF16),
                   jax.ShapeDtypeStruct((t, 256), F32)],
        scratch_shapes=[pltpu.VMEM((TM_IN, D_MODEL), BF16)],
        compiler_params=_params(
            ("parallel", "arbitrary"),
            [((TM_IN, D_MODEL), F32), ((D_MODEL, TN_IN), BF16), ((D_MODEL, 256), BF16),
             ((TM_IN, TN_IN), BF16), ((TM_IN, 256), F32)],
            [((TM_IN, D_MODEL), BF16)]),
        name="in_proj",
    )(x2, mod, g1, w_main, w_g, b_g)


def _seg_scan(x, row, op, fill, reverse):
    n = x.shape[0]
    d = 1
    while d < n:
        if reverse:
            y = pltpu.roll(x, n - d, axis=0)
            x = op(x, jnp.where(row < n - d, y, fill))
        else:
            y = pltpu.roll(x, d, axis=0)
            x = op(x, jnp.where(row >= d, y, fill))
        d *= 2
    return x


def _gate_direction(g_ref, m_ref, cols_ref, rowb_ref, reverse):
    n = CHUNK_A
    gi = g_ref[:, 0:128]
    gf = g_ref[:, 128:256]
    lf = jnp.minimum(gf, 0.0) - jnp.log1p(jnp.exp(-jnp.abs(gf)))
    row = lax.broadcasted_iota(jnp.int32, (n, 128), 0)
    lane = lax.broadcasted_iota(jnp.int32, (n, 128), 1)
    bc = _seg_scan(lf, row, jnp.add, 0.0, reverse)
    rb = gi - bc
    cm = _seg_scan(rb, row, jnp.maximum, -jnp.inf, reverse)
    last = 0 if reverse else n - 1
    gsum = bc[last:last + 1, :]
    m_loc = gsum + cm[last:last + 1, :]
    m = m_ref[...]
    m_inter = bc + m
    m_t = jnp.maximum(m_inter, bc + cm)
    m_new = jnp.maximum(gsum + m, m_loc)
    m_ref[...] = m_new
    col_a = bc - m_t
    a = jnp.exp(m_inter - m_t)
    e = jnp.exp(-m_t)
    w = jnp.exp(gsum + rb - m_new)
    dec = jnp.broadcast_to(jnp.exp(gsum + m - m_new), (n, 128))
    keep = lane < 8
    packed = jnp.where(keep, col_a, 0.0)
    for val, off in ((a, LANE_A), (e, LANE_E), (w, LANE_W), (dec, LANE_DEC)):
        packed = packed + pltpu.roll(jnp.where(keep, val, 0.0), off, axis=1)
    cols_ref[...] = packed
    rowb_ref[0] = rb.T[0:8, :]


def _gates_kernel(gf_ref, gb_ref, colsf_ref, colsb_ref, rowbf_ref, rowbb_ref, mf_ref, mb_ref):
    @pl.when(pl.program_id(1) == 0)
    def _():
        mf_ref[...] = jnp.full((1, 128), NEG_BIG, F32)
        mb_ref[...] = jnp.full((1, 128), NEG_BIG, F32)

    _gate_direction(gf_ref, mf_ref, colsf_ref, rowbf_ref, reverse=False)
    _gate_direction(gb_ref, mb_ref, colsb_ref, rowbb_ref, reverse=True)


def _gate_vectors(gates, batch, seq):
    nc = seq // CHUNK_A
    t = gates.shape[0]
    fwd = lambda b, k: (b * nc + k, 0)
    bwd = lambda b, k: (b * nc + nc - 1 - k, 0)
    return pl.pallas_call(
        _gates_kernel,
        grid=(batch, nc),
        in_specs=[pl.BlockSpec((CHUNK_A, 256), fwd), pl.BlockSpec((CHUNK_A, 256), bwd)],
        out_specs=[pl.BlockSpec((CHUNK_A, 128), fwd), pl.BlockSpec((CHUNK_A, 128), bwd),
                   pl.BlockSpec((1, 8, CHUNK_A), lambda b, k: (b, 0, k)),
                   pl.BlockSpec((1, 8, CHUNK_A), lambda b, k: (b, 0, nc - 1 - k))],
        out_shape=[jax.ShapeDtypeStruct((t, 128), F32), jax.ShapeDtypeStruct((t, 128), F32),
                   jax.ShapeDtypeStruct((batch, 8, seq), F32),
                   jax.ShapeDtypeStruct((batch, 8, seq), F32)],
        scratch_shapes=[pltpu.VMEM((1, 128), F32), pltpu.VMEM((1, 128), F32)],
        compiler_params=_params(("parallel", "arbitrary"),
                                [((CHUNK_A, 256), F32)] * 2 + [((CHUNK_A, 128), F32)] * 2),
        name="gate_vectors",
    )(gates, gates)


def _mlstm_kernel(qf, kf, vf, qb, kb, vb, colsf, colsb, rowbf, rowbb, hf, hb, ct_ref):
    @pl.when(pl.program_id(1) == 0)
    def _():
        ct_ref[...] = jnp.zeros(ct_ref.shape, F32)

    n = CHUNK_A
    r = lax.broadcasted_iota(jnp.int32, (n, n), 0)
    c = lax.broadcasted_iota(jnp.int32, (n, n), 1)
    ones_cols = jnp.ones((n, 128), BF16)
    ones_rows = jnp.ones((128, n), BF16)
    seqs = []
    for head in range(NH_A):
        seqs.append((qf, kf, vf, colsf, rowbf, hf, head, head, r >= c))
        seqs.append((qb, kb, vb, colsb, rowbb, hb, head, NH_A + head, r <= c))

    loaded = []
    for q_ref, k_ref, v_ref, _, _, _, head, sid, _ in seqs:
        sl = slice(head * DH_A, (head + 1) * DH_A)
        q = q_ref[:, sl]
        k = k_ref[:, sl] * (DH_A ** -0.5)
        v = v_ref[:, sl]
        ct_old = ct_ref[sid]
        qk = lax.dot_general(q, k, (((1,), (1,)), ((), ())), preferred_element_type=F32)
        qc = lax.dot_general(q, ct_old.astype(BF16), (((1,), (1,)), ((), ())),
                             preferred_element_type=F32)
        loaded.append((k, v, ct_old, qk, qc))

    for (_, _, _, cols_ref, rowb_ref, h_ref, head, sid, mask), (k, v, ct_old, qk, qc) in zip(seqs, loaded):
        sl = slice(head * DH_A, (head + 1) * DH_A)
        col_a = cols_ref[:, LANE_COLA + sid:LANE_COLA + sid + 1]
        a = cols_ref[:, LANE_A + sid:LANE_A + sid + 1]
        e = cols_ref[:, LANE_E + sid:LANE_E + sid + 1]
        row_b = rowb_ref[0, sid:sid + 1, :]
        s = qk * jnp.exp(jnp.where(mask, col_a + row_b, -jnp.inf))
        v_ext = jnp.concatenate([v, ones_cols], axis=1)
        sv = jnp.dot(s.astype(BF16), v_ext, preferred_element_type=F32) + a * qc
        inv = 1.0 / jnp.maximum(jnp.abs(sv[:, DH_A:]), e)
        h_ref[:, sl] = sv[:, :DH_A] * jnp.concatenate([inv, inv], axis=1)

    for (_, _, _, cols_ref, _, _, _, sid, _), (k, v, ct_old, _, _) in zip(seqs, loaded):
        w = cols_ref[:, LANE_W + sid:LANE_W + sid + 1]
        dec = cols_ref[0:1, LANE_DEC + sid:LANE_DEC + sid + 1]
        kw = (w * k.astype(F32)).astype(BF16)
        vt_ext = jnp.concatenate([v.T, ones_rows], axis=0)
        ct_ref[sid] = dec * ct_old + jnp.dot(vt_ext, kw, preferred_element_type=F32)


def _mlstm(p, colsf, colsb, rowbf, rowbb, batch, seq):
    nc = seq // CHUNK_A
    t = p.shape[0]

    def blk(col, rev):
        if rev:
            return pl.BlockSpec((CHUNK_A, W_A), lambda b, k: (b * nc + nc - 1 - k, col))
        return pl.BlockSpec((CHUNK_A, W_A), lambda b, k: (b * nc + k, col))

    colspec = lambda rev: pl.BlockSpec(
        (CHUNK_A, 128), (lambda b, k: (b * nc + nc - 1 - k, 0)) if rev else (lambda b, k: (b * nc + k, 0)))
    rowspec = lambda rev: pl.BlockSpec(
        (1, 8, CHUNK_A), (lambda b, k: (b, 0, nc - 1 - k)) if rev else (lambda b, k: (b, 0, k)))
    return pl.pallas_call(
        _mlstm_kernel,
        grid=(batch, nc),
        in_specs=[blk(0, False), blk(1, False), blk(2, False),
                  blk(0, True), blk(1, True), blk(2, True),
                  colspec(False), colspec(True), rowspec(False), rowspec(True)],
        out_specs=[blk(0, False), blk(0, True)],
        out_shape=[jax.ShapeDtypeStruct((t, W_A), F32), jax.ShapeDtypeStruct((t, W_A), F32)],
        scratch_shapes=[pltpu.VMEM((2 * NH_A, DH_A + 128, DH_A), F32)],
        compiler_params=_params(
            ("parallel", "arbitrary"),
            [((CHUNK_A, W_A), BF16)] * 6 + [((CHUNK_A, 128), F32)] * 2 + [((CHUNK_A, W_A), F32)] * 2,
            [((2 * NH_A, DH_A + 128, DH_A), F32)]),
        name="mlstm",
    )(p, p, p, p, p, p, colsf, colsb, rowbf, rowbb)


def _natten_kernel(q_ref, kp_ref, kc_ref, kn_ref, vp_ref, vc_ref, vn_ref, bias_ref, o_ref,
                   kbuf, vbuf, *, rows):
    g = pl.program_id(1)
    blk = NA_GROUP * GRID_W
    kbuf[0:blk, :] = kp_ref[...]
    kbuf[blk:2 * blk, :] = kc_ref[...]
    kbuf[2 * blk:3 * blk, :] = kn_ref[...]
    vbuf[0:blk, :] = vp_ref[...]
    vbuf[blk:2 * blk, :] = vc_ref[...]
    vbuf[2 * blk:3 * blk, :] = vn_ref[...]
    scale = DH_B ** -0.5
    starts, variants = [], []
    for i in range(NA_GROUP):
        r = g * NA_GROUP + i
        rs = jnp.clip(r - WIN_ROWS // 2, 0, rows - WIN_ROWS)
        starts.append(pl.multiple_of((rs - (g - 1) * NA_GROUP) * GRID_W, GRID_W))
        variants.append(rs - r + (WIN_ROWS - 1))

    def scores(i, h):
        hs = slice(h * DH_B, (h + 1) * DH_B)
        q = q_ref[i * GRID_W:(i + 1) * GRID_W, hs]
        kw = kbuf[pl.ds(starts[i], WIN_ROWS * GRID_W), hs]
        s = lax.dot_general(q, kw, (((1,), (1,)), ((), ())), preferred_element_type=F32)
        s = s * scale + bias_ref[variants[i], h]
        p = jnp.exp(s - jnp.max(s, axis=-1, keepdims=True))
        return p.astype(BF16), 1.0 / jnp.sum(p, axis=-1, keepdims=True)

    def weighted_sum(i, h, p, inv_l):
        hs = slice(h * DH_B, (h + 1) * DH_B)
        vw = vbuf[pl.ds(starts[i], WIN_ROWS * GRID_W), hs]
        o = jnp.dot(p, vw, preferred_element_type=F32)
        o_ref[i * GRID_W:(i + 1) * GRID_W, hs] = (o * inv_l).astype(BF16)

    tiles = [(i, h) for i in range(NA_GROUP) for h in range(NH_B)]
    pending = []
    for idx in range(len(tiles) + NA_LOOKAHEAD):
        if idx < len(tiles):
            pending.append(scores(*tiles[idx]))
        if idx >= NA_LOOKAHEAD:
            weighted_sum(*tiles[idx - NA_LOOKAHEAD], *pending[idx - NA_LOOKAHEAD])
            pending[idx - NA_LOOKAHEAD] = None


def _natten(p, bias, batch, seq):
    rows = seq // GRID_W
    ng = rows // NA_GROUP
    blk = NA_GROUP * GRID_W
    t = p.shape[0]
    qcol, kcol, vcol = 4, 5, 6

    def spec(col, shift):
        def imap(b, g):
            return (b * ng + jnp.clip(g + shift, 0, ng - 1), col)
        return pl.BlockSpec((blk, W_B), imap)

    return pl.pallas_call(
        functools.partial(_natten_kernel, rows=rows),
        grid=(batch, ng),
        in_specs=[spec(qcol, 0),
                  spec(kcol, -1), spec(kcol, 0), spec(kcol, 1),
                  spec(vcol, -1), spec(vcol, 0), spec(vcol, 1),
                  pl.BlockSpec(bias.shape, lambda b, g: (0, 0, 0, 0))],
        out_specs=pl.BlockSpec((blk, W_B), lambda b, g: (b * ng + g, 0)),
        out_shape=jax.ShapeDtypeStruct((t, W_B), BF16),
        scratch_shapes=[pltpu.VMEM((3 * blk, W_B), BF16), pltpu.VMEM((3 * blk, W_B), BF16)],
        compiler_params=_params(
            ("parallel", "arbitrary"),
            [((blk, W_B), BF16)] * 8 + [(bias.shape, F32)],
            [((3 * blk, W_B), BF16)] * 2),
        name="natten",
    )(p, p, p, p, p, p, p, bias)


def _natten_bias(rpb):
    cols = np.arange(GRID_W)
    cstart = np.clip(cols - WIN_COLS // 2, 0, GRID_W - WIN_COLS)
    inside = (cols[None, :] >= cstart[:, None]) & (cols[None, :] < cstart[:, None] + WIN_COLS)
    dc = cols[None, :] - cols[:, None] + (WIN_COLS - 1)
    onehot = (np.arange(2 * WIN_COLS - 1)[:, None, None] == dc[None]) & inside[None]
    toep = jnp.einsum('hdj,jck->hdck', rpb.astype(F32), onehot.astype(np.float32),
                      precision=lax.Precision.HIGHEST)
    toep = toep + np.where(inside, 0.0, NEG_BIG).astype(np.float32)
    tiles = []
    for variant in range(WIN_ROWS):
        t = toep[:, variant:variant + WIN_ROWS].transpose(0, 2, 1, 3)
        tiles.append(t.reshape(NH_B, GRID_W, WIN_ROWS * GRID_W))
    return jnp.stack(tiles, axis=0)


def _out_kernel(hf_ref, hb_ref, oa_ref, hbt_ref, x_ref, mod_ref, gh_ref, gn_ref, wa_ref, wb_ref,
                x1_ref, h2_ref):
    for c in range(TM_OUT // ROW_CHUNK):
        rows = slice(c * ROW_CHUNK, (c + 1) * ROW_CHUNK)
        mixed = jnp.dot(hbt_ref[rows, :], wb_ref[...], preferred_element_type=F32)
        hs = hf_ref[rows, :] + hb_ref[rows, :]
        parts = []
        for head in range(NH_A):
            hh = hs[:, head * DH_A:(head + 1) * DH_A]
            parts.append(hh * lax.rsqrt(jnp.mean(hh * hh, axis=-1, keepdims=True) + EPS))
        hn = jnp.concatenate(parts, axis=-1) * gh_ref[...]
        oa = oa_ref[rows, :].astype(F32)
        ha = (hn * (1.0 / (1.0 + jnp.exp(-oa)))).astype(BF16)
        mixed = mixed + jnp.dot(ha, wa_ref[...], preferred_element_type=F32)
        x1 = x_ref[rows, :] + mod_ref[0, 2:3, :] * mixed
        x1_ref[rows, :] = x1
        h2_ref[rows, :] = _rms_mod(x1, gn_ref[...], mod_ref[0, 3:4, :], mod_ref[0, 4:5, :]).astype(BF16)


def _out_proj(hf, hb, p, hbt, x2, mod, gh, g2, w_o, seq):
    t = x2.shape[0]
    tiles_per_seq = seq // TM_OUT
    row = lambda i: (i, 0)
    resident = pl.Buffered(1)
    return pl.pallas_call(
        _out_kernel,
        grid=(t // TM_OUT,),
        in_specs=[pl.BlockSpec((TM_OUT, W_A), row), pl.BlockSpec((TM_OUT, W_A), row),
                  pl.BlockSpec((TM_OUT, W_A), lambda i: (i, 3)),
                  pl.BlockSpec((TM_OUT, W_B), row),
                  pl.BlockSpec((TM_OUT, D_MODEL), row),
                  pl.BlockSpec((1, N_MOD, D_MODEL), lambda i: (i // tiles_per_seq, 0, 0)),
                  pl.BlockSpec((1, W_A), lambda i: (0, 0)),
                  pl.BlockSpec((1, D_MODEL), lambda i: (0, 0)),
                  pl.BlockSpec((W_A, D_MODEL), lambda i: (0, 0), pipeline_mode=resident),
                  pl.BlockSpec((W_B, D_MODEL), lambda i: (1, 0), pipeline_mode=resident)],
        out_specs=[pl.BlockSpec((TM_OUT, D_MODEL), row), pl.BlockSpec((TM_OUT, D_MODEL), row)],
        out_shape=[jax.ShapeDtypeStruct((t, D_MODEL), F32), jax.ShapeDtypeStruct((t, D_MODEL), BF16)],
        compiler_params=_params(
            ("parallel",),
            [((TM_OUT, W_A), F32)] * 2 + [((TM_OUT, W_A), BF16)] * 2 + [((TM_OUT, D_MODEL), F32)] * 2
            + [((TM_OUT, D_MODEL), BF16)] + [((W_A, D_MODEL), BF16)]),
        name="out_proj",
    )(hf, hb, p, hbt, x2, mod, gh, g2, w_o, w_o)


def _up_kernel(hp_ref, h_ref, hn_ref, wu_ref, wg_ref, cw_ref, cb_ref, act_ref, hext_ref, *,
               tiles_per_seq):
    pos = pl.program_id(1) % tiles_per_seq
    n = TM_UP + 2 * HALO
    hext_ref[0:HALO, :] = jnp.where(pos > 0, hp_ref[...], jnp.zeros_like(hp_ref))
    hext_ref[HALO:HALO + TM_UP, :] = h_ref[...]
    hext_ref[HALO + TM_UP:, :] = jnp.where(pos < tiles_per_seq - 1, hn_ref[...], jnp.zeros_like(hn_ref))
    for c in range(TN_UP // COL_CHUNK):
        cols = slice(c * COL_CHUNK, (c + 1) * COL_CHUNK)
        u = jnp.dot(h_ref[...], wu_ref[:, cols], preferred_element_type=F32)
        g = jnp.dot(hext_ref[...], wg_ref[:, cols], preferred_element_type=F32)
        g_prev = pltpu.roll(g, 1, axis=0)[HALO:HALO + TM_UP, :]
        g_next = pltpu.roll(g, n - 1, axis=0)[HALO:HALO + TM_UP, :]
        gc = (g_prev * cw_ref[0:1, cols] + g[HALO:HALO + TM_UP, :] * cw_ref[1:2, cols]
              + g_next * cw_ref[2:3, cols] + cb_ref[:, cols])
        gelu = 0.5 * gc * (1.0 + lax.erf(gc * (2.0 ** -0.5)))
        act_ref[:, cols] = (gelu * u).astype(BF16)


def _up_proj(h2, w_up, conv_w, conv_b, seq):
    t = h2.shape[0]
    tiles_per_seq = seq // TM_UP
    hb = TM_UP // HALO
    nh = t // HALO
    ngroups = D_FF // TN_UP
    resident = pl.Buffered(1)
    return pl.pallas_call(
        functools.partial(_up_kernel, tiles_per_seq=tiles_per_seq),
        grid=(ngroups, t // TM_UP),
        in_specs=[pl.BlockSpec((HALO, D_MODEL), lambda j, i: (jnp.maximum(i * hb - 1, 0), 0)),
                  pl.BlockSpec((TM_UP, D_MODEL), lambda j, i: (i, 0)),
                  pl.BlockSpec((HALO, D_MODEL), lambda j, i: (jnp.minimum((i + 1) * hb, nh - 1), 0)),
                  pl.BlockSpec((D_MODEL, TN_UP), lambda j, i: (0, j), pipeline_mode=resident),
                  pl.BlockSpec((D_MODEL, TN_UP), lambda j, i: (0, ngroups + j), pipeline_mode=resident),
                  pl.BlockSpec((8, TN_UP), lambda j, i: (0, j)),
                  pl.BlockSpec((1, TN_UP), lambda j, i: (0, j))],
        out_specs=pl.BlockSpec((TM_UP, TN_UP), lambda j, i: (i, j)),
        out_shape=jax.ShapeDtypeStruct((t, D_FF), BF16),
        scratch_shapes=[pltpu.VMEM((TM_UP + 2 * HALO, D_MODEL), BF16)],
        compiler_params=_params(
            ("arbitrary", "arbitrary"),
            [((TM_UP + 2 * HALO, D_MODEL), BF16), ((D_MODEL, TN_UP), BF16), ((TM_UP, TN_UP), BF16)],
            [((TM_UP + 2 * HALO, D_MODEL), BF16)]),
        name="up_proj",
    )(h2, h2, h2, w_up, w_up, conv_w, conv_b)


def _down_kernel(a_ref, w_ref, x_ref, mod_ref, g_ref, o_ref):
    for c in range(TM_DOWN // ROW_CHUNK):
        rows = slice(c * ROW_CHUNK, (c + 1) * ROW_CHUNK)
        y = x_ref[rows, :] + mod_ref[0, 5:6, :] * jnp.dot(a_ref[rows, :], w_ref[...],
                                                           preferred_element_type=F32)
        o_ref[rows, :] = y * lax.rsqrt(jnp.mean(y * y, axis=-1, keepdims=True) + EPS) * g_ref[...]


def _down_proj(act, w_d, x1, mod, gfin, seq):
    t = x1.shape[0]
    tiles_per_seq = seq // TM_DOWN
    return pl.pallas_call(
        _down_kernel,
        grid=(t // TM_DOWN,),
        in_specs=[pl.BlockSpec((TM_DOWN, D_FF), lambda i: (i, 0)),
                  pl.BlockSpec((D_FF, D_MODEL), lambda i: (0, 0), pipeline_mode=pl.Buffered(1)),
                  pl.BlockSpec((TM_DOWN, D_MODEL), lambda i: (i, 0)),
                  pl.BlockSpec((1, N_MOD, D_MODEL), lambda i: (i // tiles_per_seq, 0, 0)),
                  pl.BlockSpec((1, D_MODEL), lambda i: (0, 0))],
        out_specs=pl.BlockSpec((TM_DOWN, D_MODEL), lambda i: (i, 0)),
        out_shape=jax.ShapeDtypeStruct((t, D_MODEL), F32),
        compiler_params=pltpu.CompilerParams(
            dimension_semantics=("parallel",),
            vmem_limit_bytes=_nbytes((D_FF, D_MODEL), BF16) + 2 * (
                _nbytes((TM_DOWN, D_FF), BF16) + 2 * _nbytes((TM_DOWN, D_MODEL), F32))
            + DOWN_TEMP_ALLOWANCE),
        name="down_proj",
    )(act, w_d, x1, mod, gfin)


def kernel(x, c, w_ada, b_ada, g_norm1, w_in, b_gates, g_head_a, rpb, w_out, g_norm2, w_up, conv_w,
           conv_b, w_down, g_final):
    batch, seq, d = x.shape
    assert w_ada.shape[0] == 1
    assert d == D_MODEL and seq % CHUNK_A == 0 and seq % (NA_GROUP * GRID_W) == 0
    x2 = x.reshape(batch * seq, d)
    c8 = jnp.zeros((8, d), F32).at[:batch].set(c)
    mod = _ada(c8, w_ada[0], b_ada)[:batch].reshape(batch, N_MOD, d)

    gate_lo = 4 * W_A
    gate_hi = gate_lo + 4 * NH_A
    w_main = jnp.concatenate([w_in[0][:, :gate_lo].astype(BF16), w_in[0][:, gate_hi:].astype(BF16)],
                             axis=1)
    wg = w_in[0][:, gate_lo:gate_hi]
    bg = b_gates[0]
    w_g = jnp.zeros((d, 256), F32)
    w_g = w_g.at[:, 0:4].set(wg[:, 0:4]).at[:, 4:8].set(wg[:, 8:12])
    w_g = w_g.at[:, 128:132].set(wg[:, 4:8]).at[:, 132:136].set(wg[:, 12:16]).astype(BF16)
    b_g = jnp.zeros((1, 256), F32)
    b_g = b_g.at[0, 0:4].set(bg[0:4]).at[0, 4:8].set(bg[8:12])
    b_g = b_g.at[0, 128:132].set(bg[4:8]).at[0, 132:136].set(bg[12:16])

    p, gates = _in_proj(x2, mod, g_norm1, w_main, w_g, b_g, seq)
    colsf, colsb, rowbf, rowbb = _gate_vectors(gates, batch, seq)
    hf, hb = _mlstm(p, colsf, colsb, rowbf, rowbb, batch, seq)
    hbt = _natten(p, _natten_bias(rpb[0]), batch, seq)
    x1, h2 = _out_proj(hf, hb, p, hbt, x2, mod, g_head_a, g_norm2, w_out[0].astype(BF16), seq)

    cw8 = jnp.zeros((8, D_FF), F32).at[:3].set(conv_w[0])
    act = _up_proj(h2, w_up[0].astype(BF16), cw8, conv_b, seq)
    out = _down_proj(act, w_down[0].astype(BF16), x1, mod, g_final[None, :], seq)
    return out.reshape(batch, seq, d)
```

```python
import functools

import jax
import jax.numpy as jnp
import numpy as np
from jax import lax
from jax.experimental import pallas as pl
from jax.experimental.pallas import tpu as pltpu

F32 = jnp.float32
BF16 = jnp.bfloat16

D_MODEL = 2048
GRID_W = 64
NH_A = 4
DH_A = 256
W_A = NH_A * DH_A
NH_B = 8
DH_B = 128
W_B = NH_B * DH_B
WIN_ROWS = 8
WIN_COLS = 16
D_FF = 5632
N_MOD = 6
EPS = 1e-6
NEG_BIG = -1e30

CHUNK_A = 256
LANE_COLA, LANE_A, LANE_E, LANE_W, LANE_DEC = 0, 8, 16, 24, 32

TM_IN = 1024
TN_IN = 1024
TM_OUT = 512
ROW_CHUNK = 256
TM_UP = 512
TN_UP = D_FF // 2
COL_CHUNK = 256
HALO = 16
TM_DOWN = 512
NA_GROUP = 8
NA_WINDOW = NA_GROUP + WIN_ROWS
NA_LOOKAHEAD = 4

V7X_VMEM_BYTES = 64 * 1024 * 1024
VMEM_TEMP_ALLOWANCE = 12 * 1024 * 1024
DOWN_TEMP_ALLOWANCE = 6 * 1024 * 1024


def _vmem_limit(block_bytes, scratch_bytes=0):
    est = 2 * block_bytes + scratch_bytes + VMEM_TEMP_ALLOWANCE
    return int(min(est, V7X_VMEM_BYTES - 4 * 1024 * 1024))


def _nbytes(shape, dtype):
    n = 1
    for s in shape:
        n *= s
    return n * jnp.dtype(dtype).itemsize


def _params(sem, blocks, scratch=()):
    bb = sum(_nbytes(s, d) for s, d in blocks)
    sb = sum(_nbytes(s, d) for s, d in scratch)
    return pltpu.CompilerParams(dimension_semantics=sem, vmem_limit_bytes=_vmem_limit(bb, sb))


def _rms_mod(x, g, shift, scale):
    y = x * lax.rsqrt(jnp.mean(x * x, axis=-1, keepdims=True) + EPS) * g
    return y * (1.0 + scale) + shift


def _ada_kernel(c_ref, w_ref, b_ref, o_ref):
    c = c_ref[...]
    s = c * (1.0 / (1.0 + jnp.exp(-c)))
    o_ref[...] = jnp.dot(s, w_ref[...], preferred_element_type=F32,
                         precision=lax.Precision.HIGHEST) + b_ref[...]


def _ada(c8, w_ada, b_ada):
    n = w_ada.shape[1]
    tn = 1024
    return pl.pallas_call(
        _ada_kernel,
        grid=(n // tn,),
        in_specs=[pl.BlockSpec((8, D_MODEL), lambda j: (0, 0)),
                  pl.BlockSpec((D_MODEL, tn), lambda j: (0, j)),
                  pl.BlockSpec((1, tn), lambda j: (0, j))],
        out_specs=pl.BlockSpec((8, tn), lambda j: (0, j)),
        out_shape=jax.ShapeDtypeStruct((8, n), F32),
        compiler_params=_params(("arbitrary",), [((D_MODEL, tn), F32), ((8, D_MODEL), F32)]),
        name="ada",
    )(c8, w_ada, b_ada)


def _in_kernel(x_ref, mod_ref, g_ref, w_ref, wg_ref, bg_ref, p_ref, gates_ref, h_ref):
    @pl.when(pl.program_id(1) == 0)
    def _():
        h = _rms_mod(x_ref[...], g_ref[...], mod_ref[0, 0:1, :], mod_ref[0, 1:2, :])
        hb = h.astype(BF16)
        h_ref[...] = hb
        gates_ref[...] = jnp.dot(hb, wg_ref[...], preferred_element_type=F32) + bg_ref[...]

    p_ref[...] = jnp.dot(h_ref[...], w_ref[...], preferred_element_type=F32).astype(BF16)


def _in_proj(x2, mod, g1, w_main, w_g, b_g, seq):
    t = x2.shape[0]
    n = w_main.shape[1]
    tiles_per_seq = seq // TM_IN
    return pl.pallas_call(
        _in_kernel,
        grid=(t // TM_IN, n // TN_IN),
        in_specs=[pl.BlockSpec((TM_IN, D_MODEL), lambda i, j: (i, 0)),
                  pl.BlockSpec((1, N_MOD, D_MODEL), lambda i, j: (i // tiles_per_seq, 0, 0)),
                  pl.BlockSpec((1, D_MODEL), lambda i, j: (0, 0)),
                  pl.BlockSpec((D_MODEL, TN_IN), lambda i, j: (0, j)),
                  pl.BlockSpec((D_MODEL, 256), lambda i, j: (0, 0)),
                  pl.BlockSpec((1, 256), lambda i, j: (0, 0))],
        out_specs=[pl.BlockSpec((None, TM_IN, TN_IN), lambda i, j: (j, i, 0)),
                   pl.BlockSpec((TM_IN, 256), lambda i, j: (i, 0))],
        out_shape=[jax.ShapeDtypeStruct((n // TN_IN, t, TN_IN), BF16),
                   jax.ShapeDtypeStruct((t, 256), F32)],
        scratch_shapes=[pltpu.VMEM((TM_IN, D_MODEL), BF16)],
        compiler_params=_params(
            ("parallel", "arbitrary"),
            [((TM_IN, D_MODEL), F32), ((D_MODEL, TN_IN), BF16), ((D_MODEL, 256), BF16),
             ((TM_IN, TN_IN), BF16), ((TM_IN, 256), F32)],
            [((TM_IN, D_MODEL), BF16)]),
        name="in_proj",
    )(x2, mod, g1, w_main, w_g, b_g)


def _seg_scan(x, row, op, fill, reverse):
    n = x.shape[0]
    d = 1
    while d < n:
        if reverse:
            y = pltpu.roll(x, n - d, axis=0)
            x = op(x, jnp.where(row < n - d, y, fill))
        else:
            y = pltpu.roll(x, d, axis=0)
            x = op(x, jnp.where(row >= d, y, fill))
        d *= 2
    return x


def _gate_direction(g_ref, m_ref, cols_ref, rowb_ref, reverse):
    n = CHUNK_A
    gi = g_ref[:, 0:128]
    gf = g_ref[:, 128:256]
    lf = jnp.minimum(gf, 0.0) - jnp.log1p(jnp.exp(-jnp.abs(gf)))
    row = lax.broadcasted_iota(jnp.int32, (n, 128), 0)
    lane = lax.broadcasted_iota(jnp.int32, (n, 128), 1)
    bc = _seg_scan(lf, row, jnp.add, 0.0, reverse)
    rb = gi - bc
    cm = _seg_scan(rb, row, jnp.maximum, -jnp.inf, reverse)
    last = 0 if reverse else n - 1
    gsum = bc[last:last + 1, :]
    m_loc = gsum + cm[last:last + 1, :]
    m = m_ref[...]
    m_inter = bc + m
    m_t = jnp.maximum(m_inter, bc + cm)
    m_new = jnp.maximum(gsum + m, m_loc)
    m_ref[...] = m_new
    col_a = bc - m_t
    a = jnp.exp(m_inter - m_t)
    e = jnp.exp(-m_t)
    w = jnp.exp(gsum + rb - m_new)
    dec = jnp.broadcast_to(jnp.exp(gsum + m - m_new), (n, 128))
    keep = lane < 8
    packed = jnp.where(keep, col_a, 0.0)
    for val, off in ((a, LANE_A), (e, LANE_E), (w, LANE_W), (dec, LANE_DEC)):
        packed = packed + pltpu.roll(jnp.where(keep, val, 0.0), off, axis=1)
    cols_ref[...] = packed
    rowb_ref[0] = rb.T[0:8, :]


def _gates_kernel(gf_ref, gb_ref, colsf_ref, colsb_ref, rowbf_ref, rowbb_ref, mf_ref, mb_ref):
    @pl.when(pl.program_id(1) == 0)
    def _():
        mf_ref[...] = jnp.full((1, 128), NEG_BIG, F32)
        mb_ref[...] = jnp.full((1, 128), NEG_BIG, F32)

    _gate_direction(gf_ref, mf_ref, colsf_ref, rowbf_ref, reverse=False)
    _gate_direction(gb_ref, mb_ref, colsb_ref, rowbb_ref, reverse=True)


def _gate_vectors(gates, batch, seq):
    nc = seq // CHUNK_A
    t = gates.shape[0]
    fwd = lambda b, k: (b * nc + k, 0)
    bwd = lambda b, k: (b * nc + nc - 1 - k, 0)
    return pl.pallas_call(
        _gates_kernel,
        grid=(batch, nc),
        in_specs=[pl.BlockSpec((CHUNK_A, 256), fwd), pl.BlockSpec((CHUNK_A, 256), bwd)],
        out_specs=[pl.BlockSpec((CHUNK_A, 128), fwd), pl.BlockSpec((CHUNK_A, 128), bwd),
                   pl.BlockSpec((1, 8, CHUNK_A), lambda b, k: (b, 0, k)),
                   pl.BlockSpec((1, 8, CHUNK_A), lambda b, k: (b, 0, nc - 1 - k))],
        out_shape=[jax.ShapeDtypeStruct((t, 128), F32), jax.ShapeDtypeStruct((t, 128), F32),
                   jax.ShapeDtypeStruct((batch, 8, seq), F32),
                   jax.ShapeDtypeStruct((batch, 8, seq), F32)],
        scratch_shapes=[pltpu.VMEM((1, 128), F32), pltpu.VMEM((1, 128), F32)],
        compiler_params=_params(("parallel", "arbitrary"),
                                [((CHUNK_A, 256), F32)] * 2 + [((CHUNK_A, 128), F32)] * 2),
        name="gate_vectors",
    )(gates, gates)


def _mlstm_kernel(qf, kf, vf, qb, kb, vb, colsf, colsb, rowbf, rowbb, hf, hb, ct_ref):
    @pl.when(pl.program_id(1) == 0)
    def _():
        ct_ref[...] = jnp.zeros(ct_ref.shape, F32)

    n = CHUNK_A
    r = lax.broadcasted_iota(jnp.int32, (n, n), 0)
    c = lax.broadcasted_iota(jnp.int32, (n, n), 1)
    ones_cols = jnp.ones((n, 128), BF16)
    ones_rows = jnp.ones((128, n), BF16)
    seqs = []
    for head in range(NH_A):
        seqs.append((qf, kf, vf, colsf, rowbf, hf, head, head, r >= c))
        seqs.append((qb, kb, vb, colsb, rowbb, hb, head, NH_A + head, r <= c))

    loaded = []
    for q_ref, k_ref, v_ref, _, _, _, head, sid, _ in seqs:
        sl = slice(head * DH_A, (head + 1) * DH_A)
        q = q_ref[:, sl]
        k = k_ref[:, sl] * (DH_A ** -0.5)
        v = v_ref[:, sl]
        ct_old = ct_ref[sid]
        qk = lax.dot_general(q, k, (((1,), (1,)), ((), ())), preferred_element_type=F32)
        qc = lax.dot_general(q, ct_old.astype(BF16), (((1,), (1,)), ((), ())),
                             preferred_element_type=F32)
        loaded.append((k, v, ct_old, qk, qc))

    for (_, _, _, cols_ref, rowb_ref, h_ref, head, sid, mask), (k, v, ct_old, qk, qc) in zip(seqs, loaded):
        sl = slice(head * DH_A, (head + 1) * DH_A)
        col_a = cols_ref[:, LANE_COLA + sid:LANE_COLA + sid + 1]
        a = cols_ref[:, LANE_A + sid:LANE_A + sid + 1]
        e = cols_ref[:, LANE_E + sid:LANE_E + sid + 1]
        row_b = rowb_ref[0, sid:sid + 1, :]
        s = qk * jnp.exp(jnp.where(mask, col_a + row_b, -jnp.inf))
        v_ext = jnp.concatenate([v, ones_cols], axis=1)
        sv = jnp.dot(s.astype(BF16), v_ext, preferred_element_type=F32) + a * qc
        inv = 1.0 / jnp.maximum(jnp.abs(sv[:, DH_A:]), e)
        h_ref[:, sl] = (sv[:, :DH_A] * jnp.concatenate([inv, inv], axis=1)).astype(BF16)

    for (_, _, _, cols_ref, _, _, _, sid, _), (k, v, ct_old, _, _) in zip(seqs, loaded):
        w = cols_ref[:, LANE_W + sid:LANE_W + sid + 1]
        dec = cols_ref[0:1, LANE_DEC + sid:LANE_DEC + sid + 1]
        kw = (w * k.astype(F32)).astype(BF16)
        vt_ext = jnp.concatenate([v.T, ones_rows], axis=0)
        ct_ref[sid] = dec * ct_old + jnp.dot(vt_ext, kw, preferred_element_type=F32)


def _mlstm(p, colsf, colsb, rowbf, rowbb, batch, seq):
    nc = seq // CHUNK_A
    t = p.shape[1]

    def blk(col, rev):
        if rev:
            return pl.BlockSpec((None, CHUNK_A, W_A), lambda b, k: (col, b * nc + nc - 1 - k, 0))
        return pl.BlockSpec((None, CHUNK_A, W_A), lambda b, k: (col, b * nc + k, 0))

    def outblk(rev):
        if rev:
            return pl.BlockSpec((CHUNK_A, W_A), lambda b, k: (b * nc + nc - 1 - k, 0))
        return pl.BlockSpec((CHUNK_A, W_A), lambda b, k: (b * nc + k, 0))

    colspec = lambda rev: pl.BlockSpec(
        (CHUNK_A, 128), (lambda b, k: (b * nc + nc - 1 - k, 0)) if rev else (lambda b, k: (b * nc + k, 0)))
    rowspec = lambda rev: pl.BlockSpec(
        (1, 8, CHUNK_A), (lambda b, k: (b, 0, nc - 1 - k)) if rev else (lambda b, k: (b, 0, k)))
    return pl.pallas_call(
        _mlstm_kernel,
        grid=(batch, nc),
        in_specs=[blk(0, False), blk(1, False), blk(2, False),
                  blk(0, True), blk(1, True), blk(2, True),
                  colspec(False), colspec(True), rowspec(False), rowspec(True)],
        out_specs=[outblk(False), outblk(True)],
        out_shape=[jax.ShapeDtypeStruct((t, W_A), BF16), jax.ShapeDtypeStruct((t, W_A), BF16)],
        scratch_shapes=[pltpu.VMEM((2 * NH_A, DH_A + 128, DH_A), F32)],
        compiler_params=_params(
            ("parallel", "arbitrary"),
            [((CHUNK_A, W_A), BF16)] * 8 + [((CHUNK_A, 128), F32)] * 2,
            [((2 * NH_A, DH_A + 128, DH_A), F32)]),
        name="mlstm",
    )(p, p, p, p, p, p, colsf, colsb, rowbf, rowbb)


def _na_window_start(g, rows):
    return jnp.clip(g * NA_GROUP - WIN_ROWS // 2, 0, rows - NA_WINDOW)


def _natten_kernel(q_ref, kbuf, vbuf, bias_ref, o_ref, *, rows):
    g = pl.program_id(1)
    wstart = _na_window_start(g, rows)
    scale = DH_B ** -0.5
    starts, variants = [], []
    for i in range(NA_GROUP):
        r = g * NA_GROUP + i
        rs = jnp.clip(r - WIN_ROWS // 2, 0, rows - WIN_ROWS)
        starts.append(pl.multiple_of((rs - wstart) * GRID_W, GRID_W))
        variants.append(rs - r + (WIN_ROWS - 1))

    def scores(i, h):
        hs = slice(h * DH_B, (h + 1) * DH_B)
        q = q_ref[i * GRID_W:(i + 1) * GRID_W, hs]
        kw = kbuf[pl.ds(starts[i], WIN_ROWS * GRID_W), hs]
        s = lax.dot_general(q, kw, (((1,), (1,)), ((), ())), preferred_element_type=F32)
        s = s * scale + bias_ref[variants[i], h]
        p = jnp.exp(s - jnp.max(s, axis=-1, keepdims=True))
        return p.astype(BF16), 1.0 / jnp.sum(p, axis=-1, keepdims=True)

    def weighted_sum(i, h, p, inv_l):
        hs = slice(h * DH_B, (h + 1) * DH_B)
        vw = vbuf[pl.ds(starts[i], WIN_ROWS * GRID_W), hs]
        o = jnp.dot(p, vw, preferred_element_type=F32)
        o_ref[i * GRID_W:(i + 1) * GRID_W, hs] = (o * inv_l).astype(BF16)

    tiles = [(i, h) for i in range(NA_GROUP) for h in range(NH_B)]
    pending = []
    for idx in range(len(tiles) + NA_LOOKAHEAD):
        if idx < len(tiles):
            pending.append(scores(*tiles[idx]))
        if idx >= NA_LOOKAHEAD:
            weighted_sum(*tiles[idx - NA_LOOKAHEAD], *pending[idx - NA_LOOKAHEAD])
            pending[idx - NA_LOOKAHEAD] = None


def _natten(p, bias, batch, seq):
    rows = seq // GRID_W
    ng = rows // NA_GROUP
    blk = NA_GROUP * GRID_W
    win = NA_WINDOW * GRID_W
    t = p.shape[1]
    qcol, kcol, vcol = 4, 5, 6

    def window(col):
        def imap(b, g):
            return (col, (b * rows + _na_window_start(g, rows)) * GRID_W, 0)
        return pl.BlockSpec((None, pl.Element(win), pl.Element(W_B)), imap)

    return pl.pallas_call(
        functools.partial(_natten_kernel, rows=rows),
        grid=(batch, ng),
        in_specs=[pl.BlockSpec((None, blk, W_B), lambda b, g: (qcol, b * ng + g, 0)),
                  window(kcol), window(vcol),
                  pl.BlockSpec(bias.shape, lambda b, g: (0, 0, 0, 0))],
        out_specs=pl.BlockSpec((blk, W_B), lambda b, g: (b * ng + g, 0)),
        out_shape=jax.ShapeDtypeStruct((t, W_B), BF16),
        compiler_params=_params(
            ("parallel", "arbitrary"),
            [((blk, W_B), BF16)] * 2 + [((win, W_B), BF16)] * 2 + [(bias.shape, F32)]),
        name="natten",
    )(p, p, p, bias)


def _natten_bias(rpb):
    cols = np.arange(GRID_W)
    cstart = np.clip(cols - WIN_COLS // 2, 0, GRID_W - WIN_COLS)
    inside = (cols[None, :] >= cstart[:, None]) & (cols[None, :] < cstart[:, None] + WIN_COLS)
    dc = cols[None, :] - cols[:, None] + (WIN_COLS - 1)
    onehot = (np.arange(2 * WIN_COLS - 1)[:, None, None] == dc[None]) & inside[None]
    toep = jnp.einsum('hdj,jck->hdck', rpb.astype(F32), onehot.astype(np.float32),
                      precision=lax.Precision.HIGHEST)
    toep = toep + np.where(inside, 0.0, NEG_BIG).astype(np.float32)
    tiles = []
    for variant in range(WIN_ROWS):
        t = toep[:, variant:variant + WIN_ROWS].transpose(0, 2, 1, 3)
        tiles.append(t.reshape(NH_B, GRID_W, WIN_ROWS * GRID_W))
    return jnp.stack(tiles, axis=0)


def _out_kernel(hf_ref, hb_ref, oa_ref, hbt_ref, x_ref, mod_ref, gh_ref, gn_ref, wa_ref, wb_ref,
                x1_ref, h2_ref):
    for c in range(TM_OUT // ROW_CHUNK):
        rows = slice(c * ROW_CHUNK, (c + 1) * ROW_CHUNK)
        mixed = jnp.dot(hbt_ref[rows, :], wb_ref[...], preferred_element_type=F32)
        hs = hf_ref[rows, :].astype(F32) + hb_ref[rows, :].astype(F32)
        parts = []
        for head in range(NH_A):
            hh = hs[:, head * DH_A:(head + 1) * DH_A]
            parts.append(hh * lax.rsqrt(jnp.mean(hh * hh, axis=-1, keepdims=True) + EPS))
        hn = jnp.concatenate(parts, axis=-1) * gh_ref[...]
        oa = oa_ref[rows, :].astype(F32)
        ha = (hn * (1.0 / (1.0 + jnp.exp(-oa)))).astype(BF16)
        mixed = mixed + jnp.dot(ha, wa_ref[...], preferred_element_type=F32)
        x1 = x_ref[rows, :] + mod_ref[0, 2:3, :] * mixed
        x1_ref[rows, :] = x1
        h2_ref[rows, :] = _rms_mod(x1, gn_ref[...], mod_ref[0, 3:4, :], mod_ref[0, 4:5, :]).astype(BF16)


def _out_proj(hf, hb, p, hbt, x2, mod, gh, g2, w_o, seq):
    t = x2.shape[0]
    tiles_per_seq = seq // TM_OUT
    row = lambda i: (i, 0)
    resident = pl.Buffered(1)
    return pl.pallas_call(
        _out_kernel,
        grid=(t // TM_OUT,),
        in_specs=[pl.BlockSpec((TM_OUT, W_A), row), pl.BlockSpec((TM_OUT, W_A), row),
                  pl.BlockSpec((None, TM_OUT, W_A), lambda i: (3, i, 0)),
                  pl.BlockSpec((TM_OUT, W_B), row),
                  pl.BlockSpec((TM_OUT, D_MODEL), row),
                  pl.BlockSpec((1, N_MOD, D_MODEL), lambda i: (i // tiles_per_seq, 0, 0)),
                  pl.BlockSpec((1, W_A), lambda i: (0, 0)),
                  pl.BlockSpec((1, D_MODEL), lambda i: (0, 0)),
                  pl.BlockSpec((W_A, D_MODEL), lambda i: (0, 0), pipeline_mode=resident),
                  pl.BlockSpec((W_B, D_MODEL), lambda i: (1, 0), pipeline_mode=resident)],
        out_specs=[pl.BlockSpec((TM_OUT, D_MODEL), row), pl.BlockSpec((TM_OUT, D_MODEL), row)],
        out_shape=[jax.ShapeDtypeStruct((t, D_MODEL), F32), jax.ShapeDtypeStruct((t, D_MODEL), BF16)],
        compiler_params=_params(
            ("parallel",),
            [((TM_OUT, W_A), BF16)] * 4 + [((TM_OUT, D_MODEL), F32)] * 2
            + [((TM_OUT, D_MODEL), BF16)] + [((W_A, D_MODEL), BF16)]),
        name="out_proj",
    )(hf, hb, p, hbt, x2, mod, gh, g2, w_o, w_o)


def _up_kernel(hp_ref, h_ref, hn_ref, wu_ref, wg_ref, cw_ref, cb_ref, act_ref, hext_ref, *,
               tiles_per_seq):
    pos = pl.program_id(1) % tiles_per_seq
    n = TM_UP + 2 * HALO
    hext_ref[0:HALO, :] = jnp.where(pos > 0, hp_ref[...], jnp.zeros_like(hp_ref))
    hext_ref[HALO:HALO + TM_UP, :] = h_ref[...]
    hext_ref[HALO + TM_UP:, :] = jnp.where(pos < tiles_per_seq - 1, hn_ref[...], jnp.zeros_like(hn_ref))
    for c in range(TN_UP // COL_CHUNK):
        cols = slice(c * COL_CHUNK, (c + 1) * COL_CHUNK)
        u = jnp.dot(h_ref[...], wu_ref[:, cols], preferred_element_type=F32)
        g = jnp.dot(hext_ref[...], wg_ref[:, cols], preferred_element_type=F32)
        g_prev = pltpu.roll(g, 1, axis=0)[HALO:HALO + TM_UP, :]
        g_next = pltpu.roll(g, n - 1, axis=0)[HALO:HALO + TM_UP, :]
        gc = (g_prev * cw_ref[0:1, cols] + g[HALO:HALO + TM_UP, :] * cw_ref[1:2, cols]
              + g_next * cw_ref[2:3, cols] + cb_ref[:, cols])
        gelu = 0.5 * gc * (1.0 + lax.erf(gc * (2.0 ** -0.5)))
        act_ref[:, cols] = (gelu * u).astype(BF16)


def _up_proj(h2, w_up, conv_w, conv_b, seq):
    t = h2.shape[0]
    tiles_per_seq = seq // TM_UP
    hb = TM_UP // HALO
    nh = t // HALO
    ngroups = D_FF // TN_UP
    resident = pl.Buffered(1)
    return pl.pallas_call(
        functools.partial(_up_kernel, tiles_per_seq=tiles_per_seq),
        grid=(ngroups, t // TM_UP),
        in_specs=[pl.BlockSpec((HALO, D_MODEL), lambda j, i: (jnp.maximum(i * hb - 1, 0), 0)),
                  pl.BlockSpec((TM_UP, D_MODEL), lambda j, i: (i, 0)),
                  pl.BlockSpec((HALO, D_MODEL), lambda j, i: (jnp.minimum((i + 1) * hb, nh - 1), 0)),
                  pl.BlockSpec((D_MODEL, TN_UP), lambda j, i: (0, j), pipeline_mode=resident),
                  pl.BlockSpec((D_MODEL, TN_UP), lambda j, i: (0, ngroups + j), pipeline_mode=resident),
                  pl.BlockSpec((8, TN_UP), lambda j, i: (0, j)),
                  pl.BlockSpec((1, TN_UP), lambda j, i: (0, j))],
        out_specs=pl.BlockSpec((TM_UP, TN_UP), lambda j, i: (i, j)),
        out_shape=jax.ShapeDtypeStruct((t, D_FF), BF16),
        scratch_shapes=[pltpu.VMEM((TM_UP + 2 * HALO, D_MODEL), BF16)],
        compiler_params=_params(
            ("arbitrary", "arbitrary"),
            [((TM_UP + 2 * HALO, D_MODEL), BF16), ((D_MODEL, TN_UP), BF16), ((TM_UP, TN_UP), BF16)],
            [((TM_UP + 2 * HALO, D_MODEL), BF16)]),
        name="up_proj",
    )(h2, h2, h2, w_up, w_up, conv_w, conv_b)


def _down_kernel(a_ref, w_ref, x_ref, mod_ref, g_ref, o_ref):
    for c in range(TM_DOWN // ROW_CHUNK):
        rows = slice(c * ROW_CHUNK, (c + 1) * ROW_CHUNK)
        y = x_ref[rows, :] + mod_ref[0, 5:6, :] * jnp.dot(a_ref[rows, :], w_ref[...],
                                                           preferred_element_type=F32)
        o_ref[rows, :] = y * lax.rsqrt(jnp.mean(y * y, axis=-1, keepdims=True) + EPS) * g_ref[...]


def _down_proj(act, w_d, x1, mod, gfin, seq):
    t = x1.shape[0]
    tiles_per_seq = seq // TM_DOWN
    return pl.pallas_call(
        _down_kernel,
        grid=(t // TM_DOWN,),
        in_specs=[pl.BlockSpec((TM_DOWN, D_FF), lambda i: (i, 0)),
                  pl.BlockSpec((D_FF, D_MODEL), lambda i: (0, 0), pipeline_mode=pl.Buffered(1)),
                  pl.BlockSpec((TM_DOWN, D_MODEL), lambda i: (i, 0)),
                  pl.BlockSpec((1, N_MOD, D_MODEL), lambda i: (i // tiles_per_seq, 0, 0)),
                  pl.BlockSpec((1, D_MODEL), lambda i: (0, 0))],
        out_specs=pl.BlockSpec((TM_DOWN, D_MODEL), lambda i: (i, 0)),
        out_shape=jax.ShapeDtypeStruct((t, D_MODEL), F32),
        compiler_params=pltpu.CompilerParams(
            dimension_semantics=("parallel",),
            vmem_limit_bytes=_nbytes((D_FF, D_MODEL), BF16) + 2 * (
                _nbytes((TM_DOWN, D_FF), BF16) + 2 * _nbytes((TM_DOWN, D_MODEL), F32))
            + DOWN_TEMP_ALLOWANCE),
        name="down_proj",
    )(act, w_d, x1, mod, gfin)


def kernel(x, c, w_ada, b_ada, g_norm1, w_in, b_gates, g_head_a, rpb, w_out, g_norm2, w_up, conv_w,
           conv_b, w_down, g_final):
    batch, seq, d = x.shape
    assert w_ada.shape[0] == 1
    assert d == D_MODEL and seq % CHUNK_A == 0 and seq % (NA_GROUP * GRID_W) == 0
    x2 = x.reshape(batch * seq, d)
    c8 = jnp.zeros((8, d), F32).at[:batch].set(c)
    mod = _ada(c8, w_ada[0], b_ada)[:batch].reshape(batch, N_MOD, d)

    gate_lo = 4 * W_A
    gate_hi = gate_lo + 4 * NH_A
    w_main = jnp.concatenate([w_in[0][:, :gate_lo].astype(BF16), w_in[0][:, gate_hi:].astype(BF16)],
                             axis=1)
    wg = w_in[0][:, gate_lo:gate_hi]
    bg = b_gates[0]
    w_g = jnp.zeros((d, 256), F32)
    w_g = w_g.at[:, 0:4].set(wg[:, 0:4]).at[:, 4:8].set(wg[:, 8:12])
    w_g = w_g.at[:, 128:132].set(wg[:, 4:8]).at[:, 132:136].set(wg[:, 12:16]).astype(BF16)
    b_g = jnp.zeros((1, 256), F32)
    b_g = b_g.at[0, 0:4].set(bg[0:4]).at[0, 4:8].set(bg[8:12])
    b_g = b_g.at[0, 128:132].set(bg[4:8]).at[0, 132:136].set(bg[12:16])

    p, gates = _in_proj(x2, mod, g_norm1, w_main, w_g, b_g, seq)
    colsf, colsb, rowbf, rowbb = _gate_vectors(gates, batch, seq)
    hf, hb = _mlstm(p, colsf, colsb, rowbf, rowbb, batch, seq)
    hbt = _natten(p, _natten_bias(rpb[0]), batch, seq)
    x1, h2 = _out_proj(hf, hb, p, hbt, x2, mod, g_head_a, g_norm2, w_out[0].astype(BF16), seq)

    cw8 = jnp.zeros((8, D_FF), F32).at[:3].set(conv_w[0])
    act = _up_proj(h2, w_up[0].astype(BF16), cw8, conv_b, seq)
    out = _down_proj(act, w_down[0].astype(BF16), x1, mod, g_final[None, :], seq)
    return out.reshape(batch, seq, d)
```

```python
import functools

import jax
import jax.numpy as jnp
import numpy as np
from jax import lax
from jax.experimental import pallas as pl
from jax.experimental.pallas import tpu as pltpu

F32 = jnp.float32
BF16 = jnp.bfloat16

D_MODEL = 2048
GRID_W = 64
NH_A = 4
DH_A = 256
W_A = NH_A * DH_A
NH_B = 8
DH_B = 128
W_B = NH_B * DH_B
WIN_ROWS = 8
WIN_COLS = 16
D_FF = 5632
N_MOD = 6
EPS = 1e-6
NEG_BIG = -1e30

CHUNK_A = 256
LANE_COLA, LANE_A, LANE_E, LANE_W, LANE_DEC = 0, 8, 16, 24, 32

TM_IN = 1024
TN_IN = 1024
TM_OUT = 512
ROW_CHUNK = 256
TM_UP = 1024
TN_UP = D_FF // 2
COL_CHUNK = 256
HALO = 16
TM_DOWN = 512
NA_GROUP = 8
NA_WINDOW = NA_GROUP + WIN_ROWS
NA_LOOKAHEAD = 4

V7X_VMEM_BYTES = 64 * 1024 * 1024
VMEM_TEMP_ALLOWANCE = 12 * 1024 * 1024
DOWN_TEMP_ALLOWANCE = 6 * 1024 * 1024


def _vmem_limit(block_bytes, scratch_bytes=0):
    est = 2 * block_bytes + scratch_bytes + VMEM_TEMP_ALLOWANCE
    return int(min(est, V7X_VMEM_BYTES - 4 * 1024 * 1024))


def _nbytes(shape, dtype):
    n = 1
    for s in shape:
        n *= s
    return n * jnp.dtype(dtype).itemsize


def _params(sem, blocks, scratch=()):
    bb = sum(_nbytes(s, d) for s, d in blocks)
    sb = sum(_nbytes(s, d) for s, d in scratch)
    return pltpu.CompilerParams(dimension_semantics=sem, vmem_limit_bytes=_vmem_limit(bb, sb))


def _rms_mod(x, g, shift, scale):
    y = x * lax.rsqrt(jnp.mean(x * x, axis=-1, keepdims=True) + EPS) * g
    return y * (1.0 + scale) + shift


def _ada_kernel(c_ref, w_ref, b_ref, o_ref):
    c = c_ref[...]
    s = c * (1.0 / (1.0 + jnp.exp(-c)))
    o_ref[...] = jnp.dot(s, w_ref[...], preferred_element_type=F32,
                         precision=lax.Precision.HIGHEST) + b_ref[...]


def _ada(c8, w_ada, b_ada):
    n = w_ada.shape[1]
    tn = 1024
    return pl.pallas_call(
        _ada_kernel,
        grid=(n // tn,),
        in_specs=[pl.BlockSpec((8, D_MODEL), lambda j: (0, 0)),
                  pl.BlockSpec((D_MODEL, tn), lambda j: (0, j)),
                  pl.BlockSpec((1, tn), lambda j: (0, j))],
        out_specs=pl.BlockSpec((8, tn), lambda j: (0, j)),
        out_shape=jax.ShapeDtypeStruct((8, n), F32),
        compiler_params=_params(("arbitrary",), [((D_MODEL, tn), F32), ((8, D_MODEL), F32)]),
        name="ada",
    )(c8, w_ada, b_ada)


GATE_LO = 4 * W_A
GATE_HI = GATE_LO + 4 * NH_A
TM_PREP = 256


def _win_prep_kernel(w_ref, o_ref):
    o_ref[:, :GATE_LO] = w_ref[:, :GATE_LO].astype(BF16)
    o_ref[:, GATE_LO:] = w_ref[:, GATE_HI:].astype(BF16)


def _win_prep(w):
    rows, cols = w.shape
    n = cols - (GATE_HI - GATE_LO)
    return pl.pallas_call(
        _win_prep_kernel,
        grid=(rows // TM_PREP,),
        in_specs=[pl.BlockSpec((TM_PREP, cols), lambda i: (i, 0))],
        out_specs=pl.BlockSpec((TM_PREP, n), lambda i: (i, 0)),
        out_shape=jax.ShapeDtypeStruct((rows, n), BF16),
        compiler_params=_params(("parallel",), [((TM_PREP, cols), F32), ((TM_PREP, n), BF16)]),
        name="w_in_prep",
    )(w)


def _in_kernel(x_ref, mod_ref, g_ref, w_ref, wg_ref, bg_ref, p_ref, gates_ref, h_ref):
    @pl.when(pl.program_id(1) == 0)
    def _():
        h = _rms_mod(x_ref[...], g_ref[...], mod_ref[0, 0:1, :], mod_ref[0, 1:2, :])
        hb = h.astype(BF16)
        h_ref[...] = hb
        gates_ref[...] = jnp.dot(hb, wg_ref[...], preferred_element_type=F32) + bg_ref[...]

    p_ref[...] = jnp.dot(h_ref[...], w_ref[...], preferred_element_type=F32).astype(BF16)


def _in_proj(x2, mod, g1, w_main, w_g, b_g, seq):
    t = x2.shape[0]
    n = w_main.shape[1]
    tiles_per_seq = seq // TM_IN
    return pl.pallas_call(
        _in_kernel,
        grid=(t // TM_IN, n // TN_IN),
        in_specs=[pl.BlockSpec((TM_IN, D_MODEL), lambda i, j: (i, 0)),
                  pl.BlockSpec((1, N_MOD, D_MODEL), lambda i, j: (i // tiles_per_seq, 0, 0)),
                  pl.BlockSpec((1, D_MODEL), lambda i, j: (0, 0)),
                  pl.BlockSpec((D_MODEL, TN_IN), lambda i, j: (0, j)),
                  pl.BlockSpec((D_MODEL, 256), lambda i, j: (0, 0)),
                  pl.BlockSpec((1, 256), lambda i, j: (0, 0))],
        out_specs=[pl.BlockSpec((None, TM_IN, TN_IN), lambda i, j: (j, i, 0)),
                   pl.BlockSpec((TM_IN, 256), lambda i, j: (i, 0))],
        out_shape=[jax.ShapeDtypeStruct((n // TN_IN, t, TN_IN), BF16),
                   jax.ShapeDtypeStruct((t, 256), F32)],
        scratch_shapes=[pltpu.VMEM((TM_IN, D_MODEL), BF16)],
        compiler_params=_params(
            ("parallel", "arbitrary"),
            [((TM_IN, D_MODEL), F32), ((D_MODEL, TN_IN), BF16), ((D_MODEL, 256), BF16),
             ((TM_IN, TN_IN), BF16), ((TM_IN, 256), F32)],
            [((TM_IN, D_MODEL), BF16)]),
        name="in_proj",
    )(x2, mod, g1, w_main, w_g, b_g)


def _seg_scan(x, row, op, fill, reverse):
    n = x.shape[0]
    d = 1
    while d < n:
        if reverse:
            y = pltpu.roll(x, n - d, axis=0)
            x = op(x, jnp.where(row < n - d, y, fill))
        else:
            y = pltpu.roll(x, d, axis=0)
            x = op(x, jnp.where(row >= d, y, fill))
        d *= 2
    return x


def _gate_direction(g_ref, m_ref, cols_ref, rowb_ref, reverse):
    n = CHUNK_A
    gi = g_ref[:, 0:128]
    gf = g_ref[:, 128:256]
    lf = jnp.minimum(gf, 0.0) - jnp.log1p(jnp.exp(-jnp.abs(gf)))
    row = lax.broadcasted_iota(jnp.int32, (n, 128), 0)
    lane = lax.broadcasted_iota(jnp.int32, (n, 128), 1)
    bc = _seg_scan(lf, row, jnp.add, 0.0, reverse)
    rb = gi - bc
    cm = _seg_scan(rb, row, jnp.maximum, -jnp.inf, reverse)
    last = 0 if reverse else n - 1
    gsum = bc[last:last + 1, :]
    m_loc = gsum + cm[last:last + 1, :]
    m = m_ref[...]
    m_inter = bc + m
    m_t = jnp.maximum(m_inter, bc + cm)
    m_new = jnp.maximum(gsum + m, m_loc)
    m_ref[...] = m_new
    col_a = bc - m_t
    a = jnp.exp(m_inter - m_t)
    e = jnp.exp(-m_t)
    w = jnp.exp(gsum + rb - m_new)
    dec = jnp.broadcast_to(jnp.exp(gsum + m - m_new), (n, 128))
    keep = lane < 8
    packed = jnp.where(keep, col_a, 0.0)
    for val, off in ((a, LANE_A), (e, LANE_E), (w, LANE_W), (dec, LANE_DEC)):
        packed = packed + pltpu.roll(jnp.where(keep, val, 0.0), off, axis=1)
    cols_ref[...] = packed
    rowb_ref[0] = rb.T[0:8, :]


def _gates_kernel(gf_ref, gb_ref, colsf_ref, colsb_ref, rowbf_ref, rowbb_ref, mf_ref, mb_ref):
    @pl.when(pl.program_id(1) == 0)
    def _():
        mf_ref[...] = jnp.full((1, 128), NEG_BIG, F32)
        mb_ref[...] = jnp.full((1, 128), NEG_BIG, F32)

    _gate_direction(gf_ref, mf_ref, colsf_ref, rowbf_ref, reverse=False)
    _gate_direction(gb_ref, mb_ref, colsb_ref, rowbb_ref, reverse=True)


def _gate_vectors(gates, batch, seq):
    nc = seq // CHUNK_A
    t = gates.shape[0]
    fwd = lambda b, k: (b * nc + k, 0)
    bwd = lambda b, k: (b * nc + nc - 1 - k, 0)
    return pl.pallas_call(
        _gates_kernel,
        grid=(batch, nc),
        in_specs=[pl.BlockSpec((CHUNK_A, 256), fwd), pl.BlockSpec((CHUNK_A, 256), bwd)],
        out_specs=[pl.BlockSpec((CHUNK_A, 128), fwd), pl.BlockSpec((CHUNK_A, 128), bwd),
                   pl.BlockSpec((1, 8, CHUNK_A), lambda b, k: (b, 0, k)),
                   pl.BlockSpec((1, 8, CHUNK_A), lambda b, k: (b, 0, nc - 1 - k))],
        out_shape=[jax.ShapeDtypeStruct((t, 128), F32), jax.ShapeDtypeStruct((t, 128), F32),
                   jax.ShapeDtypeStruct((batch, 8, seq), F32),
                   jax.ShapeDtypeStruct((batch, 8, seq), F32)],
        scratch_shapes=[pltpu.VMEM((1, 128), F32), pltpu.VMEM((1, 128), F32)],
        compiler_params=_params(("parallel", "arbitrary"),
                                [((CHUNK_A, 256), F32)] * 2 + [((CHUNK_A, 128), F32)] * 2),
        name="gate_vectors",
    )(gates, gates)


def _cast_slab_specs(w, steps, imap):
    rows, cols = w.shape
    slab = rows // steps
    assert slab * steps == rows and slab % 16 == 0
    spec = pl.BlockSpec((slab, cols), imap)
    return spec, spec, jax.ShapeDtypeStruct((rows, cols), BF16), ((slab, cols), F32)


def _mlstm_kernel(qf, kf, vf, qb, kb, vb, colsf, colsb, rowbf, rowbb, wsrc, hf, hb, wdst, ct_ref):
    @pl.when(pl.program_id(1) == 0)
    def _():
        ct_ref[...] = jnp.zeros(ct_ref.shape, F32)

    wdst[...] = wsrc[...].astype(BF16)

    n = CHUNK_A
    r = lax.broadcasted_iota(jnp.int32, (n, n), 0)
    c = lax.broadcasted_iota(jnp.int32, (n, n), 1)
    ones_cols = jnp.ones((n, 128), BF16)
    ones_rows = jnp.ones((128, n), BF16)
    seqs = []
    for head in range(NH_A):
        seqs.append((qf, kf, vf, colsf, rowbf, hf, head, head, r >= c))
        seqs.append((qb, kb, vb, colsb, rowbb, hb, head, NH_A + head, r <= c))

    loaded = []
    for q_ref, k_ref, v_ref, _, _, _, head, sid, _ in seqs:
        sl = slice(head * DH_A, (head + 1) * DH_A)
        q = q_ref[:, sl]
        k = k_ref[:, sl] * (DH_A ** -0.5)
        v = v_ref[:, sl]
        ct_old = ct_ref[sid]
        qk = lax.dot_general(q, k, (((1,), (1,)), ((), ())), preferred_element_type=F32)
        qc = lax.dot_general(q, ct_old.astype(BF16), (((1,), (1,)), ((), ())),
                             preferred_element_type=F32)
        loaded.append((k, v, ct_old, qk, qc))

    for (_, _, _, cols_ref, rowb_ref, h_ref, head, sid, mask), (k, v, ct_old, qk, qc) in zip(seqs, loaded):
        sl = slice(head * DH_A, (head + 1) * DH_A)
        col_a = cols_ref[:, LANE_COLA + sid:LANE_COLA + sid + 1]
        a = cols_ref[:, LANE_A + sid:LANE_A + sid + 1]
        e = cols_ref[:, LANE_E + sid:LANE_E + sid + 1]
        row_b = rowb_ref[0, sid:sid + 1, :]
        s = qk * jnp.exp(jnp.where(mask, col_a + row_b, -jnp.inf))
        v_ext = jnp.concatenate([v, ones_cols], axis=1)
        sv = jnp.dot(s.astype(BF16), v_ext, preferred_element_type=F32) + a * qc
        inv = 1.0 / jnp.maximum(jnp.abs(sv[:, DH_A:]), e)
        h_ref[:, sl] = (sv[:, :DH_A] * jnp.concatenate([inv, inv], axis=1)).astype(BF16)

    for (_, _, _, cols_ref, _, _, _, sid, _), (k, v, ct_old, _, _) in zip(seqs, loaded):
        w = cols_ref[:, LANE_W + sid:LANE_W + sid + 1]
        dec = cols_ref[0:1, LANE_DEC + sid:LANE_DEC + sid + 1]
        kw = (w * k.astype(F32)).astype(BF16)
        vt_ext = jnp.concatenate([v.T, ones_rows], axis=0)
        ct_ref[sid] = dec * ct_old + jnp.dot(vt_ext, kw, preferred_element_type=F32)


def _mlstm(p, colsf, colsb, rowbf, rowbb, w_cast, batch, seq):
    nc = seq // CHUNK_A
    t = p.shape[1]
    cast_in, cast_out, cast_shape, cast_blk = _cast_slab_specs(w_cast, batch * nc,
                                                               lambda b, k: (b * nc + k, 0))

    def blk(col, rev):
        if rev:
            return pl.BlockSpec((None, CHUNK_A, W_A), lambda b, k: (col, b * nc + nc - 1 - k, 0))
        return pl.BlockSpec((None, CHUNK_A, W_A), lambda b, k: (col, b * nc + k, 0))

    def outblk(rev):
        if rev:
            return pl.BlockSpec((CHUNK_A, W_A), lambda b, k: (b * nc + nc - 1 - k, 0))
        return pl.BlockSpec((CHUNK_A, W_A), lambda b, k: (b * nc + k, 0))

    colspec = lambda rev: pl.BlockSpec(
        (CHUNK_A, 128), (lambda b, k: (b * nc + nc - 1 - k, 0)) if rev else (lambda b, k: (b * nc + k, 0)))
    rowspec = lambda rev: pl.BlockSpec(
        (1, 8, CHUNK_A), (lambda b, k: (b, 0, nc - 1 - k)) if rev else (lambda b, k: (b, 0, k)))
    return pl.pallas_call(
        _mlstm_kernel,
        grid=(batch, nc),
        in_specs=[blk(0, False), blk(1, False), blk(2, False),
                  blk(0, True), blk(1, True), blk(2, True),
                  colspec(False), colspec(True), rowspec(False), rowspec(True), cast_in],
        out_specs=[outblk(False), outblk(True), cast_out],
        out_shape=[jax.ShapeDtypeStruct((t, W_A), BF16), jax.ShapeDtypeStruct((t, W_A), BF16),
                   cast_shape],
        scratch_shapes=[pltpu.VMEM((2 * NH_A, DH_A + 128, DH_A), F32)],
        compiler_params=_params(
            ("parallel", "arbitrary"),
            [((CHUNK_A, W_A), BF16)] * 8 + [((CHUNK_A, 128), F32)] * 2 + [cast_blk, cast_blk],
            [((2 * NH_A, DH_A + 128, DH_A), F32)]),
        name="mlstm",
    )(p, p, p, p, p, p, colsf, colsb, rowbf, rowbb, w_cast)


def _na_window_start(g, rows):
    return jnp.clip(g * NA_GROUP - WIN_ROWS // 2, 0, rows - NA_WINDOW)


def _natten_kernel(q_ref, kbuf, vbuf, bias_ref, wsrc1, wsrc2, o_ref, wdst1, wdst2, *, rows):
    wdst1[...] = wsrc1[...].astype(BF16)
    wdst2[...] = wsrc2[...].astype(BF16)
    g = pl.program_id(1)
    wstart = _na_window_start(g, rows)
    scale = DH_B ** -0.5
    starts, variants = [], []
    for i in range(NA_GROUP):
        r = g * NA_GROUP + i
        rs = jnp.clip(r - WIN_ROWS // 2, 0, rows - WIN_ROWS)
        starts.append(pl.multiple_of((rs - wstart) * GRID_W, GRID_W))
        variants.append(rs - r + (WIN_ROWS - 1))

    def scores(i, h):
        hs = slice(h * DH_B, (h + 1) * DH_B)
        q = q_ref[i * GRID_W:(i + 1) * GRID_W, hs]
        kw = kbuf[pl.ds(starts[i], WIN_ROWS * GRID_W), hs]
        s = lax.dot_general(q, kw, (((1,), (1,)), ((), ())), preferred_element_type=F32)
        s = s * scale + bias_ref[variants[i], h]
        p = jnp.exp(s - jnp.max(s, axis=-1, keepdims=True))
        return p.astype(BF16), 1.0 / jnp.sum(p, axis=-1, keepdims=True)

    def weighted_sum(i, h, p, inv_l):
        hs = slice(h * DH_B, (h + 1) * DH_B)
        vw = vbuf[pl.ds(starts[i], WIN_ROWS * GRID_W), hs]
        o = jnp.dot(p, vw, preferred_element_type=F32)
        o_ref[i * GRID_W:(i + 1) * GRID_W, hs] = (o * inv_l).astype(BF16)

    tiles = [(i, h) for i in range(NA_GROUP) for h in range(NH_B)]
    pending = []
    for idx in range(len(tiles) + NA_LOOKAHEAD):
        if idx < len(tiles):
            pending.append(scores(*tiles[idx]))
        if idx >= NA_LOOKAHEAD:
            weighted_sum(*tiles[idx - NA_LOOKAHEAD], *pending[idx - NA_LOOKAHEAD])
            pending[idx - NA_LOOKAHEAD] = None


def _natten(p, bias, w_cast1, w_cast2, batch, seq):
    rows = seq // GRID_W
    ng = rows // NA_GROUP
    blk = NA_GROUP * GRID_W
    win = NA_WINDOW * GRID_W
    t = p.shape[1]
    qcol, kcol, vcol = 4, 5, 6

    def window(col):
        def imap(b, g):
            return (col, (b * rows + _na_window_start(g, rows)) * GRID_W, 0)
        return pl.BlockSpec((None, pl.Element(win), pl.Element(W_B)), imap)

    step = lambda b, g: (b * ng + g, 0)
    c1_in, c1_out, c1_shape, c1_blk = _cast_slab_specs(w_cast1, batch * ng, step)
    c2_in, c2_out, c2_shape, c2_blk = _cast_slab_specs(w_cast2, batch * ng, step)
    return pl.pallas_call(
        functools.partial(_natten_kernel, rows=rows),
        grid=(batch, ng),
        in_specs=[pl.BlockSpec((None, blk, W_B), lambda b, g: (qcol, b * ng + g, 0)),
                  window(kcol), window(vcol),
                  pl.BlockSpec(bias.shape, lambda b, g: (0, 0, 0, 0)), c1_in, c2_in],
        out_specs=[pl.BlockSpec((blk, W_B), step), c1_out, c2_out],
        out_shape=[jax.ShapeDtypeStruct((t, W_B), BF16), c1_shape, c2_shape],
        compiler_params=_params(
            ("parallel", "arbitrary"),
            [((blk, W_B), BF16)] * 2 + [((win, W_B), BF16)] * 2 + [(bias.shape, F32)]
            + [c1_blk, c1_blk, c2_blk, c2_blk]),
        name="natten",
    )(p, p, p, bias, w_cast1, w_cast2)


def _natten_bias(rpb):
    cols = np.arange(GRID_W)
    cstart = np.clip(cols - WIN_COLS // 2, 0, GRID_W - WIN_COLS)
    inside = (cols[None, :] >= cstart[:, None]) & (cols[None, :] < cstart[:, None] + WIN_COLS)
    dc = cols[None, :] - cols[:, None] + (WIN_COLS - 1)
    onehot = (np.arange(2 * WIN_COLS - 1)[:, None, None] == dc[None]) & inside[None]
    toep = jnp.einsum('hdj,jck->hdck', rpb.astype(F32), onehot.astype(np.float32),
                      precision=lax.Precision.HIGHEST)
    toep = toep + np.where(inside, 0.0, NEG_BIG).astype(np.float32)
    tiles = []
    for variant in range(WIN_ROWS):
        t = toep[:, variant:variant + WIN_ROWS].transpose(0, 2, 1, 3)
        tiles.append(t.reshape(NH_B, GRID_W, WIN_ROWS * GRID_W))
    return jnp.stack(tiles, axis=0)


def _out_kernel(hf_ref, hb_ref, oa_ref, hbt_ref, x_ref, mod_ref, gh_ref, gn_ref, wa_ref, wb_ref,
                x1_ref, h2_ref):
    for c in range(TM_OUT // ROW_CHUNK):
        rows = slice(c * ROW_CHUNK, (c + 1) * ROW_CHUNK)
        mixed = jnp.dot(hbt_ref[rows, :], wb_ref[...], preferred_element_type=F32)
        hs = hf_ref[rows, :].astype(F32) + hb_ref[rows, :].astype(F32)
        parts = []
        for head in range(NH_A):
            hh = hs[:, head * DH_A:(head + 1) * DH_A]
            parts.append(hh * lax.rsqrt(jnp.mean(hh * hh, axis=-1, keepdims=True) + EPS))
        hn = jnp.concatenate(parts, axis=-1) * gh_ref[...]
        oa = oa_ref[rows, :].astype(F32)
        ha = (hn * (1.0 / (1.0 + jnp.exp(-oa)))).astype(BF16)
        mixed = mixed + jnp.dot(ha, wa_ref[...], preferred_element_type=F32)
        x1 = x_ref[rows, :] + mod_ref[0, 2:3, :] * mixed
        x1_ref[rows, :] = x1
        h2_ref[rows, :] = _rms_mod(x1, gn_ref[...], mod_ref[0, 3:4, :], mod_ref[0, 4:5, :]).astype(BF16)


def _out_proj(hf, hb, p, hbt, x2, mod, gh, g2, w_o, seq):
    t = x2.shape[0]
    tiles_per_seq = seq // TM_OUT
    row = lambda i: (i, 0)
    resident = pl.Buffered(1)
    return pl.pallas_call(
        _out_kernel,
        grid=(t // TM_OUT,),
        in_specs=[pl.BlockSpec((TM_OUT, W_A), row), pl.BlockSpec((TM_OUT, W_A), row),
                  pl.BlockSpec((None, TM_OUT, W_A), lambda i: (3, i, 0)),
                  pl.BlockSpec((TM_OUT, W_B), row),
                  pl.BlockSpec((TM_OUT, D_MODEL), row),
                  pl.BlockSpec((1, N_MOD, D_MODEL), lambda i: (i // tiles_per_seq, 0, 0)),
                  pl.BlockSpec((1, W_A), lambda i: (0, 0)),
                  pl.BlockSpec((1, D_MODEL), lambda i: (0, 0)),
                  pl.BlockSpec((W_A, D_MODEL), lambda i: (0, 0), pipeline_mode=resident),
                  pl.BlockSpec((W_B, D_MODEL), lambda i: (1, 0), pipeline_mode=resident)],
        out_specs=[pl.BlockSpec((TM_OUT, D_MODEL), row), pl.BlockSpec((TM_OUT, D_MODEL), row)],
        out_shape=[jax.ShapeDtypeStruct((t, D_MODEL), F32), jax.ShapeDtypeStruct((t, D_MODEL), BF16)],
        compiler_params=_params(
            ("parallel",),
            [((TM_OUT, W_A), BF16)] * 4 + [((TM_OUT, D_MODEL), F32)] * 2
            + [((TM_OUT, D_MODEL), BF16)] + [((W_A, D_MODEL), BF16)]),
        name="out_proj",
    )(hf, hb, p, hbt, x2, mod, gh, g2, w_o, w_o)


def _up_kernel(hp_ref, h_ref, hn_ref, wu_ref, wg_ref, cw_ref, cb_ref, act_ref, hext_ref, *,
               tiles_per_seq):
    pos = pl.program_id(1) % tiles_per_seq
    n = TM_UP + 2 * HALO
    hext_ref[0:HALO, :] = jnp.where(pos > 0, hp_ref[...], jnp.zeros_like(hp_ref))
    hext_ref[HALO:HALO + TM_UP, :] = h_ref[...]
    hext_ref[HALO + TM_UP:, :] = jnp.where(pos < tiles_per_seq - 1, hn_ref[...], jnp.zeros_like(hn_ref))
    for c in range(TN_UP // COL_CHUNK):
        cols = slice(c * COL_CHUNK, (c + 1) * COL_CHUNK)
        u = jnp.dot(h_ref[...], wu_ref[:, cols], preferred_element_type=F32)
        g = jnp.dot(hext_ref[...], wg_ref[:, cols], preferred_element_type=F32)
        g_prev = pltpu.roll(g, 1, axis=0)[HALO:HALO + TM_UP, :]
        g_next = pltpu.roll(g, n - 1, axis=0)[HALO:HALO + TM_UP, :]
        gc = (g_prev * cw_ref[0:1, cols] + g[HALO:HALO + TM_UP, :] * cw_ref[1:2, cols]
              + g_next * cw_ref[2:3, cols] + cb_ref[:, cols])
        gelu = 0.5 * gc * (1.0 + lax.erf(gc * (2.0 ** -0.5)))
        act_ref[:, cols] = (gelu * u).astype(BF16)


def _up_proj(h2, w_up, conv_w, conv_b, seq):
    t = h2.shape[0]
    tiles_per_seq = seq // TM_UP
    hb = TM_UP // HALO
    nh = t // HALO
    ngroups = D_FF // TN_UP
    resident = pl.Buffered(1)
    return pl.pallas_call(
        functools.partial(_up_kernel, tiles_per_seq=tiles_per_seq),
        grid=(ngroups, t // TM_UP),
        in_specs=[pl.BlockSpec((HALO, D_MODEL), lambda j, i: (jnp.maximum(i * hb - 1, 0), 0)),
                  pl.BlockSpec((TM_UP, D_MODEL), lambda j, i: (i, 0)),
                  pl.BlockSpec((HALO, D_MODEL), lambda j, i: (jnp.minimum((i + 1) * hb, nh - 1), 0)),
                  pl.BlockSpec((D_MODEL, TN_UP), lambda j, i: (0, j), pipeline_mode=resident),
                  pl.BlockSpec((D_MODEL, TN_UP), lambda j, i: (0, ngroups + j), pipeline_mode=resident),
                  pl.BlockSpec((8, TN_UP), lambda j, i: (0, j)),
                  pl.BlockSpec((1, TN_UP), lambda j, i: (0, j))],
        out_specs=pl.BlockSpec((TM_UP, TN_UP), lambda j, i: (i, j)),
        out_shape=jax.ShapeDtypeStruct((t, D_FF), BF16),
        scratch_shapes=[pltpu.VMEM((TM_UP + 2 * HALO, D_MODEL), BF16)],
        compiler_params=_params(
            ("arbitrary", "arbitrary"),
            [((TM_UP + 2 * HALO, D_MODEL), BF16), ((D_MODEL, TN_UP), BF16), ((TM_UP, TN_UP), BF16)],
            [((TM_UP + 2 * HALO, D_MODEL), BF16)]),
        name="up_proj",
    )(h2, h2, h2, w_up, w_up, conv_w, conv_b)


def _down_kernel(a_ref, w_ref, x_ref, mod_ref, g_ref, o_ref):
    for c in range(TM_DOWN // ROW_CHUNK):
        rows = slice(c * ROW_CHUNK, (c + 1) * ROW_CHUNK)
        y = x_ref[rows, :] + mod_ref[0, 5:6, :] * jnp.dot(a_ref[rows, :], w_ref[...],
                                                           preferred_element_type=F32)
        o_ref[rows, :] = y * lax.rsqrt(jnp.mean(y * y, axis=-1, keepdims=True) + EPS) * g_ref[...]


def _down_proj(act, w_d, x1, mod, gfin, seq):
    t = x1.shape[0]
    tiles_per_seq = seq // TM_DOWN
    return pl.pallas_call(
        _down_kernel,
        grid=(t // TM_DOWN,),
        in_specs=[pl.BlockSpec((TM_DOWN, D_FF), lambda i: (i, 0)),
                  pl.BlockSpec((D_FF, D_MODEL), lambda i: (0, 0), pipeline_mode=pl.Buffered(1)),
                  pl.BlockSpec((TM_DOWN, D_MODEL), lambda i: (i, 0)),
                  pl.BlockSpec((1, N_MOD, D_MODEL), lambda i: (i // tiles_per_seq, 0, 0)),
                  pl.BlockSpec((1, D_MODEL), lambda i: (0, 0))],
        out_specs=pl.BlockSpec((TM_DOWN, D_MODEL), lambda i: (i, 0)),
        out_shape=jax.ShapeDtypeStruct((t, D_MODEL), F32),
        compiler_params=pltpu.CompilerParams(
            dimension_semantics=("parallel",),
            vmem_limit_bytes=_nbytes((D_FF, D_MODEL), BF16) + 2 * (
                _nbytes((TM_DOWN, D_FF), BF16) + 2 * _nbytes((TM_DOWN, D_MODEL), F32))
            + DOWN_TEMP_ALLOWANCE),
        name="down_proj",
    )(act, w_d, x1, mod, gfin)


def kernel(x, c, w_ada, b_ada, g_norm1, w_in, b_gates, g_head_a, rpb, w_out, g_norm2, w_up, conv_w,
           conv_b, w_down, g_final):
    batch, seq, d = x.shape
    assert w_ada.shape[0] == 1
    assert d == D_MODEL and seq % CHUNK_A == 0 and seq % (NA_GROUP * GRID_W) == 0
    x2 = x.reshape(batch * seq, d)
    c8 = jnp.zeros((8, d), F32).at[:batch].set(c)
    mod = _ada(c8, w_ada[0], b_ada)[:batch].reshape(batch, N_MOD, d)

    w_main = _win_prep(w_in[0])
    lane_of = np.concatenate([np.arange(0, 4), np.arange(128, 132), np.arange(4, 8), np.arange(132, 136)])
    sel = np.zeros((4 * NH_A, 256), np.float32)
    sel[np.arange(4 * NH_A), lane_of] = 1.0
    w_g = jnp.dot(w_in[0][:, GATE_LO:GATE_HI], sel, precision=lax.Precision.HIGHEST).astype(BF16)
    b_g = jnp.dot(b_gates, sel, precision=lax.Precision.HIGHEST)

    p, gates = _in_proj(x2, mod, g_norm1, w_main, w_g, b_g, seq)
    colsf, colsb, rowbf, rowbb = _gate_vectors(gates, batch, seq)
    hf, hb, w_up_b = _mlstm(p, colsf, colsb, rowbf, rowbb, w_up[0], batch, seq)
    hbt, w_down_b, w_out_b = _natten(p, _natten_bias(rpb[0]), w_down[0], w_out[0], batch, seq)
    x1, h2 = _out_proj(hf, hb, p, hbt, x2, mod, g_head_a, g_norm2, w_out_b, seq)

    cw8 = jnp.zeros((8, D_FF), F32).at[:3].set(conv_w[0])
    act = _up_proj(h2, w_up_b, cw8, conv_b, seq)
    out = _down_proj(act, w_down_b, x1, mod, g_final[None, :], seq)
    return out.reshape(batch, seq, d)
```

```python
import functools

import jax
import jax.numpy as jnp
import numpy as np
from jax import lax
from jax.experimental import pallas as pl
from jax.experimental.pallas import tpu as pltpu

F32 = jnp.float32
BF16 = jnp.bfloat16

D_MODEL = 2048
GRID_W = 64
NH_A = 4
DH_A = 256
W_A = NH_A * DH_A
NH_B = 8
DH_B = 128
W_B = NH_B * DH_B
WIN_ROWS = 8
WIN_COLS = 16
D_FF = 5632
N_MOD = 6
EPS = 1e-6
NEG_BIG = -1e30

CHUNK_A = 256

TM_IN = 1024
TN_IN = 1024
TM_OUT = 512
ROW_CHUNK = 256
TM_UP = 1024
TN_UP = D_FF // 2
COL_CHUNK = 256
HALO = 16
TM_DOWN = 512
NA_GROUP = 8
NA_WINDOW = NA_GROUP + WIN_ROWS
NA_LOOKAHEAD = 4

V7X_VMEM_BYTES = 64 * 1024 * 1024
VMEM_TEMP_ALLOWANCE = 12 * 1024 * 1024
DOWN_TEMP_ALLOWANCE = 6 * 1024 * 1024


def _vmem_limit(block_bytes, scratch_bytes=0):
    est = 2 * block_bytes + scratch_bytes + VMEM_TEMP_ALLOWANCE
    return int(min(est, V7X_VMEM_BYTES - 4 * 1024 * 1024))


def _nbytes(shape, dtype):
    n = 1
    for s in shape:
        n *= s
    return n * jnp.dtype(dtype).itemsize


def _params(sem, blocks, scratch=()):
    bb = sum(_nbytes(s, d) for s, d in blocks)
    sb = sum(_nbytes(s, d) for s, d in scratch)
    return pltpu.CompilerParams(dimension_semantics=sem, vmem_limit_bytes=_vmem_limit(bb, sb))


def _rms_mod(x, g, shift, scale):
    y = x * lax.rsqrt(jnp.mean(x * x, axis=-1, keepdims=True) + EPS) * g
    return y * (1.0 + scale) + shift


def _ada_kernel(c_ref, w_ref, b_ref, o_ref):
    c = c_ref[...]
    s = c * (1.0 / (1.0 + jnp.exp(-c)))
    o_ref[...] = jnp.dot(s, w_ref[...], preferred_element_type=F32,
                         precision=lax.Precision.HIGHEST) + b_ref[...]


def _ada(c8, w_ada, b_ada):
    n = w_ada.shape[2]
    tn = 1024
    return pl.pallas_call(
        _ada_kernel,
        grid=(n // tn,),
        in_specs=[pl.BlockSpec((8, D_MODEL), lambda j: (0, 0)),
                  pl.BlockSpec((None, D_MODEL, tn), lambda j: (0, 0, j)),
                  pl.BlockSpec((1, tn), lambda j: (0, j))],
        out_specs=pl.BlockSpec((8, tn), lambda j: (0, j)),
        out_shape=jax.ShapeDtypeStruct((8, n), F32),
        compiler_params=_params(("arbitrary",), [((D_MODEL, tn), F32), ((8, D_MODEL), F32)]),
        name="ada",
    )(c8, w_ada, b_ada)


GATE_LO = 4 * W_A
GATE_HI = GATE_LO + 4 * NH_A
TM_PREP = 256


def _win_prep_kernel(w_ref, o_ref, g_ref):
    o_ref[:, :GATE_LO] = w_ref[:, :GATE_LO].astype(BF16)
    o_ref[:, GATE_LO:] = w_ref[:, GATE_HI:].astype(BF16)
    g_ref[...] = w_ref[:, GATE_LO:GATE_HI]


def _win_prep(w):
    _, rows, cols = w.shape
    ng = GATE_HI - GATE_LO
    n = cols - ng
    return pl.pallas_call(
        _win_prep_kernel,
        grid=(rows // TM_PREP,),
        in_specs=[pl.BlockSpec((None, TM_PREP, cols), lambda i: (0, i, 0))],
        out_specs=[pl.BlockSpec((TM_PREP, n), lambda i: (i, 0)),
                   pl.BlockSpec((TM_PREP, ng), lambda i: (i, 0))],
        out_shape=[jax.ShapeDtypeStruct((rows, n), BF16), jax.ShapeDtypeStruct((rows, ng), F32)],
        compiler_params=_params(("parallel",), [((TM_PREP, cols), F32), ((TM_PREP, n), BF16)]),
        name="w_in_prep",
    )(w)


def _in_kernel(x_ref, mod_ref, g_ref, w_ref, wg_ref, bg_ref, p_ref, gates_ref, h_ref):
    @pl.when(pl.program_id(1) == 0)
    def _():
        h = _rms_mod(x_ref[...], g_ref[...], mod_ref[0, 0:1, :], mod_ref[0, 1:2, :])
        hb = h.astype(BF16)
        h_ref[...] = hb
        gates_ref[...] = jnp.dot(hb, wg_ref[...], preferred_element_type=F32) + bg_ref[...]

    p_ref[...] = jnp.dot(h_ref[...], w_ref[...], preferred_element_type=F32).astype(BF16)


def _in_proj(x2, mod, g1, w_main, w_g, b_g, seq):
    t = x2.shape[0]
    n = w_main.shape[1]
    tiles_per_seq = seq // TM_IN
    return pl.pallas_call(
        _in_kernel,
        grid=(t // TM_IN, n // TN_IN),
        in_specs=[pl.BlockSpec((TM_IN, D_MODEL), lambda i, j: (i, 0)),
                  pl.BlockSpec((1, N_MOD, D_MODEL), lambda i, j: (i // tiles_per_seq, 0, 0)),
                  pl.BlockSpec((1, D_MODEL), lambda i, j: (0, 0)),
                  pl.BlockSpec((D_MODEL, TN_IN), lambda i, j: (0, j)),
                  pl.BlockSpec((D_MODEL, 256), lambda i, j: (0, 0)),
                  pl.BlockSpec((1, 256), lambda i, j: (0, 0))],
        out_specs=[pl.BlockSpec((None, TM_IN, TN_IN), lambda i, j: (j, i, 0)),
                   pl.BlockSpec((TM_IN, 256), lambda i, j: (i, 0))],
        out_shape=[jax.ShapeDtypeStruct((n // TN_IN, t, TN_IN), BF16),
                   jax.ShapeDtypeStruct((t, 256), F32)],
        scratch_shapes=[pltpu.VMEM((TM_IN, D_MODEL), BF16)],
        compiler_params=_params(
            ("parallel", "arbitrary"),
            [((TM_IN, D_MODEL), F32), ((D_MODEL, TN_IN), BF16), ((D_MODEL, 256), BF16),
             ((TM_IN, TN_IN), BF16), ((TM_IN, 256), F32)],
            [((TM_IN, D_MODEL), BF16)]),
        name="in_proj",
    )(x2, mod, g1, w_main, w_g, b_g)


def _seg_scan(x, row, op, fill, reverse):
    n = x.shape[0]
    d = 1
    while d < n:
        if reverse:
            y = pltpu.roll(x, n - d, axis=0)
            x = op(x, jnp.where(row < n - d, y, fill))
        else:
            y = pltpu.roll(x, d, axis=0)
            x = op(x, jnp.where(row >= d, y, fill))
        d *= 2
    return x


def _gate_vectors(g_ref, m_ref, reverse):
    n = CHUNK_A
    gi = g_ref[:, 0:128]
    gf = g_ref[:, 128:256]
    lf = jnp.minimum(gf, 0.0) - jnp.log1p(jnp.exp(-jnp.abs(gf)))
    row = lax.broadcasted_iota(jnp.int32, (n, 128), 0)
    bc = _seg_scan(lf, row, jnp.add, 0.0, reverse)
    rb = gi - bc
    cm = _seg_scan(rb, row, jnp.maximum, -jnp.inf, reverse)
    last = 0 if reverse else n - 1
    gsum = bc[last:last + 1, :]
    m_loc = gsum + cm[last:last + 1, :]
    m = m_ref[...]
    m_inter = bc + m
    m_t = jnp.maximum(m_inter, bc + cm)
    m_new = jnp.maximum(gsum + m, m_loc)
    m_ref[...] = m_new
    col_a = bc - m_t
    a = jnp.exp(m_inter - m_t)
    e = jnp.exp(-m_t)
    w = jnp.exp(gsum + rb - m_new)
    dec = jnp.exp(gsum + m - m_new)
    return col_a, a, e, w, dec, rb.T


def _cast_slab_specs(w, steps, imap):
    _, rows, cols = w.shape
    slab = rows // steps
    assert slab * steps == rows and slab % 16 == 0
    in_spec = pl.BlockSpec((None, slab, cols), lambda *ids: (0,) + tuple(imap(*ids)))
    out_spec = pl.BlockSpec((slab, cols), imap)
    return in_spec, out_spec, jax.ShapeDtypeStruct((rows, cols), BF16), ((slab, cols), F32)


def _mlstm_kernel(qf, kf, vf, qb, kb, vb, gf, gb, wsrc, hf, hb, wdst, ct_ref, mf_ref, mb_ref):
    @pl.when(pl.program_id(1) == 0)
    def _():
        ct_ref[...] = jnp.zeros(ct_ref.shape, F32)
        mf_ref[...] = jnp.full((1, 128), NEG_BIG, F32)
        mb_ref[...] = jnp.full((1, 128), NEG_BIG, F32)

    wdst[...] = wsrc[...].astype(BF16)

    n = CHUNK_A
    r = lax.broadcasted_iota(jnp.int32, (n, n), 0)
    c = lax.broadcasted_iota(jnp.int32, (n, n), 1)
    ones_cols = jnp.ones((n, 128), BF16)
    ones_rows = jnp.ones((128, n), BF16)
    seqs = []
    for head in range(NH_A):
        seqs.append((qf, kf, vf, False, hf, head, head, r >= c))
        seqs.append((qb, kb, vb, True, hb, head, NH_A + head, r <= c))

    loaded = []
    for q_ref, k_ref, v_ref, _, _, head, sid, _ in seqs:
        sl = slice(head * DH_A, (head + 1) * DH_A)
        q = q_ref[:, sl]
        k = k_ref[:, sl] * (DH_A ** -0.5)
        v = v_ref[:, sl]
        ct_old = ct_ref[sid]
        qk = lax.dot_general(q, k, (((1,), (1,)), ((), ())), preferred_element_type=F32)
        qc = lax.dot_general(q, ct_old.astype(BF16), (((1,), (1,)), ((), ())),
                             preferred_element_type=F32)
        loaded.append((k, v, ct_old, qk, qc))

    gate_vecs = {False: _gate_vectors(gf, mf_ref, reverse=False),
                 True: _gate_vectors(gb, mb_ref, reverse=True)}
    for (_, _, _, rev, h_ref, head, sid, mask), (k, v, ct_old, qk, qc) in zip(seqs, loaded):
        gates = gate_vecs[rev]
        sl = slice(head * DH_A, (head + 1) * DH_A)
        col_a, a, e = (g[:, sid:sid + 1] for g in gates[:3])
        row_b = gates[5][sid:sid + 1, :]
        s = qk * jnp.exp(jnp.where(mask, col_a + row_b, -jnp.inf))
        v_ext = jnp.concatenate([v, ones_cols], axis=1)
        sv = jnp.dot(s.astype(BF16), v_ext, preferred_element_type=F32) + a * qc
        inv = 1.0 / jnp.maximum(jnp.abs(sv[:, DH_A:]), e)
        h_ref[:, sl] = (sv[:, :DH_A] * jnp.concatenate([inv, inv], axis=1)).astype(BF16)

    for (_, _, _, rev, _, _, sid, _), (k, v, ct_old, _, _) in zip(seqs, loaded):
        gates = gate_vecs[rev]
        w = gates[3][:, sid:sid + 1]
        dec = gates[4][:, sid:sid + 1]
        kw = (w * k.astype(F32)).astype(BF16)
        vt_ext = jnp.concatenate([v.T, ones_rows], axis=0)
        ct_ref[sid] = dec * ct_old + jnp.dot(vt_ext, kw, preferred_element_type=F32)


def _mlstm(p, gates, w_cast, batch, seq):
    nc = seq // CHUNK_A
    t = p.shape[1]
    cast_in, cast_out, cast_shape, cast_blk = _cast_slab_specs(w_cast, batch * nc,
                                                               lambda b, k: (b * nc + k, 0))

    def blk(col, rev):
        if rev:
            return pl.BlockSpec((None, CHUNK_A, W_A), lambda b, k: (col, b * nc + nc - 1 - k, 0))
        return pl.BlockSpec((None, CHUNK_A, W_A), lambda b, k: (col, b * nc + k, 0))

    def outblk(rev):
        if rev:
            return pl.BlockSpec((CHUNK_A, W_A), lambda b, k: (b * nc + nc - 1 - k, 0))
        return pl.BlockSpec((CHUNK_A, W_A), lambda b, k: (b * nc + k, 0))

    gatespec = lambda rev: pl.BlockSpec(
        (CHUNK_A, 256), (lambda b, k: (b * nc + nc - 1 - k, 0)) if rev else (lambda b, k: (b * nc + k, 0)))
    return pl.pallas_call(
        _mlstm_kernel,
        grid=(batch, nc),
        in_specs=[blk(0, False), blk(1, False), blk(2, False),
                  blk(0, True), blk(1, True), blk(2, True),
                  gatespec(False), gatespec(True), cast_in],
        out_specs=[outblk(False), outblk(True), cast_out],
        out_shape=[jax.ShapeDtypeStruct((t, W_A), BF16), jax.ShapeDtypeStruct((t, W_A), BF16),
                   cast_shape],
        scratch_shapes=[pltpu.VMEM((2 * NH_A, DH_A + 128, DH_A), F32),
                        pltpu.VMEM((1, 128), F32), pltpu.VMEM((1, 128), F32)],
        compiler_params=_params(
            ("parallel", "arbitrary"),
            [((CHUNK_A, W_A), BF16)] * 8 + [((CHUNK_A, 256), F32)] * 2 + [cast_blk, cast_blk],
            [((2 * NH_A, DH_A + 128, DH_A), F32)]),
        name="mlstm",
    )(p, p, p, p, p, p, gates, gates, w_cast)


def _na_window_start(g, rows):
    return jnp.clip(g * NA_GROUP - WIN_ROWS // 2, 0, rows - NA_WINDOW)


def _natten_kernel(q_ref, kbuf, vbuf, bias_ref, wsrc1, wsrc2, o_ref, wdst1, wdst2, *, rows):
    wdst1[...] = wsrc1[...].astype(BF16)
    wdst2[...] = wsrc2[...].astype(BF16)
    g = pl.program_id(1)
    wstart = _na_window_start(g, rows)
    scale = DH_B ** -0.5
    starts, variants = [], []
    for i in range(NA_GROUP):
        r = g * NA_GROUP + i
        rs = jnp.clip(r - WIN_ROWS // 2, 0, rows - WIN_ROWS)
        starts.append(pl.multiple_of((rs - wstart) * GRID_W, GRID_W))
        variants.append(rs - r + (WIN_ROWS - 1))

    def scores(i, h):
        hs = slice(h * DH_B, (h + 1) * DH_B)
        q = q_ref[i * GRID_W:(i + 1) * GRID_W, hs]
        kw = kbuf[pl.ds(starts[i], WIN_ROWS * GRID_W), hs]
        s = lax.dot_general(q, kw, (((1,), (1,)), ((), ())), preferred_element_type=F32)
        s = s * scale + bias_ref[variants[i], h]
        p = jnp.exp(s - jnp.max(s, axis=-1, keepdims=True))
        return p.astype(BF16), 1.0 / jnp.sum(p, axis=-1, keepdims=True)

    def weighted_sum(i, h, p, inv_l):
        hs = slice(h * DH_B, (h + 1) * DH_B)
        vw = vbuf[pl.ds(starts[i], WIN_ROWS * GRID_W), hs]
        o = jnp.dot(p, vw, preferred_element_type=F32)
        o_ref[i * GRID_W:(i + 1) * GRID_W, hs] = (o * inv_l).astype(BF16)

    tiles = [(i, h) for i in range(NA_GROUP) for h in range(NH_B)]
    pending = []
    for idx in range(len(tiles) + NA_LOOKAHEAD):
        if idx < len(tiles):
            pending.append(scores(*tiles[idx]))
        if idx >= NA_LOOKAHEAD:
            weighted_sum(*tiles[idx - NA_LOOKAHEAD], *pending[idx - NA_LOOKAHEAD])
            pending[idx - NA_LOOKAHEAD] = None


def _natten(p, bias, w_cast1, w_cast2, batch, seq):
    rows = seq // GRID_W
    ng = rows // NA_GROUP
    blk = NA_GROUP * GRID_W
    win = NA_WINDOW * GRID_W
    t = p.shape[1]
    qcol, kcol, vcol = 4, 5, 6

    def window(col):
        def imap(b, g):
            return (col, (b * rows + _na_window_start(g, rows)) * GRID_W, 0)
        return pl.BlockSpec((None, pl.Element(win), pl.Element(W_B)), imap)

    step = lambda b, g: (b * ng + g, 0)
    c1_in, c1_out, c1_shape, c1_blk = _cast_slab_specs(w_cast1, batch * ng, step)
    c2_in, c2_out, c2_shape, c2_blk = _cast_slab_specs(w_cast2, batch * ng, step)
    return pl.pallas_call(
        functools.partial(_natten_kernel, rows=rows),
        grid=(batch, ng),
        in_specs=[pl.BlockSpec((None, blk, W_B), lambda b, g: (qcol, b * ng + g, 0)),
                  window(kcol), window(vcol),
                  pl.BlockSpec(bias.shape, lambda b, g: (0, 0, 0, 0)), c1_in, c2_in],
        out_specs=[pl.BlockSpec((blk, W_B), step), c1_out, c2_out],
        out_shape=[jax.ShapeDtypeStruct((t, W_B), BF16), c1_shape, c2_shape],
        compiler_params=_params(
            ("parallel", "arbitrary"),
            [((blk, W_B), BF16)] * 2 + [((win, W_B), BF16)] * 2 + [(bias.shape, F32)]
            + [c1_blk, c1_blk, c2_blk, c2_blk]),
        name="natten",
    )(p, p, p, bias, w_cast1, w_cast2)


def _natten_bias(rpb):
    cols = np.arange(GRID_W)
    cstart = np.clip(cols - WIN_COLS // 2, 0, GRID_W - WIN_COLS)
    inside = (cols[None, :] >= cstart[:, None]) & (cols[None, :] < cstart[:, None] + WIN_COLS)
    dc = cols[None, :] - cols[:, None] + (WIN_COLS - 1)
    onehot = (np.arange(2 * WIN_COLS - 1)[:, None, None] == dc[None]) & inside[None]
    toep = jnp.einsum('hdj,jck->hdck', rpb.astype(F32), onehot.astype(np.float32),
                      precision=lax.Precision.HIGHEST)
    toep = toep + np.where(inside, 0.0, NEG_BIG).astype(np.float32)
    tiles = []
    for variant in range(WIN_ROWS):
        t = toep[:, variant:variant + WIN_ROWS].transpose(0, 2, 1, 3)
        tiles.append(t.reshape(NH_B, GRID_W, WIN_ROWS * GRID_W))
    return jnp.stack(tiles, axis=0)


def _out_kernel(hf_ref, hb_ref, oa_ref, hbt_ref, x_ref, mod_ref, gh_ref, gn_ref, wa_ref, wb_ref,
                x1_ref, h2_ref):
    for c in range(TM_OUT // ROW_CHUNK):
        rows = slice(c * ROW_CHUNK, (c + 1) * ROW_CHUNK)
        mixed = jnp.dot(hbt_ref[rows, :], wb_ref[...], preferred_element_type=F32)
        hs = hf_ref[rows, :].astype(F32) + hb_ref[rows, :].astype(F32)
        parts = []
        for head in range(NH_A):
            hh = hs[:, head * DH_A:(head + 1) * DH_A]
            parts.append(hh * lax.rsqrt(jnp.mean(hh * hh, axis=-1, keepdims=True) + EPS))
        hn = jnp.concatenate(parts, axis=-1) * gh_ref[...]
        oa = oa_ref[rows, :].astype(F32)
        ha = (hn * (1.0 / (1.0 + jnp.exp(-oa)))).astype(BF16)
        mixed = mixed + jnp.dot(ha, wa_ref[...], preferred_element_type=F32)
        x1 = x_ref[rows, :] + mod_ref[0, 2:3, :] * mixed
        x1_ref[rows, :] = x1
        h2_ref[rows, :] = _rms_mod(x1, gn_ref[...], mod_ref[0, 3:4, :], mod_ref[0, 4:5, :]).astype(BF16)


def _out_proj(hf, hb, p, hbt, x2, mod, gh, g2, w_o, seq):
    t = x2.shape[0]
    tiles_per_seq = seq // TM_OUT
    row = lambda i: (i, 0)
    resident = pl.Buffered(1)
    return pl.pallas_call(
        _out_kernel,
        grid=(t // TM_OUT,),
        in_specs=[pl.BlockSpec((TM_OUT, W_A), row), pl.BlockSpec((TM_OUT, W_A), row),
                  pl.BlockSpec((None, TM_OUT, W_A), lambda i: (3, i, 0)),
                  pl.BlockSpec((TM_OUT, W_B), row),
                  pl.BlockSpec((TM_OUT, D_MODEL), row),
                  pl.BlockSpec((1, N_MOD, D_MODEL), lambda i: (i // tiles_per_seq, 0, 0)),
                  pl.BlockSpec((1, W_A), lambda i: (0, 0)),
                  pl.BlockSpec((1, D_MODEL), lambda i: (0, 0)),
                  pl.BlockSpec((W_A, D_MODEL), lambda i: (0, 0), pipeline_mode=resident),
                  pl.BlockSpec((W_B, D_MODEL), lambda i: (1, 0), pipeline_mode=resident)],
        out_specs=[pl.BlockSpec((TM_OUT, D_MODEL), row), pl.BlockSpec((TM_OUT, D_MODEL), row)],
        out_shape=[jax.ShapeDtypeStruct((t, D_MODEL), F32), jax.ShapeDtypeStruct((t, D_MODEL), BF16)],
        compiler_params=_params(
            ("parallel",),
            [((TM_OUT, W_A), BF16)] * 4 + [((TM_OUT, D_MODEL), F32)] * 2
            + [((TM_OUT, D_MODEL), BF16)] + [((W_A, D_MODEL), BF16)]),
        name="out_proj",
    )(hf, hb, p, hbt, x2, mod, gh, g2, w_o, w_o)


def _up_kernel(hp_ref, h_ref, hn_ref, wu_ref, wg_ref, cw_ref, cb_ref, act_ref, hext_ref, *,
               tiles_per_seq):
    pos = pl.program_id(1) % tiles_per_seq
    n = TM_UP + 2 * HALO
    hext_ref[0:HALO, :] = jnp.where(pos > 0, hp_ref[...], jnp.zeros_like(hp_ref))
    hext_ref[HALO:HALO + TM_UP, :] = h_ref[...]
    hext_ref[HALO + TM_UP:, :] = jnp.where(pos < tiles_per_seq - 1, hn_ref[...], jnp.zeros_like(hn_ref))
    for c in range(TN_UP // COL_CHUNK):
        cols = slice(c * COL_CHUNK, (c + 1) * COL_CHUNK)
        u = jnp.dot(h_ref[...], wu_ref[:, cols], preferred_element_type=F32)
        g = jnp.dot(hext_ref[...], wg_ref[:, cols], preferred_element_type=F32)
        g_prev = pltpu.roll(g, 1, axis=0)[HALO:HALO + TM_UP, :]
        g_next = pltpu.roll(g, n - 1, axis=0)[HALO:HALO + TM_UP, :]
        gc = (g_prev * cw_ref[0:1, cols] + g[HALO:HALO + TM_UP, :] * cw_ref[1:2, cols]
              + g_next * cw_ref[2:3, cols] + cb_ref[:, cols])
        gelu = 0.5 * gc * (1.0 + lax.erf(gc * (2.0 ** -0.5)))
        act_ref[:, cols] = (gelu * u).astype(BF16)


def _up_proj(h2, w_up, conv_w, conv_b, seq):
    t = h2.shape[0]
    tiles_per_seq = seq // TM_UP
    hb = TM_UP // HALO
    nh = t // HALO
    ngroups = D_FF // TN_UP
    resident = pl.Buffered(1)
    return pl.pallas_call(
        functools.partial(_up_kernel, tiles_per_seq=tiles_per_seq),
        grid=(ngroups, t // TM_UP),
        in_specs=[pl.BlockSpec((HALO, D_MODEL), lambda j, i: (jnp.maximum(i * hb - 1, 0), 0)),
                  pl.BlockSpec((TM_UP, D_MODEL), lambda j, i: (i, 0)),
                  pl.BlockSpec((HALO, D_MODEL), lambda j, i: (jnp.minimum((i + 1) * hb, nh - 1), 0)),
                  pl.BlockSpec((D_MODEL, TN_UP), lambda j, i: (0, j), pipeline_mode=resident),
                  pl.BlockSpec((D_MODEL, TN_UP), lambda j, i: (0, ngroups + j), pipeline_mode=resident),
                  pl.BlockSpec((8, TN_UP), lambda j, i: (0, j)),
                  pl.BlockSpec((1, TN_UP), lambda j, i: (0, j))],
        out_specs=pl.BlockSpec((TM_UP, TN_UP), lambda j, i: (i, j)),
        out_shape=jax.ShapeDtypeStruct((t, D_FF), BF16),
        scratch_shapes=[pltpu.VMEM((TM_UP + 2 * HALO, D_MODEL), BF16)],
        compiler_params=_params(
            ("arbitrary", "arbitrary"),
            [((TM_UP + 2 * HALO, D_MODEL), BF16), ((D_MODEL, TN_UP), BF16), ((TM_UP, TN_UP), BF16)],
            [((TM_UP + 2 * HALO, D_MODEL), BF16)]),
        name="up_proj",
    )(h2, h2, h2, w_up, w_up, conv_w, conv_b)


def _down_kernel(a_ref, w_ref, x_ref, mod_ref, g_ref, o_ref):
    for c in range(TM_DOWN // ROW_CHUNK):
        rows = slice(c * ROW_CHUNK, (c + 1) * ROW_CHUNK)
        y = x_ref[rows, :] + mod_ref[0, 5:6, :] * jnp.dot(a_ref[rows, :], w_ref[...],
                                                           preferred_element_type=F32)
        o_ref[rows, :] = y * lax.rsqrt(jnp.mean(y * y, axis=-1, keepdims=True) + EPS) * g_ref[...]


def _down_proj(act, w_d, x1, mod, gfin, seq):
    t = x1.shape[0]
    tiles_per_seq = seq // TM_DOWN
    return pl.pallas_call(
        _down_kernel,
        grid=(t // TM_DOWN,),
        in_specs=[pl.BlockSpec((TM_DOWN, D_FF), lambda i: (i, 0)),
                  pl.BlockSpec((D_FF, D_MODEL), lambda i: (0, 0), pipeline_mode=pl.Buffered(1)),
                  pl.BlockSpec((TM_DOWN, D_MODEL), lambda i: (i, 0)),
                  pl.BlockSpec((1, N_MOD, D_MODEL), lambda i: (i // tiles_per_seq, 0, 0)),
                  pl.BlockSpec((1, D_MODEL), lambda i: (0, 0))],
        out_specs=pl.BlockSpec((TM_DOWN, D_MODEL), lambda i: (i, 0)),
        out_shape=jax.ShapeDtypeStruct((t, D_MODEL), F32),
        compiler_params=pltpu.CompilerParams(
            dimension_semantics=("parallel",),
            vmem_limit_bytes=_nbytes((D_FF, D_MODEL), BF16) + 2 * (
                _nbytes((TM_DOWN, D_FF), BF16) + 2 * _nbytes((TM_DOWN, D_MODEL), F32))
            + DOWN_TEMP_ALLOWANCE),
        name="down_proj",
    )(act, w_d, x1, mod, gfin)


def kernel(x, c, w_ada, b_ada, g_norm1, w_in, b_gates, g_head_a, rpb, w_out, g_norm2, w_up, conv_w,
           conv_b, w_down, g_final):
    batch, seq, d = x.shape
    assert w_ada.shape[0] == 1
    assert d == D_MODEL and seq % CHUNK_A == 0 and seq % (NA_GROUP * GRID_W) == 0
    x2 = x.reshape(batch * seq, d)
    c8 = jnp.zeros((8, d), F32).at[:batch].set(c)
    mod = _ada(c8, w_ada, b_ada)[:batch].reshape(batch, N_MOD, d)

    w_main, w_gate_cols = _win_prep(w_in)
    lane_of = np.concatenate([np.arange(0, 4), np.arange(128, 132), np.arange(4, 8), np.arange(132, 136)])
    sel = np.zeros((4 * NH_A, 256), np.float32)
    sel[np.arange(4 * NH_A), lane_of] = 1.0
    w_g = jnp.dot(w_gate_cols, sel, precision=lax.Precision.HIGHEST).astype(BF16)
    b_g = jnp.dot(b_gates, sel, precision=lax.Precision.HIGHEST)

    p, gates = _in_proj(x2, mod, g_norm1, w_main, w_g, b_g, seq)
    hf, hb, w_up_b = _mlstm(p, gates, w_up, batch, seq)
    hbt, w_down_b, w_out_b = _natten(p, _natten_bias(rpb[0]), w_down, w_out, batch, seq)
    x1, h2 = _out_proj(hf, hb, p, hbt, x2, mod, g_head_a, g_norm2, w_out_b, seq)

    cw8 = jnp.zeros((8, D_FF), F32).at[:3].set(conv_w[0])
    act = _up_proj(h2, w_up_b, cw8, conv_b, seq)
    out = _down_proj(act, w_down_b, x1, mod, g_final[None, :], seq)
    return out.reshape(batch, seq, d)
```

```python
import functools

import jax
import jax.numpy as jnp
import numpy as np
from jax import lax
from jax.experimental import pallas as pl
from jax.experimental.pallas import tpu as pltpu

F32 = jnp.float32
BF16 = jnp.bfloat16

D_MODEL = 2048
GRID_W = 64
NH_A = 4
DH_A = 256
W_A = NH_A * DH_A
NH_B = 8
DH_B = 128
W_B = NH_B * DH_B
WIN_ROWS = 8
WIN_COLS = 16
D_FF = 5632
N_MOD = 6
EPS = 1e-6
NEG_BIG = -1e30

CHUNK_A = 256

TM_IN = 1024
TN_IN = 1024
TM_OUT = 512
ROW_CHUNK = 256
TM_UP = 1024
TN_UP = D_FF // 2
COL_CHUNK = 256
HALO = 16
TM_DOWN = 512
NA_GROUP = 8
NA_WINDOW = NA_GROUP + WIN_ROWS
NA_LOOKAHEAD = 4

V7X_VMEM_BYTES = 64 * 1024 * 1024
VMEM_TEMP_ALLOWANCE = 12 * 1024 * 1024
DOWN_TEMP_ALLOWANCE = 6 * 1024 * 1024


def _vmem_limit(block_bytes, scratch_bytes=0):
    est = 2 * block_bytes + scratch_bytes + VMEM_TEMP_ALLOWANCE
    return int(min(est, V7X_VMEM_BYTES - 4 * 1024 * 1024))


def _nbytes(shape, dtype):
    n = 1
    for s in shape:
        n *= s
    return n * jnp.dtype(dtype).itemsize


def _params(sem, blocks, scratch=()):
    bb = sum(_nbytes(s, d) for s, d in blocks)
    sb = sum(_nbytes(s, d) for s, d in scratch)
    return pltpu.CompilerParams(dimension_semantics=sem, vmem_limit_bytes=_vmem_limit(bb, sb))


def _rms_mod(x, g, shift, scale):
    return x * lax.rsqrt(jnp.mean(x * x, axis=-1, keepdims=True) + EPS) * (g * (1.0 + scale)) + shift


TN_ADA = 1024


def _ada_kernel(ct_ref, w_ref, b_ref, o_ref, s_ref, *, batch):
    @pl.when(pl.program_id(0) == 0)
    def _():
        ct = ct_ref[...]
        s = ct * (1.0 / (1.0 + jnp.exp(-ct)))
        for b in range(batch):
            s_ref[b] = jnp.broadcast_to(s[:, b:b + 1], (D_MODEL, 128))

    row = lax.broadcasted_iota(jnp.int32, (8, 128), 0)
    for l in range(TN_ADA // 128):
        lanes = slice(l * 128, (l + 1) * 128)
        wl = w_ref[:, lanes]
        tile = jnp.zeros((8, 128), F32)
        for b in range(batch):
            tile = jnp.where(row == b, jnp.sum(wl * s_ref[b], axis=0, keepdims=True), tile)
        o_ref[:, lanes] = tile + b_ref[:, lanes]


def _ada(c, w_ada, b_ada):
    batch = c.shape[0]
    n = w_ada.shape[2]
    ct = jnp.zeros((D_MODEL, 8), F32).at[:, :batch].set(c.T)
    return pl.pallas_call(
        functools.partial(_ada_kernel, batch=batch),
        grid=(n // TN_ADA,),
        in_specs=[pl.BlockSpec((D_MODEL, 8), lambda j: (0, 0)),
                  pl.BlockSpec((None, D_MODEL, TN_ADA), lambda j: (0, 0, j)),
                  pl.BlockSpec((1, TN_ADA), lambda j: (0, j))],
        out_specs=pl.BlockSpec((8, TN_ADA), lambda j: (0, j)),
        out_shape=jax.ShapeDtypeStruct((8, n), F32),
        scratch_shapes=[pltpu.VMEM((batch, D_MODEL, 128), F32)],
        compiler_params=_params(("arbitrary",), [((D_MODEL, TN_ADA), F32), ((D_MODEL, 128), F32)],
                                [((batch, D_MODEL, 128), F32)]),
        name="ada",
    )(ct, w_ada, b_ada)


GATE_LO = 4 * W_A
GATE_HI = GATE_LO + 4 * NH_A
TM_PREP = 512
_NT = (((1,), (1,)), ((), ()))


def _win_prep_kernel(w_ref, o_ref):
    o_ref[...] = w_ref[...].astype(BF16)


def _win_prep(w_t):
    _, cols, d = w_t.shape
    ng = GATE_HI - GATE_LO
    n = cols - ng
    assert GATE_LO % TM_PREP == 0 and n % TM_PREP == 0 and TM_PREP % ng == 0

    def src_row(i):
        past = (i >= GATE_LO // TM_PREP).astype(jnp.int32)
        return (0, (i * (TM_PREP // ng) + past) * ng, 0)

    return pl.pallas_call(
        _win_prep_kernel,
        grid=(n // TM_PREP,),
        in_specs=[pl.BlockSpec((None, pl.Element(TM_PREP), pl.Element(d)), src_row)],
        out_specs=pl.BlockSpec((TM_PREP, d), lambda i: (i, 0)),
        out_shape=jax.ShapeDtypeStruct((n, d), BF16),
        compiler_params=_params(("parallel",), [((TM_PREP, d), F32), ((TM_PREP, d), BF16)]),
        name="w_in_prep",
    )(w_t)


def _in_kernel(x_ref, mod_ref, g_ref, w_ref, wg_ref, bg_ref, p_ref, gates_ref, h_ref):
    j = pl.program_id(1)

    @pl.when(j == 0)
    def _():
        for c in range(TM_IN // ROW_CHUNK):
            rows = slice(c * ROW_CHUNK, (c + 1) * ROW_CHUNK)
            h = _rms_mod(x_ref[rows, :], g_ref[...], mod_ref[0, 0:1, :], mod_ref[0, 1:2, :])
            hb = h.astype(BF16)
            h_ref[rows, :] = hb
            p_ref[rows, :] = lax.dot_general(hb, w_ref[...], _NT,
                                             preferred_element_type=F32).astype(BF16)
            gates_ref[rows, :] = lax.dot_general(hb, wg_ref[...], _NT,
                                                 preferred_element_type=F32) + bg_ref[...]

    @pl.when(j > 0)
    def _():
        p_ref[...] = lax.dot_general(h_ref[...], w_ref[...], _NT,
                                     preferred_element_type=F32).astype(BF16)


def _in_proj(x2, mod, g1, w_main, w_g, b_g, seq):
    t = x2.shape[0]
    n = w_main.shape[0]
    tiles_per_seq = seq // TM_IN
    return pl.pallas_call(
        _in_kernel,
        grid=(t // TM_IN, n // TN_IN),
        in_specs=[pl.BlockSpec((TM_IN, D_MODEL), lambda i, j: (i, 0)),
                  pl.BlockSpec((1, N_MOD, D_MODEL), lambda i, j: (i // tiles_per_seq, 0, 0)),
                  pl.BlockSpec((1, D_MODEL), lambda i, j: (0, 0)),
                  pl.BlockSpec((TN_IN, D_MODEL), lambda i, j: (j, 0)),
                  pl.BlockSpec((256, D_MODEL), lambda i, j: (0, 0)),
                  pl.BlockSpec((1, 256), lambda i, j: (0, 0))],
        out_specs=[pl.BlockSpec((None, TM_IN, TN_IN), lambda i, j: (j, i, 0)),
                   pl.BlockSpec((TM_IN, 256), lambda i, j: (i, 0))],
        out_shape=[jax.ShapeDtypeStruct((n // TN_IN, t, TN_IN), BF16),
                   jax.ShapeDtypeStruct((t, 256), F32)],
        scratch_shapes=[pltpu.VMEM((TM_IN, D_MODEL), BF16)],
        compiler_params=_params(
            ("parallel", "arbitrary"),
            [((TM_IN, D_MODEL), F32), ((D_MODEL, TN_IN), BF16), ((D_MODEL, 256), BF16),
             ((TM_IN, TN_IN), BF16), ((TM_IN, 256), F32)],
            [((TM_IN, D_MODEL), BF16)]),
        name="in_proj",
    )(x2, mod, g1, w_main, w_g, b_g)


def _seg_scan(x, row, op, fill, reverse):
    n = x.shape[0]
    d = 1
    while d < n:
        if reverse:
            y = pltpu.roll(x, n - d, axis=0)
            x = op(x, jnp.where(row < n - d, y, fill))
        else:
            y = pltpu.roll(x, d, axis=0)
            x = op(x, jnp.where(row >= d, y, fill))
        d *= 2
    return x


def _gate_vectors(g_ref, m_ref, reverse):
    n = CHUNK_A
    gi = g_ref[:, 0:128]
    gf = g_ref[:, 128:256]
    lf = jnp.minimum(gf, 0.0) - jnp.log1p(jnp.exp(-jnp.abs(gf)))
    row = lax.broadcasted_iota(jnp.int32, (n, 128), 0)
    bc = _seg_scan(lf, row, jnp.add, 0.0, reverse)
    rb = gi - bc
    cm = _seg_scan(rb, row, jnp.maximum, -jnp.inf, reverse)
    last = 0 if reverse else n - 1
    gsum = bc[last:last + 1, :]
    m_loc = gsum + cm[last:last + 1, :]
    m = m_ref[...]
    m_inter = bc + m
    m_t = jnp.maximum(m_inter, bc + cm)
    m_new = jnp.maximum(gsum + m, m_loc)
    m_ref[...] = m_new
    col_a = bc - m_t
    a = jnp.exp(m_inter - m_t)
    e = jnp.exp(-m_t)
    w = jnp.exp(gsum + rb - m_new)
    dec = jnp.exp(gsum + m - m_new)
    return col_a, a, e, w, dec, rb.T


def _cast_slab_specs(w, steps, imap):
    _, rows, cols = w.shape
    slab = rows // steps
    assert slab * steps == rows and slab % 16 == 0
    in_spec = pl.BlockSpec((None, slab, cols), lambda *ids: (0,) + tuple(imap(*ids)))
    out_spec = pl.BlockSpec((slab, cols), imap)
    return in_spec, out_spec, jax.ShapeDtypeStruct((rows, cols), BF16), ((slab, cols), F32)


def _mlstm_kernel(qf, kf, vf, qb, kb, vb, gf, gb, wsrc, hf, hb, wdst, ct_ref, mf_ref, mb_ref):
    @pl.when(pl.program_id(1) == 0)
    def _():
        ct_ref[...] = jnp.zeros(ct_ref.shape, F32)
        mf_ref[...] = jnp.full((1, 128), NEG_BIG, F32)
        mb_ref[...] = jnp.full((1, 128), NEG_BIG, F32)

    wdst[...] = wsrc[...].astype(BF16)

    n = CHUNK_A
    r = lax.broadcasted_iota(jnp.int32, (n, n), 0)
    c = lax.broadcasted_iota(jnp.int32, (n, n), 1)
    ones_cols = jnp.ones((n, 128), BF16)
    ones_rows = jnp.ones((128, n), BF16)
    seqs = []
    for head in range(NH_A):
        seqs.append((qf, kf, vf, False, hf, head, head, r >= c))
        seqs.append((qb, kb, vb, True, hb, head, NH_A + head, r <= c))

    loaded = []
    for q_ref, k_ref, v_ref, _, _, head, sid, _ in seqs:
        sl = slice(head * DH_A, (head + 1) * DH_A)
        q = q_ref[:, sl]
        k = k_ref[:, sl] * (DH_A ** -0.5)
        v = v_ref[:, sl]
        ct_old = ct_ref[sid]
        qk = lax.dot_general(q, k, (((1,), (1,)), ((), ())), preferred_element_type=F32)
        qc = lax.dot_general(q, ct_old.astype(BF16), (((1,), (1,)), ((), ())),
                             preferred_element_type=F32)
        loaded.append((k, v, ct_old, qk, qc))

    gate_vecs = {False: _gate_vectors(gf, mf_ref, reverse=False),
                 True: _gate_vectors(gb, mb_ref, reverse=True)}
    for (_, _, _, rev, h_ref, head, sid, mask), (k, v, ct_old, qk, qc) in zip(seqs, loaded):
        gates = gate_vecs[rev]
        sl = slice(head * DH_A, (head + 1) * DH_A)
        col_a, a, e = (g[:, sid:sid + 1] for g in gates[:3])
        row_b = gates[5][sid:sid + 1, :]
        s = qk * jnp.exp(jnp.where(mask, col_a + row_b, -jnp.inf))
        v_ext = jnp.concatenate([v, ones_cols], axis=1)
        sv = jnp.dot(s.astype(BF16), v_ext, preferred_element_type=F32) + a * qc
        inv = 1.0 / jnp.maximum(jnp.abs(sv[:, DH_A:]), e)
        h_ref[:, sl] = (sv[:, :DH_A] * jnp.concatenate([inv, inv], axis=1)).astype(BF16)

    for (_, _, _, rev, _, _, sid, _), (k, v, ct_old, _, _) in zip(seqs, loaded):
        gates = gate_vecs[rev]
        w = gates[3][:, sid:sid + 1]
        dec = gates[4][:, sid:sid + 1]
        kw = (w * k.astype(F32)).astype(BF16)
        vt_ext = jnp.concatenate([v.T, ones_rows], axis=0)
        ct_ref[sid] = dec * ct_old + jnp.dot(vt_ext, kw, preferred_element_type=F32)


def _mlstm(p, gates, w_cast, batch, seq):
    nc = seq // CHUNK_A
    t = p.shape[1]
    cast_in, cast_out, cast_shape, cast_blk = _cast_slab_specs(w_cast, batch * nc,
                                                               lambda b, k: (b * nc + k, 0))

    def blk(col, rev):
        if rev:
            return pl.BlockSpec((None, CHUNK_A, W_A), lambda b, k: (col, b * nc + nc - 1 - k, 0))
        return pl.BlockSpec((None, CHUNK_A, W_A), lambda b, k: (col, b * nc + k, 0))

    def outblk(rev):
        if rev:
            return pl.BlockSpec((CHUNK_A, W_A), lambda b, k: (b * nc + nc - 1 - k, 0))
        return pl.BlockSpec((CHUNK_A, W_A), lambda b, k: (b * nc + k, 0))

    gatespec = lambda rev: pl.BlockSpec(
        (CHUNK_A, 256), (lambda b, k: (b * nc + nc - 1 - k, 0)) if rev else (lambda b, k: (b * nc + k, 0)))
    return pl.pallas_call(
        _mlstm_kernel,
        grid=(batch, nc),
        in_specs=[blk(0, False), blk(1, False), blk(2, False),
                  blk(0, True), blk(1, True), blk(2, True),
                  gatespec(False), gatespec(True), cast_in],
        out_specs=[outblk(False), outblk(True), cast_out],
        out_shape=[jax.ShapeDtypeStruct((t, W_A), BF16), jax.ShapeDtypeStruct((t, W_A), BF16),
                   cast_shape],
        scratch_shapes=[pltpu.VMEM((2 * NH_A, DH_A + 128, DH_A), F32),
                        pltpu.VMEM((1, 128), F32), pltpu.VMEM((1, 128), F32)],
        compiler_params=_params(
            ("parallel", "arbitrary"),
            [((CHUNK_A, W_A), BF16)] * 8 + [((CHUNK_A, 256), F32)] * 2 + [cast_blk, cast_blk],
            [((2 * NH_A, DH_A + 128, DH_A), F32)]),
        name="mlstm",
    )(p, p, p, p, p, p, gates, gates, w_cast)


def _na_window_start(g, rows):
    return jnp.clip(g * NA_GROUP - WIN_ROWS // 2, 0, rows - NA_WINDOW)


def _natten_kernel(q_ref, kbuf, vbuf, bias_ref, wsrc1, wsrc2, o_ref, wdst1, wdst2, *, rows):
    wdst1[...] = wsrc1[...].astype(BF16)
    wdst2[...] = wsrc2[...].astype(BF16)
    g = pl.program_id(1)
    wstart = _na_window_start(g, rows)
    scale = DH_B ** -0.5
    starts, variants = [], []
    for i in range(NA_GROUP):
        r = g * NA_GROUP + i
        rs = jnp.clip(r - WIN_ROWS // 2, 0, rows - WIN_ROWS)
        starts.append(pl.multiple_of((rs - wstart) * GRID_W, GRID_W))
        variants.append(rs - r + (WIN_ROWS - 1))

    def scores(i, h):
        hs = slice(h * DH_B, (h + 1) * DH_B)
        q = q_ref[i * GRID_W:(i + 1) * GRID_W, hs]
        kw = kbuf[pl.ds(starts[i], WIN_ROWS * GRID_W), hs]
        s = lax.dot_general(q, kw, (((1,), (1,)), ((), ())), preferred_element_type=F32)
        s = s * scale + bias_ref[variants[i], h]
        p = jnp.exp(s - jnp.max(s, axis=-1, keepdims=True))
        return p.astype(BF16), 1.0 / jnp.sum(p, axis=-1, keepdims=True)

    def weighted_sum(i, h, p, inv_l):
        hs = slice(h * DH_B, (h + 1) * DH_B)
        vw = vbuf[pl.ds(starts[i], WIN_ROWS * GRID_W), hs]
        o = jnp.dot(p, vw, preferred_element_type=F32)
        o_ref[i * GRID_W:(i + 1) * GRID_W, hs] = (o * inv_l).astype(BF16)

    tiles = [(i, h) for i in range(NA_GROUP) for h in range(NH_B)]
    pending = []
    for idx in range(len(tiles) + NA_LOOKAHEAD):
        if idx < len(tiles):
            pending.append(scores(*tiles[idx]))
        if idx >= NA_LOOKAHEAD:
            weighted_sum(*tiles[idx - NA_LOOKAHEAD], *pending[idx - NA_LOOKAHEAD])
            pending[idx - NA_LOOKAHEAD] = None


def _natten(p, bias, w_cast1, w_cast2, batch, seq):
    rows = seq // GRID_W
    ng = rows // NA_GROUP
    blk = NA_GROUP * GRID_W
    win = NA_WINDOW * GRID_W
    t = p.shape[1]
    qcol, kcol, vcol = 4, 5, 6

    def window(col):
        def imap(b, g):
            return (col, (b * rows + _na_window_start(g, rows)) * GRID_W, 0)
        return pl.BlockSpec((None, pl.Element(win), pl.Element(W_B)), imap)

    step = lambda b, g: (b * ng + g, 0)
    c1_in, c1_out, c1_shape, c1_blk = _cast_slab_specs(w_cast1, batch * ng, step)
    c2_in, c2_out, c2_shape, c2_blk = _cast_slab_specs(w_cast2, batch * ng, step)
    return pl.pallas_call(
        functools.partial(_natten_kernel, rows=rows),
        grid=(batch, ng),
        in_specs=[pl.BlockSpec((None, blk, W_B), lambda b, g: (qcol, b * ng + g, 0)),
                  window(kcol), window(vcol),
                  pl.BlockSpec(bias.shape, lambda b, g: (0, 0, 0, 0)), c1_in, c2_in],
        out_specs=[pl.BlockSpec((blk, W_B), step), c1_out, c2_out],
        out_shape=[jax.ShapeDtypeStruct((t, W_B), BF16), c1_shape, c2_shape],
        compiler_params=_params(
            ("parallel", "arbitrary"),
            [((blk, W_B), BF16)] * 2 + [((win, W_B), BF16)] * 2 + [(bias.shape, F32)]
            + [c1_blk, c1_blk, c2_blk, c2_blk]),
        name="natten",
    )(p, p, p, bias, w_cast1, w_cast2)


def _natten_bias(rpb):
    cols = np.arange(GRID_W)
    cstart = np.clip(cols - WIN_COLS // 2, 0, GRID_W - WIN_COLS)
    inside = (cols[None, :] >= cstart[:, None]) & (cols[None, :] < cstart[:, None] + WIN_COLS)
    dc = cols[None, :] - cols[:, None] + (WIN_COLS - 1)
    onehot = (np.arange(2 * WIN_COLS - 1)[:, None, None] == dc[None]) & inside[None]
    toep = jnp.einsum('hdj,jck->hdck', rpb.astype(F32), onehot.astype(np.float32),
                      precision=lax.Precision.HIGHEST)
    toep = toep + np.where(inside, 0.0, NEG_BIG).astype(np.float32)
    tiles = []
    for variant in range(WIN_ROWS):
        t = toep[:, variant:variant + WIN_ROWS].transpose(0, 2, 1, 3)
        tiles.append(t.reshape(NH_B, GRID_W, WIN_ROWS * GRID_W))
    return jnp.stack(tiles, axis=0)


def _out_kernel(hf_ref, hb_ref, oa_ref, hbt_ref, x_ref, mod_ref, gh_ref, gn_ref, w_ref,
                x1_ref, h2_ref):
    for c in range(TM_OUT // ROW_CHUNK):
        rows = slice(c * ROW_CHUNK, (c + 1) * ROW_CHUNK)
        hs = hf_ref[rows, :].astype(F32) + hb_ref[rows, :].astype(F32)
        parts = []
        for head in range(NH_A):
            hh = hs[:, head * DH_A:(head + 1) * DH_A]
            parts.append(hh * lax.rsqrt(jnp.mean(hh * hh, axis=-1, keepdims=True) + EPS))
        hn = jnp.concatenate(parts, axis=-1) * gh_ref[...]
        oa = oa_ref[rows, :].astype(F32)
        ha = (hn * (1.0 / (1.0 + jnp.exp(-oa)))).astype(BF16)
        mixed = jnp.dot(jnp.concatenate([ha, hbt_ref[rows, :]], axis=1), w_ref[...],
                        preferred_element_type=F32)
        x1 = x_ref[rows, :] + mod_ref[0, 2:3, :] * mixed
        x1_ref[rows, :] = x1
        h2_ref[rows, :] = _rms_mod(x1, gn_ref[...], mod_ref[0, 3:4, :], mod_ref[0, 4:5, :]).astype(BF16)


def _out_proj(hf, hb, p, hbt, x2, mod, gh, g2, w_o, seq):
    t = x2.shape[0]
    tiles_per_seq = seq // TM_OUT
    row = lambda i: (i, 0)
    resident = pl.Buffered(1)
    return pl.pallas_call(
        _out_kernel,
        grid=(t // TM_OUT,),
        in_specs=[pl.BlockSpec((TM_OUT, W_A), row), pl.BlockSpec((TM_OUT, W_A), row),
                  pl.BlockSpec((None, TM_OUT, W_A), lambda i: (3, i, 0)),
                  pl.BlockSpec((TM_OUT, W_B), row),
                  pl.BlockSpec((TM_OUT, D_MODEL), row),
                  pl.BlockSpec((1, N_MOD, D_MODEL), lambda i: (i // tiles_per_seq, 0, 0)),
                  pl.BlockSpec((1, W_A), lambda i: (0, 0)),
                  pl.BlockSpec((1, D_MODEL), lambda i: (0, 0)),
                  pl.BlockSpec((W_A + W_B, D_MODEL), lambda i: (0, 0), pipeline_mode=resident)],
        out_specs=[pl.BlockSpec((TM_OUT, D_MODEL), row), pl.BlockSpec((TM_OUT, D_MODEL), row)],
        out_shape=[jax.ShapeDtypeStruct((t, D_MODEL), F32), jax.ShapeDtypeStruct((t, D_MODEL), BF16)],
        compiler_params=_params(
            ("parallel",),
            [((TM_OUT, W_A), BF16)] * 4 + [((TM_OUT, D_MODEL), F32)] * 2
            + [((TM_OUT, D_MODEL), BF16)] + [((W_A, D_MODEL), BF16)]),
        name="out_proj",
    )(hf, hb, p, hbt, x2, mod, gh, g2, w_o)


def _up_kernel(hp_ref, h_ref, hn_ref, wu_ref, wg_ref, cw_ref, cb_ref, act_ref, hext_ref, *,
               tiles_per_seq):
    pos = pl.program_id(1) % tiles_per_seq
    n = TM_UP + 2 * HALO
    hext_ref[0:HALO, :] = jnp.where(pos > 0, hp_ref[...], jnp.zeros_like(hp_ref))
    hext_ref[HALO:HALO + TM_UP, :] = h_ref[...]
    hext_ref[HALO + TM_UP:, :] = jnp.where(pos < tiles_per_seq - 1, hn_ref[...], jnp.zeros_like(hn_ref))
    for c in range(TN_UP // COL_CHUNK):
        cols = slice(c * COL_CHUNK, (c + 1) * COL_CHUNK)
        u = jnp.dot(h_ref[...], wu_ref[:, cols], preferred_element_type=F32)
        g = jnp.dot(hext_ref[...], wg_ref[:, cols], preferred_element_type=F32)
        g_prev = pltpu.roll(g, 1, axis=0)[HALO:HALO + TM_UP, :]
        g_next = pltpu.roll(g, n - 1, axis=0)[HALO:HALO + TM_UP, :]
        gc = (g_prev * cw_ref[0:1, cols] + g[HALO:HALO + TM_UP, :] * cw_ref[1:2, cols]
              + g_next * cw_ref[2:3, cols] + cb_ref[:, cols])
        gelu = 0.5 * gc * (1.0 + lax.erf(gc * (2.0 ** -0.5)))
        act_ref[:, cols] = (gelu * u).astype(BF16)


def _up_proj(h2, w_up, conv_w, conv_b, seq):
    t = h2.shape[0]
    tiles_per_seq = seq // TM_UP
    hb = TM_UP // HALO
    nh = t // HALO
    ngroups = D_FF // TN_UP
    resident = pl.Buffered(1)
    return pl.pallas_call(
        functools.partial(_up_kernel, tiles_per_seq=tiles_per_seq),
        grid=(ngroups, t // TM_UP),
        in_specs=[pl.BlockSpec((HALO, D_MODEL), lambda j, i: (jnp.maximum(i * hb - 1, 0), 0)),
                  pl.BlockSpec((TM_UP, D_MODEL), lambda j, i: (i, 0)),
                  pl.BlockSpec((HALO, D_MODEL), lambda j, i: (jnp.minimum((i + 1) * hb, nh - 1), 0)),
                  pl.BlockSpec((D_MODEL, TN_UP), lambda j, i: (0, j), pipeline_mode=resident),
                  pl.BlockSpec((D_MODEL, TN_UP), lambda j, i: (0, ngroups + j), pipeline_mode=resident),
                  pl.BlockSpec((8, TN_UP), lambda j, i: (0, j)),
                  pl.BlockSpec((1, TN_UP), lambda j, i: (0, j))],
        out_specs=pl.BlockSpec((TM_UP, TN_UP), lambda j, i: (i, j)),
        out_shape=jax.ShapeDtypeStruct((t, D_FF), BF16),
        scratch_shapes=[pltpu.VMEM((TM_UP + 2 * HALO, D_MODEL), BF16)],
        compiler_params=_params(
            ("arbitrary", "arbitrary"),
            [((TM_UP + 2 * HALO, D_MODEL), BF16), ((D_MODEL, TN_UP), BF16), ((TM_UP, TN_UP), BF16)],
            [((TM_UP + 2 * HALO, D_MODEL), BF16)]),
        name="up_proj",
    )(h2, h2, h2, w_up, w_up, conv_w, conv_b)


def _down_kernel(a_ref, w_ref, x_ref, mod_ref, g_ref, o_ref):
    for c in range(TM_DOWN // ROW_CHUNK):
        rows = slice(c * ROW_CHUNK, (c + 1) * ROW_CHUNK)
        y = x_ref[rows, :] + mod_ref[0, 5:6, :] * jnp.dot(a_ref[rows, :], w_ref[...],
                                                           preferred_element_type=F32)
        o_ref[rows, :] = y * lax.rsqrt(jnp.mean(y * y, axis=-1, keepdims=True) + EPS) * g_ref[...]


def _down_proj(act, w_d, x1, mod, gfin, seq):
    t = x1.shape[0]
    tiles_per_seq = seq // TM_DOWN
    return pl.pallas_call(
        _down_kernel,
        grid=(t // TM_DOWN,),
        in_specs=[pl.BlockSpec((TM_DOWN, D_FF), lambda i: (i, 0)),
                  pl.BlockSpec((D_FF, D_MODEL), lambda i: (0, 0), pipeline_mode=pl.Buffered(1)),
                  pl.BlockSpec((TM_DOWN, D_MODEL), lambda i: (i, 0)),
                  pl.BlockSpec((1, N_MOD, D_MODEL), lambda i: (i // tiles_per_seq, 0, 0)),
                  pl.BlockSpec((1, D_MODEL), lambda i: (0, 0))],
        out_specs=pl.BlockSpec((TM_DOWN, D_MODEL), lambda i: (i, 0)),
        out_shape=jax.ShapeDtypeStruct((t, D_MODEL), F32),
        compiler_params=pltpu.CompilerParams(
            dimension_semantics=("parallel",),
            vmem_limit_bytes=_nbytes((D_FF, D_MODEL), BF16) + 2 * (
                _nbytes((TM_DOWN, D_FF), BF16) + 2 * _nbytes((TM_DOWN, D_MODEL), F32))
            + DOWN_TEMP_ALLOWANCE),
        name="down_proj",
    )(act, w_d, x1, mod, gfin)


def kernel(x, c, w_ada, b_ada, g_norm1, w_in, b_gates, g_head_a, rpb, w_out, g_norm2, w_up, conv_w,
           conv_b, w_down, g_final):
    batch, seq, d = x.shape
    assert w_ada.shape[0] == 1
    assert d == D_MODEL and seq % CHUNK_A == 0 and seq % (NA_GROUP * GRID_W) == 0
    x2 = x.reshape(batch * seq, d)
    mod = _ada(c, w_ada, b_ada)[:batch].reshape(batch, N_MOD, d)

    w_in_t = jnp.swapaxes(w_in, 1, 2)
    w_main = _win_prep(w_in_t)
    lane_of = np.concatenate([np.arange(0, 4), np.arange(128, 132), np.arange(4, 8), np.arange(132, 136)])
    sel = np.zeros((4 * NH_A, 256), np.float32)
    sel[np.arange(4 * NH_A), lane_of] = 1.0
    w_g = jnp.dot(sel.T, w_in_t[0, GATE_LO:GATE_HI], precision=lax.Precision.HIGHEST).astype(BF16)
    b_g = jnp.dot(b_gates, sel, precision=lax.Precision.HIGHEST)

    p, gates = _in_proj(x2, mod, g_norm1, w_main, w_g, b_g, seq)
    hf, hb, w_up_b = _mlstm(p, gates, w_up, batch, seq)
    hbt, w_down_b, w_out_b = _natten(p, _natten_bias(rpb[0]), w_down, w_out, batch, seq)
    x1, h2 = _out_proj(hf, hb, p, hbt, x2, mod, g_head_a, g_norm2, w_out_b, seq)

    cw8 = jnp.zeros((8, D_FF), F32).at[:3].set(conv_w[0])
    act = _up_proj(h2, w_up_b, cw8, conv_b, seq)
    out = _down_proj(act, w_down_b, x1, mod, g_final[None, :], seq)
    return out.reshape(batch, seq, d)
```

```python
import functools
import math

import jax
import jax.numpy as jnp
import numpy as np
from jax import lax
from jax.experimental import pallas as pl
from jax.experimental.pallas import tpu as pltpu

F32 = jnp.float32
BF16 = jnp.bfloat16

D_MODEL = 2048
GRID_W = 64
NH_A = 4
DH_A = 256
W_A = NH_A * DH_A
NH_B = 8
DH_B = 128
W_B = NH_B * DH_B
WIN_ROWS = 8
WIN_COLS = 16
D_FF = 5632
N_MOD = 6
EPS = 1e-6
NEG_BIG = -1e30
LOG2_E = 1.4426950408889634

CHUNK_A = 256

TM_IN = 1024
TN_IN = 1024
TM_OUT = 512
ROW_CHUNK = 256
TM_UP = 1024
TN_UP = D_FF // 2
COL_CHUNK = 256
HALO = 16
TM_DOWN = 512
NA_GROUP = 16
NA_WINDOW = NA_GROUP + WIN_ROWS
NA_LOOKAHEAD = 4

V7X_VMEM_BYTES = 64 * 1024 * 1024
VMEM_TEMP_ALLOWANCE = 12 * 1024 * 1024
DOWN_TEMP_ALLOWANCE = 6 * 1024 * 1024


def _vmem_limit(block_bytes, scratch_bytes=0):
    est = 2 * block_bytes + scratch_bytes + VMEM_TEMP_ALLOWANCE
    return int(min(est, V7X_VMEM_BYTES - 4 * 1024 * 1024))


def _nbytes(shape, dtype):
    n = 1
    for s in shape:
        n *= s
    return n * jnp.dtype(dtype).itemsize


def _params(sem, blocks, scratch=()):
    bb = sum(_nbytes(s, d) for s, d in blocks)
    sb = sum(_nbytes(s, d) for s, d in scratch)
    return pltpu.CompilerParams(dimension_semantics=sem, vmem_limit_bytes=_vmem_limit(bb, sb))


def _rms_mod(x, g, shift, scale):
    return x * lax.rsqrt(jnp.mean(x * x, axis=-1, keepdims=True) + EPS) * (g * (1.0 + scale)) + shift


TN_ADA = 1024


def _ada_kernel(ct_ref, w_ref, b_ref, o_ref, s_ref, *, batch):
    @pl.when(pl.program_id(0) == 0)
    def _():
        ct = ct_ref[...]
        s = ct * (1.0 / (1.0 + jnp.exp(-ct)))
        for b in range(batch):
            s_ref[b] = jnp.broadcast_to(s[:, b:b + 1], (D_MODEL, 128))

    row = lax.broadcasted_iota(jnp.int32, (8, 128), 0)
    for l in range(TN_ADA // 128):
        lanes = slice(l * 128, (l + 1) * 128)
        wl = w_ref[:, lanes]
        tile = jnp.zeros((8, 128), F32)
        for b in range(batch):
            tile = jnp.where(row == b, jnp.sum(wl * s_ref[b], axis=0, keepdims=True), tile)
        o_ref[:, lanes] = tile + b_ref[:, lanes]


def _ada(c, w_ada, b_ada):
    batch = c.shape[0]
    n = w_ada.shape[2]
    ct = jnp.zeros((D_MODEL, 8), F32).at[:, :batch].set(c.T)
    return pl.pallas_call(
        functools.partial(_ada_kernel, batch=batch),
        grid=(n // TN_ADA,),
        in_specs=[pl.BlockSpec((D_MODEL, 8), lambda j: (0, 0)),
                  pl.BlockSpec((None, D_MODEL, TN_ADA), lambda j: (0, 0, j)),
                  pl.BlockSpec((1, TN_ADA), lambda j: (0, j))],
        out_specs=pl.BlockSpec((8, TN_ADA), lambda j: (0, j)),
        out_shape=jax.ShapeDtypeStruct((8, n), F32),
        scratch_shapes=[pltpu.VMEM((batch, D_MODEL, 128), F32)],
        compiler_params=_params(("arbitrary",), [((D_MODEL, TN_ADA), F32), ((D_MODEL, 128), F32)],
                                [((batch, D_MODEL, 128), F32)]),
        name="ada",
    )(ct, w_ada, b_ada)


GATE_LO = 4 * W_A
GATE_HI = GATE_LO + 4 * NH_A
TM_PREP = 512
_NT = (((1,), (1,)), ((), ()))


def _win_prep_kernel(w_ref, o_ref):
    o_ref[...] = w_ref[...].astype(BF16)


def _win_prep(w_t):
    _, cols, d = w_t.shape
    ng = GATE_HI - GATE_LO
    n = cols - ng
    assert GATE_LO % TM_PREP == 0 and n % TM_PREP == 0 and TM_PREP % ng == 0

    def src_row(i):
        past = (i >= GATE_LO // TM_PREP).astype(jnp.int32)
        return (0, (i * (TM_PREP // ng) + past) * ng, 0)

    return pl.pallas_call(
        _win_prep_kernel,
        grid=(n // TM_PREP,),
        in_specs=[pl.BlockSpec((None, pl.Element(TM_PREP), pl.Element(d)), src_row)],
        out_specs=pl.BlockSpec((TM_PREP, d), lambda i: (i, 0)),
        out_shape=jax.ShapeDtypeStruct((n, d), BF16),
        compiler_params=_params(("parallel",), [((TM_PREP, d), F32), ((TM_PREP, d), BF16)]),
        name="w_in_prep",
    )(w_t)


def _in_kernel(x_ref, mod_ref, g_ref, w_ref, wg_ref, bg_ref, p_ref, gates_ref, h_ref):
    j = pl.program_id(1)

    @pl.when(j == 0)
    def _():
        for c in range(TM_IN // ROW_CHUNK):
            rows = slice(c * ROW_CHUNK, (c + 1) * ROW_CHUNK)
            h = _rms_mod(x_ref[rows, :], g_ref[...], mod_ref[0, 0:1, :], mod_ref[0, 1:2, :])
            hb = h.astype(BF16)
            h_ref[rows, :] = hb
            p_ref[rows, :] = lax.dot_general(hb, w_ref[...], _NT,
                                             preferred_element_type=F32).astype(BF16)
            gates_ref[rows, :] = lax.dot_general(hb, wg_ref[...], _NT,
                                                 preferred_element_type=F32) + bg_ref[...]

    @pl.when(j > 0)
    def _():
        p_ref[...] = lax.dot_general(h_ref[...], w_ref[...], _NT,
                                     preferred_element_type=F32).astype(BF16)


def _in_proj(x2, mod, g1, w_main, w_g, b_g, seq):
    t = x2.shape[0]
    n = w_main.shape[0]
    tiles_per_seq = seq // TM_IN
    return pl.pallas_call(
        _in_kernel,
        grid=(t // TM_IN, n // TN_IN),
        in_specs=[pl.BlockSpec((TM_IN, D_MODEL), lambda i, j: (i, 0)),
                  pl.BlockSpec((1, N_MOD, D_MODEL), lambda i, j: (i // tiles_per_seq, 0, 0)),
                  pl.BlockSpec((1, D_MODEL), lambda i, j: (0, 0)),
                  pl.BlockSpec((TN_IN, D_MODEL), lambda i, j: (j, 0)),
                  pl.BlockSpec((256, D_MODEL), lambda i, j: (0, 0)),
                  pl.BlockSpec((1, 256), lambda i, j: (0, 0))],
        out_specs=[pl.BlockSpec((None, TM_IN, TN_IN), lambda i, j: (j, i, 0)),
                   pl.BlockSpec((TM_IN, 256), lambda i, j: (i, 0))],
        out_shape=[jax.ShapeDtypeStruct((n // TN_IN, t, TN_IN), BF16),
                   jax.ShapeDtypeStruct((t, 256), F32)],
        scratch_shapes=[pltpu.VMEM((TM_IN, D_MODEL), BF16)],
        compiler_params=_params(
            ("parallel", "arbitrary"),
            [((TM_IN, D_MODEL), F32), ((D_MODEL, TN_IN), BF16), ((D_MODEL, 256), BF16),
             ((TM_IN, TN_IN), BF16), ((TM_IN, 256), F32)],
            [((TM_IN, D_MODEL), BF16)]),
        name="in_proj",
    )(x2, mod, g1, w_main, w_g, b_g)


def _seg_scan(x, row, op, fill, reverse):
    n = x.shape[0]
    d = 1
    while d < n:
        if reverse:
            y = pltpu.roll(x, n - d, axis=0)
            x = op(x, jnp.where(row < n - d, y, fill))
        else:
            y = pltpu.roll(x, d, axis=0)
            x = op(x, jnp.where(row >= d, y, fill))
        d *= 2
    return x


def _gate_vectors(g_ref, m_ref, reverse):
    n = CHUNK_A
    gi = g_ref[:, 0:128]
    gf = g_ref[:, 128:256]
    lf = jnp.minimum(gf, 0.0) - jnp.log(1.0 + jnp.exp(-jnp.abs(gf)))
    row = lax.broadcasted_iota(jnp.int32, (n, 128), 0)
    bc = _seg_scan(lf, row, jnp.add, 0.0, reverse)
    rb = gi - bc
    cm = _seg_scan(rb, row, jnp.maximum, -jnp.inf, reverse)
    last = 0 if reverse else n - 1
    gsum = bc[last:last + 1, :]
    m_loc = gsum + cm[last:last + 1, :]
    m = m_ref[...]
    m_inter = bc + m
    m_t = jnp.maximum(m_inter, bc + cm)
    m_new = jnp.maximum(gsum + m, m_loc)
    m_ref[...] = m_new
    col_a = bc - m_t - math.log(DH_A ** 0.5)
    a = jnp.exp(m_inter - m_t)
    e = jnp.exp(-m_t)
    w = jnp.exp(gsum + rb - m_new) * (DH_A ** -0.5)
    dec = jnp.exp(gsum + m - m_new)
    return col_a, a, e, w, dec, rb.T


def _cast_slab_specs(w, steps, imap):
    _, rows, cols = w.shape
    slab = rows // steps
    assert slab * steps == rows and slab % 16 == 0
    in_spec = pl.BlockSpec((None, slab, cols), lambda *ids: (0,) + tuple(imap(*ids)))
    out_spec = pl.BlockSpec((slab, cols), imap)
    return in_spec, out_spec, jax.ShapeDtypeStruct((rows, cols), BF16), ((slab, cols), F32)


def _mlstm_kernel(qf, kf, vf, qb, kb, vb, gf, gb, wsrc, hf, hb, wdst, ct_ref, mf_ref, mb_ref):
    @pl.when(pl.program_id(1) == 0)
    def _():
        ct_ref[...] = jnp.zeros(ct_ref.shape, F32)
        mf_ref[...] = jnp.full((1, 128), NEG_BIG, F32)
        mb_ref[...] = jnp.full((1, 128), NEG_BIG, F32)

    wdst[...] = wsrc[...].astype(BF16)

    n = CHUNK_A
    r = lax.broadcasted_iota(jnp.int32, (n, n), 0)
    c = lax.broadcasted_iota(jnp.int32, (n, n), 1)
    ones_cols = jnp.ones((n, 128), BF16)
    ones_rows = jnp.ones((128, n), BF16)
    seqs = []
    for head in range(NH_A):
        seqs.append((qf, kf, vf, False, hf, head, head, r >= c))
        seqs.append((qb, kb, vb, True, hb, head, NH_A + head, r <= c))

    loaded = []
    for q_ref, k_ref, v_ref, _, _, head, sid, _ in seqs:
        sl = slice(head * DH_A, (head + 1) * DH_A)
        q = q_ref[:, sl]
        k = k_ref[:, sl]
        v = v_ref[:, sl]
        ct_old = ct_ref[sid]
        qk = lax.dot_general(q, k, (((1,), (1,)), ((), ())), preferred_element_type=F32)
        qc = lax.dot_general(q, ct_old.astype(BF16), (((1,), (1,)), ((), ())),
                             preferred_element_type=F32)
        loaded.append((k, v, ct_old, qk, qc))

    gate_vecs = {False: _gate_vectors(gf, mf_ref, reverse=False),
                 True: _gate_vectors(gb, mb_ref, reverse=True)}
    for (_, _, _, rev, h_ref, head, sid, mask), (k, v, ct_old, qk, qc) in zip(seqs, loaded):
        gates = gate_vecs[rev]
        sl = slice(head * DH_A, (head + 1) * DH_A)
        col_a, a, e = (g[:, sid:sid + 1] for g in gates[:3])
        row_b = gates[5][sid:sid + 1, :]
        s = qk * jnp.exp(jnp.where(mask, col_a + row_b, -jnp.inf))
        v_ext = jnp.concatenate([v, ones_cols], axis=1)
        sv = jnp.dot(s.astype(BF16), v_ext, preferred_element_type=F32) + a * qc
        inv = 1.0 / jnp.maximum(jnp.abs(sv[:, DH_A:]), e)
        h_ref[:, sl] = (sv[:, :DH_A] * jnp.concatenate([inv, inv], axis=1)).astype(BF16)

    for (_, _, _, rev, _, _, sid, _), (k, v, ct_old, _, _) in zip(seqs, loaded):
        gates = gate_vecs[rev]
        w = gates[3][:, sid:sid + 1]
        dec = gates[4][:, sid:sid + 1]
        kw = (w * k.astype(F32)).astype(BF16)
        vt_ext = jnp.concatenate([v.T, ones_rows], axis=0)
        ct_ref[sid] = dec * ct_old + jnp.dot(vt_ext, kw, preferred_element_type=F32)


def _mlstm(p, gates, w_cast, batch, seq):
    nc = seq // CHUNK_A
    t = p.shape[1]
    cast_in, cast_out, cast_shape, cast_blk = _cast_slab_specs(w_cast, batch * nc,
                                                               lambda b, k: (b * nc + k, 0))

    def blk(col, rev):
        if rev:
            return pl.BlockSpec((None, CHUNK_A, W_A), lambda b, k: (col, b * nc + nc - 1 - k, 0))
        return pl.BlockSpec((None, CHUNK_A, W_A), lambda b, k: (col, b * nc + k, 0))

    def outblk(rev):
        if rev:
            return pl.BlockSpec((CHUNK_A, W_A), lambda b, k: (b * nc + nc - 1 - k, 0))
        return pl.BlockSpec((CHUNK_A, W_A), lambda b, k: (b * nc + k, 0))

    gatespec = lambda rev: pl.BlockSpec(
        (CHUNK_A, 256), (lambda b, k: (b * nc + nc - 1 - k, 0)) if rev else (lambda b, k: (b * nc + k, 0)))
    return pl.pallas_call(
        _mlstm_kernel,
        grid=(batch, nc),
        in_specs=[blk(0, False), blk(1, False), blk(2, False),
                  blk(0, True), blk(1, True), blk(2, True),
                  gatespec(False), gatespec(True), cast_in],
        out_specs=[outblk(False), outblk(True), cast_out],
        out_shape=[jax.ShapeDtypeStruct((t, W_A), BF16), jax.ShapeDtypeStruct((t, W_A), BF16),
                   cast_shape],
        scratch_shapes=[pltpu.VMEM((2 * NH_A, DH_A + 128, DH_A), F32),
                        pltpu.VMEM((1, 128), F32), pltpu.VMEM((1, 128), F32)],
        compiler_params=_params(
            ("parallel", "arbitrary"),
            [((CHUNK_A, W_A), BF16)] * 8 + [((CHUNK_A, 256), F32)] * 2 + [cast_blk, cast_blk],
            [((2 * NH_A, DH_A + 128, DH_A), F32)]),
        name="mlstm",
    )(p, p, p, p, p, p, gates, gates, w_cast)


def _na_window_start(g, rows):
    return jnp.clip(g * NA_GROUP - WIN_ROWS // 2, 0, rows - NA_WINDOW)


def _natten_kernel(q_ref, kbuf, vbuf, bias_ref, wsrc1, wsrc2, o_ref, wdst1, wdst2, *, rows):
    wdst1[...] = wsrc1[...].astype(BF16)
    wdst2[...] = wsrc2[...].astype(BF16)
    g = pl.program_id(1)
    wstart = _na_window_start(g, rows)
    scale = DH_B ** -0.5
    starts, variants = [], []
    for i in range(NA_GROUP):
        r = g * NA_GROUP + i
        rs = jnp.clip(r - WIN_ROWS // 2, 0, rows - WIN_ROWS)
        starts.append(pl.multiple_of((rs - wstart) * GRID_W, GRID_W))
        variants.append(rs - r + (WIN_ROWS - 1))

    def scores(i, h):
        hs = slice(h * DH_B, (h + 1) * DH_B)
        q = q_ref[i * GRID_W:(i + 1) * GRID_W, hs]
        kw = kbuf[pl.ds(starts[i], WIN_ROWS * GRID_W), hs]
        s = lax.dot_general(q, kw, (((1,), (1,)), ((), ())), preferred_element_type=F32)
        s = s * (scale * LOG2_E) + bias_ref[variants[i], h]
        p = jnp.exp2(s - jnp.max(s, axis=-1, keepdims=True))
        return p.astype(BF16), 1.0 / jnp.sum(p, axis=-1, keepdims=True)

    def weighted_sum(i, h, p, inv_l):
        hs = slice(h * DH_B, (h + 1) * DH_B)
        vw = vbuf[pl.ds(starts[i], WIN_ROWS * GRID_W), hs]
        o = jnp.dot(p, vw, preferred_element_type=F32)
        o_ref[i * GRID_W:(i + 1) * GRID_W, hs] = (o * inv_l).astype(BF16)

    tiles = [(i, h) for i in range(NA_GROUP) for h in range(NH_B)]
    pending = []
    for idx in range(len(tiles) + NA_LOOKAHEAD):
        if idx < len(tiles):
            pending.append(scores(*tiles[idx]))
        if idx >= NA_LOOKAHEAD:
            weighted_sum(*tiles[idx - NA_LOOKAHEAD], *pending[idx - NA_LOOKAHEAD])
            pending[idx - NA_LOOKAHEAD] = None


def _natten(p, bias, w_cast1, w_cast2, batch, seq):
    rows = seq // GRID_W
    ng = rows // NA_GROUP
    blk = NA_GROUP * GRID_W
    win = NA_WINDOW * GRID_W
    t = p.shape[1]
    qcol, kcol, vcol = 4, 5, 6

    def window(col):
        def imap(b, g):
            return (col, (b * rows + _na_window_start(g, rows)) * GRID_W, 0)
        return pl.BlockSpec((None, pl.Element(win), pl.Element(W_B)), imap)

    step = lambda b, g: (b * ng + g, 0)
    c1_in, c1_out, c1_shape, c1_blk = _cast_slab_specs(w_cast1, batch * ng, step)
    c2_in, c2_out, c2_shape, c2_blk = _cast_slab_specs(w_cast2, batch * ng, step)
    return pl.pallas_call(
        functools.partial(_natten_kernel, rows=rows),
        grid=(batch, ng),
        in_specs=[pl.BlockSpec((None, blk, W_B), lambda b, g: (qcol, b * ng + g, 0)),
                  window(kcol), window(vcol),
                  pl.BlockSpec(bias.shape, lambda b, g: (0, 0, 0, 0), pipeline_mode=pl.Buffered(1)),
                  c1_in, c2_in],
        out_specs=[pl.BlockSpec((blk, W_B), step), c1_out, c2_out],
        out_shape=[jax.ShapeDtypeStruct((t, W_B), BF16), c1_shape, c2_shape],
        compiler_params=_params(
            ("parallel", "arbitrary"),
            [((blk, W_B), BF16)] * 2 + [((win, W_B), BF16)] * 2 + [(bias.shape, F32)]
            + [c1_blk, c1_blk, c2_blk, c2_blk]),
        name="natten",
    )(p, p, p, bias, w_cast1, w_cast2)


def _natten_bias(rpb):
    cols = np.arange(GRID_W)
    cstart = np.clip(cols - WIN_COLS // 2, 0, GRID_W - WIN_COLS)
    inside = (cols[None, :] >= cstart[:, None]) & (cols[None, :] < cstart[:, None] + WIN_COLS)
    dc = cols[None, :] - cols[:, None] + (WIN_COLS - 1)
    onehot = (np.arange(2 * WIN_COLS - 1)[:, None, None] == dc[None]) & inside[None]
    toep = jnp.einsum('hdj,jck->hdck', rpb.astype(F32), onehot.astype(np.float32),
                      precision=lax.Precision.HIGHEST)
    toep = toep * LOG2_E + np.where(inside, 0.0, NEG_BIG).astype(np.float32)
    tiles = []
    for variant in range(WIN_ROWS):
        t = toep[:, variant:variant + WIN_ROWS].transpose(0, 2, 1, 3)
        tiles.append(t.reshape(NH_B, GRID_W, WIN_ROWS * GRID_W))
    return jnp.stack(tiles, axis=0)


def _out_kernel(hf_ref, hb_ref, oa_ref, hbt_ref, x_ref, mod_ref, gh_ref, gn_ref, w_ref,
                x1_ref, h2_ref):
    for c in range(TM_OUT // ROW_CHUNK):
        rows = slice(c * ROW_CHUNK, (c + 1) * ROW_CHUNK)
        hs = hf_ref[rows, :].astype(F32) + hb_ref[rows, :].astype(F32)
        parts = []
        for head in range(NH_A):
            hh = hs[:, head * DH_A:(head + 1) * DH_A]
            parts.append(hh * lax.rsqrt(jnp.mean(hh * hh, axis=-1, keepdims=True) + EPS))
        hn = jnp.concatenate(parts, axis=-1) * gh_ref[...]
        oa = oa_ref[rows, :].astype(F32)
        ha = (hn * (1.0 / (1.0 + jnp.exp(-oa)))).astype(BF16)
        mixed = jnp.dot(jnp.concatenate([ha, hbt_ref[rows, :]], axis=1), w_ref[...],
                        preferred_element_type=F32)
        x1 = x_ref[rows, :] + mod_ref[0, 2:3, :] * mixed
        x1_ref[rows, :] = x1
        h2_ref[rows, :] = _rms_mod(x1, gn_ref[...], mod_ref[0, 3:4, :], mod_ref[0, 4:5, :]).astype(BF16)


def _out_proj(hf, hb, p, hbt, x2, mod, gh, g2, w_o, seq):
    t = x2.shape[0]
    tiles_per_seq = seq // TM_OUT
    row = lambda i: (i, 0)
    resident = pl.Buffered(1)
    return pl.pallas_call(
        _out_kernel,
        grid=(t // TM_OUT,),
        in_specs=[pl.BlockSpec((TM_OUT, W_A), row), pl.BlockSpec((TM_OUT, W_A), row),
                  pl.BlockSpec((None, TM_OUT, W_A), lambda i: (3, i, 0)),
                  pl.BlockSpec((TM_OUT, W_B), row),
                  pl.BlockSpec((TM_OUT, D_MODEL), row),
                  pl.BlockSpec((1, N_MOD, D_MODEL), lambda i: (i // tiles_per_seq, 0, 0)),
                  pl.BlockSpec((1, W_A), lambda i: (0, 0)),
                  pl.BlockSpec((1, D_MODEL), lambda i: (0, 0)),
                  pl.BlockSpec((W_A + W_B, D_MODEL), lambda i: (0, 0), pipeline_mode=resident)],
        out_specs=[pl.BlockSpec((TM_OUT, D_MODEL), row), pl.BlockSpec((TM_OUT, D_MODEL), row)],
        out_shape=[jax.ShapeDtypeStruct((t, D_MODEL), F32), jax.ShapeDtypeStruct((t, D_MODEL), BF16)],
        compiler_params=_params(
            ("parallel",),
            [((TM_OUT, W_A), BF16)] * 4 + [((TM_OUT, D_MODEL), F32)] * 2
            + [((TM_OUT, D_MODEL), BF16)] + [((W_A, D_MODEL), BF16)]),
        name="out_proj",
    )(hf, hb, p, hbt, x2, mod, gh, g2, w_o)


def _up_kernel(hp_ref, h_ref, hn_ref, wu_ref, wg_ref, cw_ref, cb_ref, act_ref, hext_ref, *,
               tiles_per_seq):
    pos = pl.program_id(1) % tiles_per_seq
    n = TM_UP + 2 * HALO
    hext_ref[0:HALO, :] = jnp.where(pos > 0, hp_ref[...], jnp.zeros_like(hp_ref))
    hext_ref[HALO:HALO + TM_UP, :] = h_ref[...]
    hext_ref[HALO + TM_UP:, :] = jnp.where(pos < tiles_per_seq - 1, hn_ref[...], jnp.zeros_like(hn_ref))
    for c in range(TN_UP // COL_CHUNK):
        cols = slice(c * COL_CHUNK, (c + 1) * COL_CHUNK)
        u = jnp.dot(h_ref[...], wu_ref[:, cols], preferred_element_type=F32)
        g = jnp.dot(hext_ref[...], wg_ref[:, cols], preferred_element_type=F32)
        g_prev = pltpu.roll(g, 1, axis=0)[HALO:HALO + TM_UP, :]
        g_next = pltpu.roll(g, n - 1, axis=0)[HALO:HALO + TM_UP, :]
        gc = (g_prev * cw_ref[0:1, cols] + g[HALO:HALO + TM_UP, :] * cw_ref[1:2, cols]
              + g_next * cw_ref[2:3, cols] + cb_ref[:, cols])
        gelu = 0.5 * gc * (1.0 + lax.erf(gc * (2.0 ** -0.5)))
        act_ref[:, cols] = (gelu * u).astype(BF16)


def _up_proj(h2, w_up, conv_w, conv_b, seq):
    t = h2.shape[0]
    tiles_per_seq = seq // TM_UP
    hb = TM_UP // HALO
    nh = t // HALO
    ngroups = D_FF // TN_UP
    resident = pl.Buffered(1)
    return pl.pallas_call(
        functools.partial(_up_kernel, tiles_per_seq=tiles_per_seq),
        grid=(ngroups, t // TM_UP),
        in_specs=[pl.BlockSpec((HALO, D_MODEL), lambda j, i: (jnp.maximum(i * hb - 1, 0), 0)),
                  pl.BlockSpec((TM_UP, D_MODEL), lambda j, i: (i, 0)),
                  pl.BlockSpec((HALO, D_MODEL), lambda j, i: (jnp.minimum((i + 1) * hb, nh - 1), 0)),
                  pl.BlockSpec((D_MODEL, TN_UP), lambda j, i: (0, j), pipeline_mode=resident),
                  pl.BlockSpec((D_MODEL, TN_UP), lambda j, i: (0, ngroups + j), pipeline_mode=resident),
                  pl.BlockSpec((8, TN_UP), lambda j, i: (0, j)),
                  pl.BlockSpec((1, TN_UP), lambda j, i: (0, j))],
        out_specs=pl.BlockSpec((TM_UP, TN_UP), lambda j, i: (i, j)),
        out_shape=jax.ShapeDtypeStruct((t, D_FF), BF16),
        scratch_shapes=[pltpu.VMEM((TM_UP + 2 * HALO, D_MODEL), BF16)],
        compiler_params=_params(
            ("arbitrary", "arbitrary"),
            [((TM_UP + 2 * HALO, D_MODEL), BF16), ((D_MODEL, TN_UP), BF16), ((TM_UP, TN_UP), BF16)],
            [((TM_UP + 2 * HALO, D_MODEL), BF16)]),
        name="up_proj",
    )(h2, h2, h2, w_up, w_up, conv_w, conv_b)


def _down_kernel(a_ref, w_ref, x_ref, mod_ref, g_ref, o_ref):
    for c in range(TM_DOWN // ROW_CHUNK):
        rows = slice(c * ROW_CHUNK, (c + 1) * ROW_CHUNK)
        y = x_ref[rows, :] + mod_ref[0, 5:6, :] * jnp.dot(a_ref[rows, :], w_ref[...],
                                                           preferred_element_type=F32)
        o_ref[rows, :] = y * lax.rsqrt(jnp.mean(y * y, axis=-1, keepdims=True) + EPS) * g_ref[...]


def _down_proj(act, w_d, x1, mod, gfin, seq):
    t = x1.shape[0]
    tiles_per_seq = seq // TM_DOWN
    return pl.pallas_call(
        _down_kernel,
        grid=(t // TM_DOWN,),
        in_specs=[pl.BlockSpec((TM_DOWN, D_FF), lambda i: (i, 0)),
                  pl.BlockSpec((D_FF, D_MODEL), lambda i: (0, 0), pipeline_mode=pl.Buffered(1)),
                  pl.BlockSpec((TM_DOWN, D_MODEL), lambda i: (i, 0)),
                  pl.BlockSpec((1, N_MOD, D_MODEL), lambda i: (i // tiles_per_seq, 0, 0)),
                  pl.BlockSpec((1, D_MODEL), lambda i: (0, 0))],
        out_specs=pl.BlockSpec((TM_DOWN, D_MODEL), lambda i: (i, 0)),
        out_shape=jax.ShapeDtypeStruct((t, D_MODEL), F32),
        compiler_params=pltpu.CompilerParams(
            dimension_semantics=("parallel",),
            vmem_limit_bytes=_nbytes((D_FF, D_MODEL), BF16) + 2 * (
                _nbytes((TM_DOWN, D_FF), BF16) + 2 * _nbytes((TM_DOWN, D_MODEL), F32))
            + DOWN_TEMP_ALLOWANCE),
        name="down_proj",
    )(act, w_d, x1, mod, gfin)


def kernel(x, c, w_ada, b_ada, g_norm1, w_in, b_gates, g_head_a, rpb, w_out, g_norm2, w_up, conv_w,
           conv_b, w_down, g_final):
    batch, seq, d = x.shape
    assert w_ada.shape[0] == 1
    assert d == D_MODEL and seq % CHUNK_A == 0 and seq % (NA_GROUP * GRID_W) == 0
    x2 = x.reshape(batch * seq, d)
    mod = _ada(c, w_ada, b_ada)[:batch].reshape(batch, N_MOD, d)

    w_in_t = jnp.swapaxes(w_in, 1, 2)
    w_main = _win_prep(w_in_t)
    lane_of = np.concatenate([np.arange(0, 4), np.arange(128, 132), np.arange(4, 8), np.arange(132, 136)])
    sel = np.zeros((4 * NH_A, 256), np.float32)
    sel[np.arange(4 * NH_A), lane_of] = 1.0
    w_g = jnp.dot(sel.T, w_in_t[0, GATE_LO:GATE_HI], precision=lax.Precision.HIGHEST).astype(BF16)
    b_g = jnp.dot(b_gates, sel, precision=lax.Precision.HIGHEST)

    p, gates = _in_proj(x2, mod, g_norm1, w_main, w_g, b_g, seq)
    hf, hb, w_up_b = _mlstm(p, gates, w_up, batch, seq)
    hbt, w_down_b, w_out_b = _natten(p, _natten_bias(rpb[0]), w_down, w_out, batch, seq)
    x1, h2 = _out_proj(hf, hb, p, hbt, x2, mod, g_head_a, g_norm2, w_out_b, seq)

    cw8 = jnp.zeros((8, D_FF), F32).at[:3].set(conv_w[0])
    act = _up_proj(h2, w_up_b, cw8, conv_b, seq)
    out = _down_proj(act, w_down_b, x1, mod, g_final[None, :], seq)
    return out.reshape(batch, seq, d)
```

```python
import functools
import math

import jax
import jax.numpy as jnp
import numpy as np
from jax import lax
from jax.experimental import pallas as pl
from jax.experimental.pallas import tpu as pltpu

F32 = jnp.float32
BF16 = jnp.bfloat16

D_MODEL = 2048
GRID_W = 64
NH_A = 4
DH_A = 256
W_A = NH_A * DH_A
NH_B = 8
DH_B = 128
W_B = NH_B * DH_B
WIN_ROWS = 8
WIN_COLS = 16
D_FF = 5632
N_MOD = 6
EPS = 1e-6
NEG_BIG = -1e30
LOG2_E = 1.4426950408889634

CHUNK_A = 256

TM_IN = 1024
TN_IN = 1024
TM_OUT = 512
ROW_CHUNK = 256
TM_UP = 1024
TN_UP = D_FF // 2
COL_CHUNK = 256
HALO = 16
TM_DOWN = 512
NA_GROUP = 8
NA_WINDOW = NA_GROUP + WIN_ROWS
NA_KEYS = WIN_ROWS * GRID_W
NA_LOOKAHEAD = 4

V7X_VMEM_BYTES = 64 * 1024 * 1024
VMEM_TEMP_ALLOWANCE = 12 * 1024 * 1024
DOWN_TEMP_ALLOWANCE = 6 * 1024 * 1024


def _vmem_limit(block_bytes, scratch_bytes=0):
    est = 2 * block_bytes + scratch_bytes + VMEM_TEMP_ALLOWANCE
    return int(min(est, V7X_VMEM_BYTES - 4 * 1024 * 1024))


def _nbytes(shape, dtype):
    n = 1
    for s in shape:
        n *= s
    return n * jnp.dtype(dtype).itemsize


def _params(sem, blocks, scratch=()):
    bb = sum(_nbytes(s, d) for s, d in blocks)
    sb = sum(_nbytes(s, d) for s, d in scratch)
    return pltpu.CompilerParams(dimension_semantics=sem, vmem_limit_bytes=_vmem_limit(bb, sb))


def _rms_mod(x, g, shift, scale):
    return x * lax.rsqrt(jnp.mean(x * x, axis=-1, keepdims=True) + EPS) * (g * (1.0 + scale)) + shift


TN_ADA = 1024


def _ada_kernel(ct_ref, w_ref, b_ref, o_ref, s_ref, *, batch):
    @pl.when(pl.program_id(0) == 0)
    def _():
        ct = ct_ref[...]
        s = ct * (1.0 / (1.0 + jnp.exp(-ct)))
        for b in range(batch):
            s_ref[b] = jnp.broadcast_to(s[:, b:b + 1], (D_MODEL, 128))

    row = lax.broadcasted_iota(jnp.int32, (8, 128), 0)
    for l in range(TN_ADA // 128):
        lanes = slice(l * 128, (l + 1) * 128)
        wl = w_ref[:, lanes]
        tile = jnp.zeros((8, 128), F32)
        for b in range(batch):
            tile = jnp.where(row == b, jnp.sum(wl * s_ref[b], axis=0, keepdims=True), tile)
        o_ref[:, lanes] = tile + b_ref[:, lanes]


def _ada(c, w_ada, b_ada):
    batch = c.shape[0]
    n = w_ada.shape[2]
    ct = jnp.zeros((D_MODEL, 8), F32).at[:, :batch].set(c.T)
    return pl.pallas_call(
        functools.partial(_ada_kernel, batch=batch),
        grid=(n // TN_ADA,),
        in_specs=[pl.BlockSpec((D_MODEL, 8), lambda j: (0, 0)),
                  pl.BlockSpec((None, D_MODEL, TN_ADA), lambda j: (0, 0, j)),
                  pl.BlockSpec((1, TN_ADA), lambda j: (0, j))],
        out_specs=pl.BlockSpec((8, TN_ADA), lambda j: (0, j)),
        out_shape=jax.ShapeDtypeStruct((8, n), F32),
        scratch_shapes=[pltpu.VMEM((batch, D_MODEL, 128), F32)],
        compiler_params=_params(("arbitrary",), [((D_MODEL, TN_ADA), F32), ((D_MODEL, 128), F32)],
                                [((batch, D_MODEL, 128), F32)]),
        name="ada",
    )(ct, w_ada, b_ada)


GATE_LO = 4 * W_A
GATE_HI = GATE_LO + 4 * NH_A
TM_PREP = 512
NBLK_A = 4
_NT = (((1,), (1,)), ((), ()))


def _win_prep_kernel(w_ref, o_ref):
    o_ref[...] = w_ref[...].astype(BF16)


def _win_prep(w_t):
    _, cols, d = w_t.shape
    ng = GATE_HI - GATE_LO
    n = cols - ng
    assert GATE_LO % TM_PREP == 0 and n % TM_PREP == 0 and TM_PREP % ng == 0

    def src_row(i):
        past = (i >= GATE_LO // TM_PREP).astype(jnp.int32)
        return (0, (i * (TM_PREP // ng) + past) * ng, 0)

    return pl.pallas_call(
        _win_prep_kernel,
        grid=(n // TM_PREP,),
        in_specs=[pl.BlockSpec((None, pl.Element(TM_PREP), pl.Element(d)), src_row)],
        out_specs=pl.BlockSpec((TM_PREP, d), lambda i: (i, 0)),
        out_shape=jax.ShapeDtypeStruct((n, d), BF16),
        compiler_params=_params(("parallel",), [((TM_PREP, d), F32), ((TM_PREP, d), BF16)]),
        name="w_in_prep",
    )(w_t)


def _in_kernel(x_ref, mod_ref, g_ref, w_ref, wg_ref, bg_ref, pa_ref, pb_ref, gates_ref, h_ref):
    j = pl.program_id(1)

    @pl.when(j == 0)
    def _():
        for c in range(TM_IN // ROW_CHUNK):
            rows = slice(c * ROW_CHUNK, (c + 1) * ROW_CHUNK)
            h = _rms_mod(x_ref[rows, :], g_ref[...], mod_ref[0, 0:1, :], mod_ref[0, 1:2, :])
            hb = h.astype(BF16)
            h_ref[rows, :] = hb
            pa_ref[rows, :] = lax.dot_general(hb, w_ref[...], _NT,
                                              preferred_element_type=F32).astype(BF16)
            gates_ref[rows, :] = lax.dot_general(hb, wg_ref[...], _NT,
                                                 preferred_element_type=F32) + bg_ref[...]

    @pl.when(jnp.logical_and(j > 0, j < NBLK_A))
    def _():
        pa_ref[...] = lax.dot_general(h_ref[...], w_ref[...], _NT,
                                      preferred_element_type=F32).astype(BF16)

    @pl.when(j >= NBLK_A)
    def _():
        r = lax.dot_general(h_ref[...], w_ref[...], _NT, preferred_element_type=F32)
        for head in range(NH_B):
            pb_ref[head] = r[:, head * DH_B:(head + 1) * DH_B].astype(BF16)


def _in_proj(x2, mod, g1, w_main, w_g, b_g, seq):
    t = x2.shape[0]
    n = w_main.shape[0]
    assert TN_IN == W_A == W_B and n == (NBLK_A + 3) * TN_IN
    tiles_per_seq = seq // TM_IN
    return pl.pallas_call(
        _in_kernel,
        grid=(t // TM_IN, n // TN_IN),
        in_specs=[pl.BlockSpec((TM_IN, D_MODEL), lambda i, j: (i, 0)),
                  pl.BlockSpec((1, N_MOD, D_MODEL), lambda i, j: (i // tiles_per_seq, 0, 0)),
                  pl.BlockSpec((1, D_MODEL), lambda i, j: (0, 0)),
                  pl.BlockSpec((TN_IN, D_MODEL), lambda i, j: (j, 0)),
                  pl.BlockSpec((256, D_MODEL), lambda i, j: (0, 0)),
                  pl.BlockSpec((1, 256), lambda i, j: (0, 0))],
        out_specs=[pl.BlockSpec((None, TM_IN, TN_IN), lambda i, j: (jnp.minimum(j, NBLK_A - 1), i, 0)),
                   pl.BlockSpec((NH_B, TM_IN, DH_B), lambda i, j: (jnp.maximum(j - NBLK_A, 0), i, 0)),
                   pl.BlockSpec((TM_IN, 256), lambda i, j: (i, 0))],
        out_shape=[jax.ShapeDtypeStruct((NBLK_A, t, TN_IN), BF16),
                   jax.ShapeDtypeStruct((3 * NH_B, t, DH_B), BF16),
                   jax.ShapeDtypeStruct((t, 256), F32)],
        scratch_shapes=[pltpu.VMEM((TM_IN, D_MODEL), BF16)],
        compiler_params=_params(
            ("parallel", "arbitrary"),
            [((TM_IN, D_MODEL), F32), ((D_MODEL, TN_IN), BF16), ((D_MODEL, 256), BF16),
             ((TM_IN, TN_IN), BF16), ((TM_IN, TN_IN), BF16), ((TM_IN, 256), F32)],
            [((TM_IN, D_MODEL), BF16)]),
        name="in_proj",
    )(x2, mod, g1, w_main, w_g, b_g)


def _seg_scan(x, row, op, fill, reverse):
    n = x.shape[0]
    d = 1
    while d < n:
        if reverse:
            y = pltpu.roll(x, n - d, axis=0)
            x = op(x, jnp.where(row < n - d, y, fill))
        else:
            y = pltpu.roll(x, d, axis=0)
            x = op(x, jnp.where(row >= d, y, fill))
        d *= 2
    return x


def _gate_vectors(g_ref, m_ref, reverse):
    n = CHUNK_A
    gi = g_ref[:, 0:128]
    gf = g_ref[:, 128:256]
    lf = jnp.minimum(gf, 0.0) - jnp.log(1.0 + jnp.exp(-jnp.abs(gf)))
    row = lax.broadcasted_iota(jnp.int32, (n, 128), 0)
    bc = _seg_scan(lf, row, jnp.add, 0.0, reverse)
    rb = gi - bc
    cm = _seg_scan(rb, row, jnp.maximum, -jnp.inf, reverse)
    last = 0 if reverse else n - 1
    gsum = bc[last:last + 1, :]
    m_loc = gsum + cm[last:last + 1, :]
    m = m_ref[...]
    m_inter = bc + m
    m_t = jnp.maximum(m_inter, bc + cm)
    m_new = jnp.maximum(gsum + m, m_loc)
    m_ref[...] = m_new
    col_a = bc - m_t - math.log(DH_A ** 0.5)
    a = jnp.exp(m_inter - m_t)
    e = jnp.exp(-m_t)
    w = jnp.exp(gsum + rb - m_new) * (DH_A ** -0.5)
    dec = jnp.exp(gsum + m - m_new)
    return col_a, a, e, w, dec, rb.T


def _cast_slab_specs(w, steps, imap):
    _, rows, cols = w.shape
    slab = rows // steps
    assert slab * steps == rows and slab % 16 == 0
    in_spec = pl.BlockSpec((None, slab, cols), lambda *ids: (0,) + tuple(imap(*ids)))
    out_spec = pl.BlockSpec((slab, cols), imap)
    return in_spec, out_spec, jax.ShapeDtypeStruct((rows, cols), BF16), ((slab, cols), F32)


def _mlstm_kernel(qf, kf, vf, qb, kb, vb, gf, gb, wsrc, hf, hb, wdst, ct_ref, mf_ref, mb_ref):
    @pl.when(pl.program_id(1) == 0)
    def _():
        ct_ref[...] = jnp.zeros(ct_ref.shape, F32)
        mf_ref[...] = jnp.full((1, 128), NEG_BIG, F32)
        mb_ref[...] = jnp.full((1, 128), NEG_BIG, F32)

    wdst[...] = wsrc[...].astype(BF16)

    n = CHUNK_A
    r = lax.broadcasted_iota(jnp.int32, (n, n), 0)
    c = lax.broadcasted_iota(jnp.int32, (n, n), 1)
    ones_cols = jnp.ones((n, 128), BF16)
    ones_rows = jnp.ones((128, n), BF16)
    seqs = []
    for head in range(NH_A):
        seqs.append((qf, kf, vf, False, hf, head, head, r >= c))
        seqs.append((qb, kb, vb, True, hb, head, NH_A + head, r <= c))

    loaded = []
    for q_ref, k_ref, v_ref, _, _, head, sid, _ in seqs:
        sl = slice(head * DH_A, (head + 1) * DH_A)
        q = q_ref[:, sl]
        k = k_ref[:, sl]
        v = v_ref[:, sl]
        ct_old = ct_ref[sid]
        qk = lax.dot_general(q, k, _NT, preferred_element_type=F32)
        qc = lax.dot_general(q, ct_old.astype(BF16), _NT, preferred_element_type=F32)
        loaded.append((k, v, ct_old, qk, qc))

    gate_vecs = {False: _gate_vectors(gf, mf_ref, reverse=False),
                 True: _gate_vectors(gb, mb_ref, reverse=True)}
    for (_, _, _, rev, h_ref, head, sid, mask), (k, v, ct_old, qk, qc) in zip(seqs, loaded):
        gates = gate_vecs[rev]
        sl = slice(head * DH_A, (head + 1) * DH_A)
        col_a, a, e = (g[:, sid:sid + 1] for g in gates[:3])
        row_b = gates[5][sid:sid + 1, :]
        s = qk * jnp.exp(jnp.where(mask, col_a + row_b, -jnp.inf))
        v_ext = jnp.concatenate([v, ones_cols], axis=1)
        sv = jnp.dot(s.astype(BF16), v_ext, preferred_element_type=F32) + a * qc
        inv = 1.0 / jnp.maximum(jnp.abs(sv[:, DH_A:]), e)
        h_ref[:, sl] = (sv[:, :DH_A] * jnp.concatenate([inv, inv], axis=1)).astype(BF16)

    for (_, _, _, rev, _, _, sid, _), (k, v, ct_old, _, _) in zip(seqs, loaded):
        gates = gate_vecs[rev]
        w = gates[3][:, sid:sid + 1]
        dec = gates[4][:, sid:sid + 1]
        kw = (w * k.astype(F32)).astype(BF16)
        vt_ext = jnp.concatenate([v.T, ones_rows], axis=0)
        ct_ref[sid] = dec * ct_old + jnp.dot(vt_ext, kw, preferred_element_type=F32)


def _mlstm(p, gates, w_cast, batch, seq):
    nc = seq // CHUNK_A
    t = p.shape[1]
    cast_in, cast_out, cast_shape, cast_blk = _cast_slab_specs(w_cast, batch * nc,
                                                               lambda b, k: (b * nc + k, 0))

    def blk(col, rev):
        if rev:
            return pl.BlockSpec((None, CHUNK_A, W_A), lambda b, k: (col, b * nc + nc - 1 - k, 0))
        return pl.BlockSpec((None, CHUNK_A, W_A), lambda b, k: (col, b * nc + k, 0))

    def outblk(rev):
        if rev:
            return pl.BlockSpec((CHUNK_A, W_A), lambda b, k: (b * nc + nc - 1 - k, 0))
        return pl.BlockSpec((CHUNK_A, W_A), lambda b, k: (b * nc + k, 0))

    gatespec = lambda rev: pl.BlockSpec(
        (CHUNK_A, 256), (lambda b, k: (b * nc + nc - 1 - k, 0)) if rev else (lambda b, k: (b * nc + k, 0)))
    return pl.pallas_call(
        _mlstm_kernel,
        grid=(batch, nc),
        in_specs=[blk(0, False), blk(1, False), blk(2, False),
                  blk(0, True), blk(1, True), blk(2, True),
                  gatespec(False), gatespec(True), cast_in],
        out_specs=[outblk(False), outblk(True), cast_out],
        out_shape=[jax.ShapeDtypeStruct((t, W_A), BF16), jax.ShapeDtypeStruct((t, W_A), BF16),
                   cast_shape],
        scratch_shapes=[pltpu.VMEM((2 * NH_A, DH_A + 128, DH_A), F32),
                        pltpu.VMEM((1, 128), F32), pltpu.VMEM((1, 128), F32)],
        compiler_params=_params(
            ("parallel", "arbitrary"),
            [((CHUNK_A, W_A), BF16)] * 8 + [((CHUNK_A, 256), F32)] * 2 + [cast_blk, cast_blk],
            [((2 * NH_A, DH_A + 128, DH_A), F32)]),
        name="mlstm",
    )(p, p, p, p, p, p, gates, gates, w_cast)


def _na_window_start(g, rows):
    return jnp.clip(g * NA_GROUP - WIN_ROWS // 2, 0, rows - NA_WINDOW)


def _natten_kernel(q_ref, kbuf, vbuf, bias_ref, wsrc1, wsrc2, o_ref, wdst1, wdst2, *, rows):
    wdst1[...] = wsrc1[...].astype(BF16)
    wdst2[...] = wsrc2[...].astype(BF16)
    g = pl.program_id(1)
    wstart = _na_window_start(g, rows)
    scale = DH_B ** -0.5
    starts, variants = [], []
    for i in range(NA_GROUP):
        r = g * NA_GROUP + i
        rs = jnp.clip(r - WIN_ROWS // 2, 0, rows - WIN_ROWS)
        starts.append(pl.multiple_of((rs - wstart) * GRID_W, GRID_W))
        variants.append(rs - r + (WIN_ROWS - 1))

    def scores(i, h):
        q = q_ref[h, i * GRID_W:(i + 1) * GRID_W, :]
        kw = kbuf[h, pl.ds(starts[i], WIN_ROWS * GRID_W), :]
        s = lax.dot_general(q, kw, _NT, preferred_element_type=F32)
        bias = jnp.concatenate([bias_ref[variants[i], h, l] for l in range(NA_KEYS // 128)], axis=1)
        s = s * (scale * LOG2_E) + bias
        p = jnp.exp2(s - jnp.max(s, axis=-1, keepdims=True))
        return p.astype(BF16), 1.0 / jnp.sum(p, axis=-1, keepdims=True)

    def weighted_sum(i, h, p, inv_l):
        hs = slice(h * DH_B, (h + 1) * DH_B)
        vw = vbuf[h, pl.ds(starts[i], WIN_ROWS * GRID_W), :]
        o = jnp.dot(p, vw, preferred_element_type=F32)
        o_ref[i * GRID_W:(i + 1) * GRID_W, hs] = (o * inv_l).astype(BF16)

    tiles = [(i, h) for i in range(NA_GROUP) for h in range(NH_B)]
    pending = []
    for idx in range(len(tiles) + NA_LOOKAHEAD):
        if idx < len(tiles):
            pending.append(scores(*tiles[idx]))
        if idx >= NA_LOOKAHEAD:
            weighted_sum(*tiles[idx - NA_LOOKAHEAD], *pending[idx - NA_LOOKAHEAD])
            pending[idx - NA_LOOKAHEAD] = None


def _natten(p, bias, w_cast1, w_cast2, batch, seq):
    rows = seq // GRID_W
    ng = rows // NA_GROUP
    blk = NA_GROUP * GRID_W
    win = NA_WINDOW * GRID_W
    t = p.shape[1]
    qpart, kpart, vpart = 0, 1, 2

    def window(part):
        def imap(b, g):
            return (part * NH_B, (b * rows + _na_window_start(g, rows)) * GRID_W, 0)
        return pl.BlockSpec((pl.Element(NH_B), pl.Element(win), pl.Element(DH_B)), imap)

    step = lambda b, g: (b * ng + g, 0)
    c1_in, c1_out, c1_shape, c1_blk = _cast_slab_specs(w_cast1, batch * ng, step)
    c2_in, c2_out, c2_shape, c2_blk = _cast_slab_specs(w_cast2, batch * ng, step)
    return pl.pallas_call(
        functools.partial(_natten_kernel, rows=rows),
        grid=(batch, ng),
        in_specs=[pl.BlockSpec((NH_B, blk, DH_B), lambda b, g: (qpart, b * ng + g, 0)),
                  window(kpart), window(vpart),
                  pl.BlockSpec(bias.shape, lambda b, g: (0, 0, 0, 0, 0), pipeline_mode=pl.Buffered(1)),
                  c1_in, c2_in],
        out_specs=[pl.BlockSpec((blk, W_B), step), c1_out, c2_out],
        out_shape=[jax.ShapeDtypeStruct((t, W_B), BF16), c1_shape, c2_shape],
        compiler_params=_params(
            ("parallel", "arbitrary"),
            [((blk, W_B), BF16)] * 2 + [((win, W_B), BF16)] * 2 + [(bias.shape, F32)]
            + [c1_blk, c1_blk, c2_blk, c2_blk]),
        name="natten",
    )(p, p, p, bias, w_cast1, w_cast2)


def _natten_bias(rpb):
    cols = np.arange(GRID_W)
    cstart = np.clip(cols - WIN_COLS // 2, 0, GRID_W - WIN_COLS)
    inside = (cols[None, :] >= cstart[:, None]) & (cols[None, :] < cstart[:, None] + WIN_COLS)
    dc = cols[None, :] - cols[:, None] + (WIN_COLS - 1)
    onehot = (np.arange(2 * WIN_COLS - 1)[:, None, None] == dc[None]) & inside[None]
    toep = jnp.einsum('hdj,jck->hdck', rpb.astype(F32), onehot.astype(np.float32),
                      precision=lax.Precision.HIGHEST)
    pick = (np.arange(WIN_ROWS)[:, None, None] + np.arange(WIN_ROWS)[None, :, None]
            == np.arange(2 * WIN_ROWS - 1)[None, None, :])
    tiles = jnp.einsum('hdck,vrd->vhcrk', toep * LOG2_E, pick.astype(np.float32),
                       precision=lax.Precision.HIGHEST)
    tiles = tiles + np.where(inside, 0.0, NEG_BIG).astype(np.float32)[:, None, :]
    tiles = tiles.reshape(WIN_ROWS, NH_B, GRID_W, NA_KEYS // 128, 128)
    return tiles.transpose(0, 1, 3, 2, 4)


def _out_kernel(hf_ref, hb_ref, oa_ref, hbt_ref, x_ref, mod_ref, gh_ref, gn_ref, w_ref,
                x1_ref, h2_ref):
    for c in range(TM_OUT // ROW_CHUNK):
        rows = slice(c * ROW_CHUNK, (c + 1) * ROW_CHUNK)
        hs = hf_ref[rows, :].astype(F32) + hb_ref[rows, :].astype(F32)
        parts = []
        for head in range(NH_A):
            hh = hs[:, head * DH_A:(head + 1) * DH_A]
            parts.append(hh * lax.rsqrt(jnp.mean(hh * hh, axis=-1, keepdims=True) + EPS))
        hn = jnp.concatenate(parts, axis=-1) * gh_ref[...]
        oa = oa_ref[rows, :].astype(F32)
        ha = (hn * (1.0 / (1.0 + jnp.exp(-oa)))).astype(BF16)
        mixed = jnp.dot(jnp.concatenate([ha, hbt_ref[rows, :]], axis=1), w_ref[...],
                        preferred_element_type=F32)
        x1 = x_ref[rows, :] + mod_ref[0, 2:3, :] * mixed
        x1_ref[rows, :] = x1
        h2_ref[rows, :] = _rms_mod(x1, gn_ref[...], mod_ref[0, 3:4, :], mod_ref[0, 4:5, :]).astype(BF16)


def _out_proj(hf, hb, pa, hbt, x2, mod, gh, g2, w_o, seq):
    t = x2.shape[0]
    tiles_per_seq = seq // TM_OUT
    row = lambda i: (i, 0)
    resident = pl.Buffered(1)
    return pl.pallas_call(
        _out_kernel,
        grid=(t // TM_OUT,),
        in_specs=[pl.BlockSpec((TM_OUT, W_A), row), pl.BlockSpec((TM_OUT, W_A), row),
                  pl.BlockSpec((None, TM_OUT, W_A), lambda i: (3, i, 0)),
                  pl.BlockSpec((TM_OUT, W_B), row),
                  pl.BlockSpec((TM_OUT, D_MODEL), row),
                  pl.BlockSpec((1, N_MOD, D_MODEL), lambda i: (i // tiles_per_seq, 0, 0)),
                  pl.BlockSpec((1, W_A), lambda i: (0, 0)),
                  pl.BlockSpec((1, D_MODEL), lambda i: (0, 0)),
                  pl.BlockSpec((W_A + W_B, D_MODEL), lambda i: (0, 0), pipeline_mode=resident)],
        out_specs=[pl.BlockSpec((TM_OUT, D_MODEL), row), pl.BlockSpec((TM_OUT, D_MODEL), row)],
        out_shape=[jax.ShapeDtypeStruct((t, D_MODEL), F32), jax.ShapeDtypeStruct((t, D_MODEL), BF16)],
        compiler_params=_params(
            ("parallel",),
            [((TM_OUT, W_A), BF16)] * 4 + [((TM_OUT, D_MODEL), F32)] * 2
            + [((TM_OUT, D_MODEL), BF16)] + [((W_A, D_MODEL), BF16)]),
        name="out_proj",
    )(hf, hb, pa, hbt, x2, mod, gh, g2, w_o)


def _up_kernel(hp_ref, h_ref, hn_ref, wu_ref, wg_ref, cw_ref, cb_ref, act_ref, hext_ref, *,
               tiles_per_seq):
    pos = pl.program_id(1) % tiles_per_seq
    n = TM_UP + 2 * HALO
    hext_ref[0:HALO, :] = jnp.where(pos > 0, hp_ref[...], jnp.zeros_like(hp_ref))
    hext_ref[HALO:HALO + TM_UP, :] = h_ref[...]
    hext_ref[HALO + TM_UP:, :] = jnp.where(pos < tiles_per_seq - 1, hn_ref[...], jnp.zeros_like(hn_ref))
    for c in range(TN_UP // COL_CHUNK):
        cols = slice(c * COL_CHUNK, (c + 1) * COL_CHUNK)
        u = jnp.dot(h_ref[...], wu_ref[:, cols], preferred_element_type=F32)
        g = jnp.dot(hext_ref[...], wg_ref[:, cols], preferred_element_type=F32)
        g_prev = pltpu.roll(g, 1, axis=0)[HALO:HALO + TM_UP, :]
        g_next = pltpu.roll(g, n - 1, axis=0)[HALO:HALO + TM_UP, :]
        gc = (g_prev * cw_ref[0:1, cols] + g[HALO:HALO + TM_UP, :] * cw_ref[1:2, cols]
              + g_next * cw_ref[2:3, cols] + cb_ref[:, cols])
        gelu = 0.5 * gc * (1.0 + lax.erf(gc * (2.0 ** -0.5)))
        act_ref[:, cols] = (gelu * u).astype(BF16)


def _up_proj(h2, w_up, conv_w, conv_b, seq):
    t = h2.shape[0]
    tiles_per_seq = seq // TM_UP
    hb = TM_UP // HALO
    nh = t // HALO
    ngroups = D_FF // TN_UP
    resident = pl.Buffered(1)
    return pl.pallas_call(
        functools.partial(_up_kernel, tiles_per_seq=tiles_per_seq),
        grid=(ngroups, t // TM_UP),
        in_specs=[pl.BlockSpec((HALO, D_MODEL), lambda j, i: (jnp.maximum(i * hb - 1, 0), 0)),
                  pl.BlockSpec((TM_UP, D_MODEL), lambda j, i: (i, 0)),
                  pl.BlockSpec((HALO, D_MODEL), lambda j, i: (jnp.minimum((i + 1) * hb, nh - 1), 0)),
                  pl.BlockSpec((D_MODEL, TN_UP), lambda j, i: (0, j), pipeline_mode=resident),
                  pl.BlockSpec((D_MODEL, TN_UP), lambda j, i: (0, ngroups + j), pipeline_mode=resident),
                  pl.BlockSpec((8, TN_UP), lambda j, i: (0, j)),
                  pl.BlockSpec((1, TN_UP), lambda j, i: (0, j))],
        out_specs=pl.BlockSpec((TM_UP, TN_UP), lambda j, i: (i, j)),
        out_shape=jax.ShapeDtypeStruct((t, D_FF), BF16),
        scratch_shapes=[pltpu.VMEM((TM_UP + 2 * HALO, D_MODEL), BF16)],
        compiler_params=_params(
            ("arbitrary", "arbitrary"),
            [((TM_UP + 2 * HALO, D_MODEL), BF16), ((D_MODEL, TN_UP), BF16), ((TM_UP, TN_UP), BF16)],
            [((TM_UP + 2 * HALO, D_MODEL), BF16)]),
        name="up_proj",
    )(h2, h2, h2, w_up, w_up, conv_w, conv_b)


def _down_kernel(a_ref, w_ref, x_ref, mod_ref, g_ref, o_ref):
    for c in range(TM_DOWN // ROW_CHUNK):
        rows = slice(c * ROW_CHUNK, (c + 1) * ROW_CHUNK)
        y = x_ref[rows, :] + mod_ref[0, 5:6, :] * jnp.dot(a_ref[rows, :], w_ref[...],
                                                           preferred_element_type=F32)
        o_ref[rows, :] = y * lax.rsqrt(jnp.mean(y * y, axis=-1, keepdims=True) + EPS) * g_ref[...]


def _down_proj(act, w_d, x1, mod, gfin, seq):
    t = x1.shape[0]
    tiles_per_seq = seq // TM_DOWN
    return pl.pallas_call(
        _down_kernel,
        grid=(t // TM_DOWN,),
        in_specs=[pl.BlockSpec((TM_DOWN, D_FF), lambda i: (i, 0)),
                  pl.BlockSpec((D_FF, D_MODEL), lambda i: (0, 0), pipeline_mode=pl.Buffered(1)),
                  pl.BlockSpec((TM_DOWN, D_MODEL), lambda i: (i, 0)),
                  pl.BlockSpec((1, N_MOD, D_MODEL), lambda i: (i // tiles_per_seq, 0, 0)),
                  pl.BlockSpec((1, D_MODEL), lambda i: (0, 0))],
        out_specs=pl.BlockSpec((TM_DOWN, D_MODEL), lambda i: (i, 0)),
        out_shape=jax.ShapeDtypeStruct((t, D_MODEL), F32),
        compiler_params=pltpu.CompilerParams(
            dimension_semantics=("parallel",),
            vmem_limit_bytes=_nbytes((D_FF, D_MODEL), BF16) + 2 * (
                _nbytes((TM_DOWN, D_FF), BF16) + 2 * _nbytes((TM_DOWN, D_MODEL), F32))
            + DOWN_TEMP_ALLOWANCE),
        name="down_proj",
    )(act, w_d, x1, mod, gfin)


def kernel(x, c, w_ada, b_ada, g_norm1, w_in, b_gates, g_head_a, rpb, w_out, g_norm2, w_up, conv_w,
           conv_b, w_down, g_final):
    batch, seq, d = x.shape
    assert w_ada.shape[0] == 1
    assert d == D_MODEL and seq % CHUNK_A == 0 and seq % (NA_GROUP * GRID_W) == 0
    x2 = x.reshape(batch * seq, d)
    mod = _ada(c, w_ada, b_ada)[:batch].reshape(batch, N_MOD, d)

    w_in_t = jnp.swapaxes(w_in, 1, 2)
    w_main = _win_prep(w_in_t)
    lane_of = np.concatenate([np.arange(0, 4), np.arange(128, 132), np.arange(4, 8), np.arange(132, 136)])
    sel = np.zeros((4 * NH_A, 256), np.float32)
    sel[np.arange(4 * NH_A), lane_of] = 1.0
    w_g = jnp.dot(sel.T, w_in_t[0, GATE_LO:GATE_HI], precision=lax.Precision.HIGHEST).astype(BF16)
    b_g = jnp.dot(b_gates, sel, precision=lax.Precision.HIGHEST)

    pa, pb, gates = _in_proj(x2, mod, g_norm1, w_main, w_g, b_g, seq)
    hf, hb, w_up_b = _mlstm(pa, gates, w_up, batch, seq)
    hbt, w_down_b, w_out_b = _natten(pb, _natten_bias(rpb[0]), w_down, w_out, batch, seq)
    x1, h2 = _out_proj(hf, hb, pa, hbt, x2, mod, g_head_a, g_norm2, w_out_b, seq)

    cw8 = jnp.zeros((8, D_FF), F32).at[:3].set(conv_w[0])
    act = _up_proj(h2, w_up_b, cw8, conv_b, seq)
    out = _down_proj(act, w_down_b, x1, mod, g_final[None, :], seq)
    return out.reshape(batch, seq, d)
```

```python
import functools
import math

import jax
import jax.numpy as jnp
import numpy as np
from jax import lax
from jax.experimental import pallas as pl
from jax.experimental.pallas import tpu as pltpu

F32 = jnp.float32
BF16 = jnp.bfloat16

D_MODEL = 2048
GRID_W = 64
NH_A = 4
DH_A = 256
W_A = NH_A * DH_A
NH_B = 8
DH_B = 128
W_B = NH_B * DH_B
WIN_ROWS = 8
WIN_COLS = 16
D_FF = 5632
N_MOD = 6
EPS = 1e-6
NEG_BIG = -1e30
LOG2_E = 1.4426950408889634

CHUNK_A = 256

TM_IN = 1024
TN_IN = 1024
TM_OUT = 512
ROW_CHUNK = 256
TM_UP = 1024
TN_UP = D_FF // 2
COL_CHUNK = 256
HALO = 16
TM_DOWN = 512
NA_GROUP = 8
NA_WINDOW = NA_GROUP + WIN_ROWS
NA_KEYS = WIN_ROWS * GRID_W
NA_LOOKAHEAD = 4

V7X_VMEM_BYTES = 64 * 1024 * 1024
VMEM_TEMP_ALLOWANCE = 12 * 1024 * 1024
DOWN_TEMP_ALLOWANCE = 6 * 1024 * 1024


def _vmem_limit(block_bytes, scratch_bytes=0):
    est = 2 * block_bytes + scratch_bytes + VMEM_TEMP_ALLOWANCE
    return int(min(est, V7X_VMEM_BYTES - 4 * 1024 * 1024))


def _nbytes(shape, dtype):
    n = 1
    for s in shape:
        n *= s
    return n * jnp.dtype(dtype).itemsize


def _params(sem, blocks, scratch=()):
    bb = sum(_nbytes(s, d) for s, d in blocks)
    sb = sum(_nbytes(s, d) for s, d in scratch)
    return pltpu.CompilerParams(dimension_semantics=sem, vmem_limit_bytes=_vmem_limit(bb, sb))


def _rms_mod(x, g, shift, scale):
    return x * lax.rsqrt(jnp.mean(x * x, axis=-1, keepdims=True) + EPS) * (g * (1.0 + scale)) + shift


TN_ADA = 1024


def _ada_kernel(ct_ref, w_ref, b_ref, o_ref, s_ref, *, batch):
    @pl.when(pl.program_id(0) == 0)
    def _():
        ct = ct_ref[...]
        s = ct * (1.0 / (1.0 + jnp.exp(-ct)))
        for b in range(batch):
            s_ref[b] = jnp.broadcast_to(s[:, b:b + 1], (D_MODEL, 128))

    row = lax.broadcasted_iota(jnp.int32, (8, 128), 0)
    for l in range(TN_ADA // 128):
        lanes = slice(l * 128, (l + 1) * 128)
        wl = w_ref[:, lanes]
        tile = jnp.zeros((8, 128), F32)
        for b in range(batch):
            tile = jnp.where(row == b, jnp.sum(wl * s_ref[b], axis=0, keepdims=True), tile)
        o_ref[:, lanes] = tile + b_ref[:, lanes]


def _ada(c, w_ada, b_ada):
    batch = c.shape[0]
    n = w_ada.shape[2]
    ct = jnp.zeros((D_MODEL, 8), F32).at[:, :batch].set(c.T)
    return pl.pallas_call(
        functools.partial(_ada_kernel, batch=batch),
        grid=(n // TN_ADA,),
        in_specs=[pl.BlockSpec((D_MODEL, 8), lambda j: (0, 0)),
                  pl.BlockSpec((None, D_MODEL, TN_ADA), lambda j: (0, 0, j)),
                  pl.BlockSpec((1, TN_ADA), lambda j: (0, j))],
        out_specs=pl.BlockSpec((8, TN_ADA), lambda j: (0, j)),
        out_shape=jax.ShapeDtypeStruct((8, n), F32),
        scratch_shapes=[pltpu.VMEM((batch, D_MODEL, 128), F32)],
        compiler_params=_params(("arbitrary",), [((D_MODEL, TN_ADA), F32), ((D_MODEL, 128), F32)],
                                [((batch, D_MODEL, 128), F32)]),
        name="ada",
    )(ct, w_ada, b_ada)


GATE_LO = 4 * W_A
GATE_HI = GATE_LO + 4 * NH_A
TM_PREP = 512
NBLK_A = 4
_NT = (((1,), (1,)), ((), ()))


def _win_prep_kernel(w_ref, o_ref):
    o_ref[...] = w_ref[...].astype(BF16)


def _win_prep(w_t):
    _, cols, d = w_t.shape
    ng = GATE_HI - GATE_LO
    n = cols - ng
    assert GATE_LO % TM_PREP == 0 and n % TM_PREP == 0 and TM_PREP % ng == 0

    def src_row(i):
        past = (i >= GATE_LO // TM_PREP).astype(jnp.int32)
        return (0, (i * (TM_PREP // ng) + past) * ng, 0)

    return pl.pallas_call(
        _win_prep_kernel,
        grid=(n // TM_PREP,),
        in_specs=[pl.BlockSpec((None, pl.Element(TM_PREP), pl.Element(d)), src_row)],
        out_specs=pl.BlockSpec((TM_PREP, d), lambda i: (i, 0)),
        out_shape=jax.ShapeDtypeStruct((n, d), BF16),
        compiler_params=_params(("parallel",), [((TM_PREP, d), F32), ((TM_PREP, d), BF16)]),
        name="w_in_prep",
    )(w_t)


def _in_kernel(x_ref, mod_ref, g_ref, w_ref, wg_ref, bg_ref, pa_ref, pb_ref, gates_ref, h_ref):
    j = pl.program_id(1)

    @pl.when(j == 0)
    def _():
        for c in range(TM_IN // ROW_CHUNK):
            rows = slice(c * ROW_CHUNK, (c + 1) * ROW_CHUNK)
            h = _rms_mod(x_ref[rows, :], g_ref[...], mod_ref[0, 0:1, :], mod_ref[0, 1:2, :])
            hb = h.astype(BF16)
            h_ref[rows, :] = hb
            pa_ref[rows, :] = lax.dot_general(hb, w_ref[...], _NT,
                                              preferred_element_type=F32).astype(BF16)
            gates_ref[rows, :] = lax.dot_general(hb, wg_ref[...], _NT,
                                                 preferred_element_type=F32) + bg_ref[...]

    @pl.when(jnp.logical_and(j > 0, j < NBLK_A))
    def _():
        pa_ref[...] = lax.dot_general(h_ref[...], w_ref[...], _NT,
                                      preferred_element_type=F32).astype(BF16)

    @pl.when(j >= NBLK_A)
    def _():
        r = lax.dot_general(h_ref[...], w_ref[...], _NT, preferred_element_type=F32)
        for head in range(NH_B):
            pb_ref[head] = r[:, head * DH_B:(head + 1) * DH_B].astype(BF16)


def _in_proj(x2, mod, g1, w_main, w_g, b_g, seq):
    t = x2.shape[0]
    n = w_main.shape[0]
    assert TN_IN == W_A == W_B and n == (NBLK_A + 3) * TN_IN
    tiles_per_seq = seq // TM_IN
    return pl.pallas_call(
        _in_kernel,
        grid=(t // TM_IN, n // TN_IN),
        in_specs=[pl.BlockSpec((TM_IN, D_MODEL), lambda i, j: (i, 0)),
                  pl.BlockSpec((1, N_MOD, D_MODEL), lambda i, j: (i // tiles_per_seq, 0, 0)),
                  pl.BlockSpec((1, D_MODEL), lambda i, j: (0, 0)),
                  pl.BlockSpec((TN_IN, D_MODEL), lambda i, j: (j, 0)),
                  pl.BlockSpec((256, D_MODEL), lambda i, j: (0, 0)),
                  pl.BlockSpec((1, 256), lambda i, j: (0, 0))],
        out_specs=[pl.BlockSpec((None, TM_IN, TN_IN), lambda i, j: (jnp.minimum(j, NBLK_A - 1), i, 0)),
                   pl.BlockSpec((NH_B, TM_IN, DH_B), lambda i, j: (jnp.maximum(j - NBLK_A, 0), i, 0)),
                   pl.BlockSpec((TM_IN, 256), lambda i, j: (i, 0))],
        out_shape=[jax.ShapeDtypeStruct((NBLK_A, t, TN_IN), BF16),
                   jax.ShapeDtypeStruct((3 * NH_B, t, DH_B), BF16),
                   jax.ShapeDtypeStruct((t, 256), F32)],
        scratch_shapes=[pltpu.VMEM((TM_IN, D_MODEL), BF16)],
        compiler_params=_params(
            ("parallel", "arbitrary"),
            [((TM_IN, D_MODEL), F32), ((D_MODEL, TN_IN), BF16), ((D_MODEL, 256), BF16),
             ((TM_IN, TN_IN), BF16), ((TM_IN, TN_IN), BF16), ((TM_IN, 256), F32)],
            [((TM_IN, D_MODEL), BF16)]),
        name="in_proj",
    )(x2, mod, g1, w_main, w_g, b_g)


def _seg_scan(x, row, op, fill, reverse):
    n = x.shape[0]
    d = 1
    while d < n:
        if reverse:
            y = pltpu.roll(x, n - d, axis=0)
            x = op(x, jnp.where(row < n - d, y, fill))
        else:
            y = pltpu.roll(x, d, axis=0)
            x = op(x, jnp.where(row >= d, y, fill))
        d *= 2
    return x


def _gate_vectors(g_ref, m_ref, reverse):
    n = CHUNK_A
    gi = g_ref[:, 0:128]
    gf = g_ref[:, 128:256]
    lf = jnp.minimum(gf, 0.0) - jnp.log(1.0 + jnp.exp(-jnp.abs(gf)))
    row = lax.broadcasted_iota(jnp.int32, (n, 128), 0)
    bc = _seg_scan(lf, row, jnp.add, 0.0, reverse)
    rb = gi - bc
    cm = _seg_scan(rb, row, jnp.maximum, -jnp.inf, reverse)
    last = 0 if reverse else n - 1
    gsum = bc[last:last + 1, :]
    m_loc = gsum + cm[last:last + 1, :]
    m = m_ref[...]
    m_inter = bc + m
    m_t = jnp.maximum(m_inter, bc + cm)
    m_new = jnp.maximum(gsum + m, m_loc)
    m_ref[...] = m_new
    col_a = bc - m_t - math.log(DH_A ** 0.5)
    a = jnp.exp(m_inter - m_t)
    e = jnp.exp(-m_t)
    w = jnp.exp(gsum + rb - m_new) * (DH_A ** -0.5)
    dec = jnp.exp(gsum + m - m_new)
    return col_a, a, e, w, dec, rb.T


def _cast_slab_specs(w, steps, imap):
    _, rows, cols = w.shape
    slab = rows // steps
    assert slab * steps == rows and slab % 16 == 0
    in_spec = pl.BlockSpec((None, slab, cols), lambda *ids: (0,) + tuple(imap(*ids)))
    out_spec = pl.BlockSpec((slab, cols), imap)
    return in_spec, out_spec, jax.ShapeDtypeStruct((rows, cols), BF16), ((slab, cols), F32)


def _mlstm_kernel(qf, kf, vf, qb, kb, vb, gf, gb, wsrc, hf, hb, wdst, ct_ref, mf_ref, mb_ref):
    @pl.when(pl.program_id(1) == 0)
    def _():
        ct_ref[...] = jnp.zeros(ct_ref.shape, F32)
        mf_ref[...] = jnp.full((1, 128), NEG_BIG, F32)
        mb_ref[...] = jnp.full((1, 128), NEG_BIG, F32)

    wdst[...] = wsrc[...].astype(BF16)

    n = CHUNK_A
    r = lax.broadcasted_iota(jnp.int32, (n, n), 0)
    c = lax.broadcasted_iota(jnp.int32, (n, n), 1)
    ones_cols = jnp.ones((n, 128), BF16)
    ones_rows = jnp.ones((128, n), BF16)
    seqs = []
    for head in range(NH_A):
        seqs.append((qf, kf, vf, False, hf, head, head, r >= c))
        seqs.append((qb, kb, vb, True, hb, head, NH_A + head, r <= c))

    loaded = []
    for q_ref, k_ref, v_ref, _, _, head, sid, _ in seqs:
        sl = slice(head * DH_A, (head + 1) * DH_A)
        q = q_ref[:, sl]
        k = k_ref[:, sl]
        v = v_ref[:, sl]
        ct_old = ct_ref[sid]
        qk = lax.dot_general(q, k, _NT, preferred_element_type=F32)
        qc = lax.dot_general(q, ct_old.astype(BF16), _NT, preferred_element_type=F32)
        loaded.append((k, v, ct_old, qk, qc))

    gate_vecs = {False: _gate_vectors(gf, mf_ref, reverse=False),
                 True: _gate_vectors(gb, mb_ref, reverse=True)}
    for (_, _, _, rev, h_ref, head, sid, mask), (k, v, ct_old, qk, qc) in zip(seqs, loaded):
        gates = gate_vecs[rev]
        sl = slice(head * DH_A, (head + 1) * DH_A)
        col_a, a, e = (g[:, sid:sid + 1] for g in gates[:3])
        row_b = gates[5][sid:sid + 1, :]
        s = qk * jnp.exp(jnp.where(mask, col_a + row_b, -jnp.inf))
        v_ext = jnp.concatenate([v, ones_cols], axis=1)
        sv = jnp.dot(s.astype(BF16), v_ext, preferred_element_type=F32) + a * qc
        inv = 1.0 / jnp.maximum(jnp.abs(sv[:, DH_A:]), e)
        h_ref[:, sl] = (sv[:, :DH_A] * jnp.concatenate([inv, inv], axis=1)).astype(BF16)

    for (_, _, _, rev, _, _, sid, _), (k, v, ct_old, _, _) in zip(seqs, loaded):
        gates = gate_vecs[rev]
        w = gates[3][:, sid:sid + 1]
        dec = gates[4][:, sid:sid + 1]
        kw = (w * k.astype(F32)).astype(BF16)
        vt_ext = jnp.concatenate([v.T, ones_rows], axis=0)
        ct_ref[sid] = dec * ct_old + jnp.dot(vt_ext, kw, preferred_element_type=F32)


def _mlstm(p, gates, w_cast, batch, seq):
    nc = seq // CHUNK_A
    t = p.shape[1]
    cast_in, cast_out, cast_shape, cast_blk = _cast_slab_specs(w_cast, batch * nc,
                                                               lambda b, k: (b * nc + k, 0))

    def blk(col, rev):
        if rev:
            return pl.BlockSpec((None, CHUNK_A, W_A), lambda b, k: (col, b * nc + nc - 1 - k, 0))
        return pl.BlockSpec((None, CHUNK_A, W_A), lambda b, k: (col, b * nc + k, 0))

    def outblk(rev):
        if rev:
            return pl.BlockSpec((CHUNK_A, W_A), lambda b, k: (b * nc + nc - 1 - k, 0))
        return pl.BlockSpec((CHUNK_A, W_A), lambda b, k: (b * nc + k, 0))

    gatespec = lambda rev: pl.BlockSpec(
        (CHUNK_A, 256), (lambda b, k: (b * nc + nc - 1 - k, 0)) if rev else (lambda b, k: (b * nc + k, 0)))
    return pl.pallas_call(
        _mlstm_kernel,
        grid=(batch, nc),
        in_specs=[blk(0, False), blk(1, False), blk(2, False),
                  blk(0, True), blk(1, True), blk(2, True),
                  gatespec(False), gatespec(True), cast_in],
        out_specs=[outblk(False), outblk(True), cast_out],
        out_shape=[jax.ShapeDtypeStruct((t, W_A), BF16), jax.ShapeDtypeStruct((t, W_A), BF16),
                   cast_shape],
        scratch_shapes=[pltpu.VMEM((2 * NH_A, DH_A + 128, DH_A), F32),
                        pltpu.VMEM((1, 128), F32), pltpu.VMEM((1, 128), F32)],
        compiler_params=_params(
            ("parallel", "arbitrary"),
            [((CHUNK_A, W_A), BF16)] * 8 + [((CHUNK_A, 256), F32)] * 2 + [cast_blk, cast_blk],
            [((2 * NH_A, DH_A + 128, DH_A), F32)]),
        name="mlstm",
    )(p, p, p, p, p, p, gates, gates, w_cast)


def _na_window_start(g, rows):
    return jnp.clip(g * NA_GROUP - WIN_ROWS // 2, 0, rows - NA_WINDOW)


def _natten_kernel(q_ref, kbuf, vbuf, bias_ref, wsrc, o_ref, wdst, *, rows):
    wdst[...] = wsrc[...].astype(BF16)
    g = pl.program_id(1)
    wstart = _na_window_start(g, rows)
    scale = DH_B ** -0.5
    starts, variants = [], []
    for i in range(NA_GROUP):
        r = g * NA_GROUP + i
        rs = jnp.clip(r - WIN_ROWS // 2, 0, rows - WIN_ROWS)
        starts.append(pl.multiple_of((rs - wstart) * GRID_W, GRID_W))
        variants.append(rs - r + (WIN_ROWS - 1))

    def scores(i, h):
        q = q_ref[h, i * GRID_W:(i + 1) * GRID_W, :]
        kw = kbuf[h, pl.ds(starts[i], WIN_ROWS * GRID_W), :]
        s = lax.dot_general(q, kw, _NT, preferred_element_type=F32)
        bias = jnp.concatenate([bias_ref[variants[i], h, l] for l in range(NA_KEYS // 128)], axis=1)
        s = s * (scale * LOG2_E) + bias
        p = jnp.exp2(s - jnp.max(s, axis=-1, keepdims=True))
        return p.astype(BF16), 1.0 / jnp.sum(p, axis=-1, keepdims=True)

    def weighted_sum(i, h, p, inv_l):
        hs = slice(h * DH_B, (h + 1) * DH_B)
        vw = vbuf[h, pl.ds(starts[i], WIN_ROWS * GRID_W), :]
        o = jnp.dot(p, vw, preferred_element_type=F32)
        o_ref[i * GRID_W:(i + 1) * GRID_W, hs] = (o * inv_l).astype(BF16)

    tiles = [(i, h) for i in range(NA_GROUP) for h in range(NH_B)]
    pending = []
    for idx in range(len(tiles) + NA_LOOKAHEAD):
        if idx < len(tiles):
            pending.append(scores(*tiles[idx]))
        if idx >= NA_LOOKAHEAD:
            weighted_sum(*tiles[idx - NA_LOOKAHEAD], *pending[idx - NA_LOOKAHEAD])
            pending[idx - NA_LOOKAHEAD] = None


def _natten(p, bias, w_cast, batch, seq):
    rows = seq // GRID_W
    ng = rows // NA_GROUP
    blk = NA_GROUP * GRID_W
    win = NA_WINDOW * GRID_W
    t = p.shape[1]
    qpart, kpart, vpart = 0, 1, 2

    def window(part):
        def imap(b, g):
            return (part * NH_B, (b * rows + _na_window_start(g, rows)) * GRID_W, 0)
        return pl.BlockSpec((pl.Element(NH_B), pl.Element(win), pl.Element(DH_B)), imap)

    step = lambda b, g: (b * ng + g, 0)
    c_in, c_out, c_shape, c_blk = _cast_slab_specs(w_cast, batch * ng, step)
    return pl.pallas_call(
        functools.partial(_natten_kernel, rows=rows),
        grid=(batch, ng),
        in_specs=[pl.BlockSpec((NH_B, blk, DH_B), lambda b, g: (qpart, b * ng + g, 0)),
                  window(kpart), window(vpart),
                  pl.BlockSpec(bias.shape, lambda b, g: (0, 0, 0, 0, 0), pipeline_mode=pl.Buffered(1)),
                  c_in],
        out_specs=[pl.BlockSpec((blk, W_B), step), c_out],
        out_shape=[jax.ShapeDtypeStruct((t, W_B), BF16), c_shape],
        compiler_params=_params(
            ("parallel", "arbitrary"),
            [((blk, W_B), BF16)] * 2 + [((win, W_B), BF16)] * 2 + [(bias.shape, F32)]
            + [c_blk, c_blk]),
        name="natten",
    )(p, p, p, bias, w_cast)


def _natten_bias(rpb):
    cols = np.arange(GRID_W)
    cstart = np.clip(cols - WIN_COLS // 2, 0, GRID_W - WIN_COLS)
    inside = (cols[None, :] >= cstart[:, None]) & (cols[None, :] < cstart[:, None] + WIN_COLS)
    dc = cols[None, :] - cols[:, None] + (WIN_COLS - 1)
    onehot = (np.arange(2 * WIN_COLS - 1)[:, None, None] == dc[None]) & inside[None]
    toep = jnp.einsum('hdj,jck->hdck', rpb.astype(F32), onehot.astype(np.float32),
                      precision=lax.Precision.HIGHEST)
    rows_per_tile = 128 // GRID_W
    ntiles = NA_KEYS // 128
    key_row = (np.arange(ntiles)[:, None] * rows_per_tile + np.arange(rows_per_tile)[None, :])
    pick = (np.arange(WIN_ROWS)[:, None, None, None] + key_row[None, :, :, None]
            == np.arange(2 * WIN_ROWS - 1)[None, None, None, :])
    tiles = jnp.einsum('hdck,vlrd->vhlcrk', toep * LOG2_E, pick.astype(np.float32),
                       precision=lax.Precision.HIGHEST)
    tiles = tiles + np.where(inside, 0.0, NEG_BIG).astype(np.float32)[:, None, :]
    return tiles.reshape(WIN_ROWS, NH_B, ntiles, GRID_W, 128)


def _out_kernel(hf_ref, hb_ref, oa_ref, hbt_ref, x_ref, mod_ref, gh_ref, gn_ref, w_ref,
                x1_ref, h2_ref):
    for c in range(TM_OUT // ROW_CHUNK):
        rows = slice(c * ROW_CHUNK, (c + 1) * ROW_CHUNK)
        hs = hf_ref[rows, :].astype(F32) + hb_ref[rows, :].astype(F32)
        parts = []
        for head in range(NH_A):
            hh = hs[:, head * DH_A:(head + 1) * DH_A]
            parts.append(hh * lax.rsqrt(jnp.mean(hh * hh, axis=-1, keepdims=True) + EPS))
        hn = jnp.concatenate(parts, axis=-1) * gh_ref[...]
        oa = oa_ref[rows, :].astype(F32)
        ha = (hn * (1.0 / (1.0 + jnp.exp(-oa)))).astype(BF16)
        mixed = jnp.dot(jnp.concatenate([ha, hbt_ref[rows, :]], axis=1), w_ref[...],
                        preferred_element_type=F32)
        x1 = x_ref[rows, :] + mod_ref[0, 2:3, :] * mixed
        x1_ref[rows, :] = x1
        h2_ref[rows, :] = _rms_mod(x1, gn_ref[...], mod_ref[0, 3:4, :], mod_ref[0, 4:5, :]).astype(BF16)


def _out_proj(hf, hb, pa, hbt, x2, mod, gh, g2, w_o, seq):
    t = x2.shape[0]
    tiles_per_seq = seq // TM_OUT
    row = lambda i: (i, 0)
    resident = pl.Buffered(1)
    return pl.pallas_call(
        _out_kernel,
        grid=(t // TM_OUT,),
        in_specs=[pl.BlockSpec((TM_OUT, W_A), row), pl.BlockSpec((TM_OUT, W_A), row),
                  pl.BlockSpec((None, TM_OUT, W_A), lambda i: (3, i, 0)),
                  pl.BlockSpec((TM_OUT, W_B), row),
                  pl.BlockSpec((TM_OUT, D_MODEL), row),
                  pl.BlockSpec((1, N_MOD, D_MODEL), lambda i: (i // tiles_per_seq, 0, 0)),
                  pl.BlockSpec((1, W_A), lambda i: (0, 0)),
                  pl.BlockSpec((1, D_MODEL), lambda i: (0, 0)),
                  pl.BlockSpec((W_A + W_B, D_MODEL), lambda i: (0, 0), pipeline_mode=resident)],
        out_specs=[pl.BlockSpec((TM_OUT, D_MODEL), row), pl.BlockSpec((TM_OUT, D_MODEL), row)],
        out_shape=[jax.ShapeDtypeStruct((t, D_MODEL), F32), jax.ShapeDtypeStruct((t, D_MODEL), BF16)],
        compiler_params=_params(
            ("parallel",),
            [((TM_OUT, W_A), BF16)] * 4 + [((TM_OUT, D_MODEL), F32)] * 2
            + [((TM_OUT, D_MODEL), BF16)] + [((W_A, D_MODEL), BF16)]),
        name="out_proj",
    )(hf, hb, pa, hbt, x2, mod, gh, g2, w_o)


def _up_kernel(hp_ref, h_ref, hn_ref, wu_ref, wg_ref, cw_ref, cb_ref, wsrc, act_ref, wdst, hext_ref, *,
               tiles_per_seq):
    wdst[...] = wsrc[...].astype(BF16)
    pos = pl.program_id(1) % tiles_per_seq
    n = TM_UP + 2 * HALO
    hext_ref[0:HALO, :] = jnp.where(pos > 0, hp_ref[...], jnp.zeros_like(hp_ref))
    hext_ref[HALO:HALO + TM_UP, :] = h_ref[...]
    hext_ref[HALO + TM_UP:, :] = jnp.where(pos < tiles_per_seq - 1, hn_ref[...], jnp.zeros_like(hn_ref))
    for c in range(TN_UP // COL_CHUNK):
        cols = slice(c * COL_CHUNK, (c + 1) * COL_CHUNK)
        u = jnp.dot(h_ref[...], wu_ref[:, cols], preferred_element_type=F32)
        g = jnp.dot(hext_ref[...], wg_ref[:, cols], preferred_element_type=F32)
        g_prev = pltpu.roll(g, 1, axis=0)[HALO:HALO + TM_UP, :]
        g_next = pltpu.roll(g, n - 1, axis=0)[HALO:HALO + TM_UP, :]
        gc = (g_prev * cw_ref[0:1, cols] + g[HALO:HALO + TM_UP, :] * cw_ref[1:2, cols]
              + g_next * cw_ref[2:3, cols] + cb_ref[:, cols])
        gelu = 0.5 * gc * (1.0 + lax.erf(gc * (2.0 ** -0.5)))
        act_ref[:, cols] = (gelu * u).astype(BF16)


def _up_proj(h2, w_up, conv_w, conv_b, w_cast, seq):
    t = h2.shape[0]
    tiles_per_seq = seq // TM_UP
    hb = TM_UP // HALO
    nh = t // HALO
    ngroups = D_FF // TN_UP
    ntiles = t // TM_UP
    resident = pl.Buffered(1)
    c_in, c_out, c_shape, c_blk = _cast_slab_specs(w_cast, ngroups * ntiles,
                                                   lambda j, i: (j * ntiles + i, 0))
    return pl.pallas_call(
        functools.partial(_up_kernel, tiles_per_seq=tiles_per_seq),
        grid=(ngroups, t // TM_UP),
        in_specs=[pl.BlockSpec((HALO, D_MODEL), lambda j, i: (jnp.maximum(i * hb - 1, 0), 0)),
                  pl.BlockSpec((TM_UP, D_MODEL), lambda j, i: (i, 0)),
                  pl.BlockSpec((HALO, D_MODEL), lambda j, i: (jnp.minimum((i + 1) * hb, nh - 1), 0)),
                  pl.BlockSpec((D_MODEL, TN_UP), lambda j, i: (0, j), pipeline_mode=resident),
                  pl.BlockSpec((D_MODEL, TN_UP), lambda j, i: (0, ngroups + j), pipeline_mode=resident),
                  pl.BlockSpec((8, TN_UP), lambda j, i: (0, j)),
                  pl.BlockSpec((1, TN_UP), lambda j, i: (0, j)), c_in],
        out_specs=[pl.BlockSpec((TM_UP, TN_UP), lambda j, i: (i, j)), c_out],
        out_shape=[jax.ShapeDtypeStruct((t, D_FF), BF16), c_shape],
        scratch_shapes=[pltpu.VMEM((TM_UP + 2 * HALO, D_MODEL), BF16)],
        compiler_params=_params(
            ("arbitrary", "arbitrary"),
            [((TM_UP + 2 * HALO, D_MODEL), BF16), ((D_MODEL, TN_UP), BF16), ((TM_UP, TN_UP), BF16),
             c_blk, c_blk],
            [((TM_UP + 2 * HALO, D_MODEL), BF16)]),
        name="up_proj",
    )(h2, h2, h2, w_up, w_up, conv_w, conv_b, w_cast)


def _down_kernel(a_ref, w_ref, x_ref, mod_ref, g_ref, o_ref):
    for c in range(TM_DOWN // ROW_CHUNK):
        rows = slice(c * ROW_CHUNK, (c + 1) * ROW_CHUNK)
        y = x_ref[rows, :] + mod_ref[0, 5:6, :] * jnp.dot(a_ref[rows, :], w_ref[...],
                                                           preferred_element_type=F32)
        o_ref[rows, :] = y * lax.rsqrt(jnp.mean(y * y, axis=-1, keepdims=True) + EPS) * g_ref[...]


def _down_proj(act, w_d, x1, mod, gfin, seq):
    t = x1.shape[0]
    tiles_per_seq = seq // TM_DOWN
    return pl.pallas_call(
        _down_kernel,
        grid=(t // TM_DOWN,),
        in_specs=[pl.BlockSpec((TM_DOWN, D_FF), lambda i: (i, 0)),
                  pl.BlockSpec((D_FF, D_MODEL), lambda i: (0, 0), pipeline_mode=pl.Buffered(1)),
                  pl.BlockSpec((TM_DOWN, D_MODEL), lambda i: (i, 0)),
                  pl.BlockSpec((1, N_MOD, D_MODEL), lambda i: (i // tiles_per_seq, 0, 0)),
                  pl.BlockSpec((1, D_MODEL), lambda i: (0, 0))],
        out_specs=pl.BlockSpec((TM_DOWN, D_MODEL), lambda i: (i, 0)),
        out_shape=jax.ShapeDtypeStruct((t, D_MODEL), F32),
        compiler_params=pltpu.CompilerParams(
            dimension_semantics=("parallel",),
            vmem_limit_bytes=_nbytes((D_FF, D_MODEL), BF16) + 2 * (
                _nbytes((TM_DOWN, D_FF), BF16) + 2 * _nbytes((TM_DOWN, D_MODEL), F32))
            + DOWN_TEMP_ALLOWANCE),
        name="down_proj",
    )(act, w_d, x1, mod, gfin)


def kernel(x, c, w_ada, b_ada, g_norm1, w_in, b_gates, g_head_a, rpb, w_out, g_norm2, w_up, conv_w,
           conv_b, w_down, g_final):
    batch, seq, d = x.shape
    assert w_ada.shape[0] == 1
    assert d == D_MODEL and seq % CHUNK_A == 0 and seq % (NA_GROUP * GRID_W) == 0
    x2 = x.reshape(batch * seq, d)
    mod = _ada(c, w_ada, b_ada)[:batch].reshape(batch, N_MOD, d)

    w_in_t = jnp.swapaxes(w_in, 1, 2)
    w_main = _win_prep(w_in_t)
    lane_of = np.concatenate([np.arange(0, 4), np.arange(128, 132), np.arange(4, 8), np.arange(132, 136)])
    sel = np.zeros((4 * NH_A, 256), np.float32)
    sel[np.arange(4 * NH_A), lane_of] = 1.0
    w_g = jnp.dot(sel.T, w_in_t[0, GATE_LO:GATE_HI], precision=lax.Precision.HIGHEST).astype(BF16)
    b_g = jnp.dot(b_gates, sel, precision=lax.Precision.HIGHEST)

    pa, pb, gates = _in_proj(x2, mod, g_norm1, w_main, w_g, b_g, seq)
    hf, hb, w_up_b = _mlstm(pa, gates, w_up, batch, seq)
    hbt, w_out_b = _natten(pb, _natten_bias(rpb[0]), w_out, batch, seq)
    x1, h2 = _out_proj(hf, hb, pa, hbt, x2, mod, g_head_a, g_norm2, w_out_b, seq)

    cw8 = jnp.zeros((8, D_FF), F32).at[:3].set(conv_w[0])
    act, w_down_b = _up_proj(h2, w_up_b, cw8, conv_b, w_down, seq)
    out = _down_proj(act, w_down_b, x1, mod, g_final[None, :], seq)
    return out.reshape(batch, seq, d)
```

```python
import functools
import math

import jax
import jax.numpy as jnp
import numpy as np
from jax import lax
from jax.experimental import pallas as pl
from jax.experimental.pallas import tpu as pltpu

F32 = jnp.float32
BF16 = jnp.bfloat16

D_MODEL = 2048
GRID_W = 64
NH_A = 4
DH_A = 256
W_A = NH_A * DH_A
NH_B = 8
DH_B = 128
W_B = NH_B * DH_B
WIN_ROWS = 8
WIN_COLS = 16
D_FF = 5632
N_MOD = 6
EPS = 1e-6
NEG_BIG = -1e30
LOG2_E = 1.4426950408889634

CHUNK_A = 256

TM_IN = 1024
TN_IN = 1024
TM_OUT = 512
ROW_CHUNK = 256
TM_UP = 1024
TN_UP = D_FF // 2
COL_CHUNK = 256
HALO = 16
TM_DOWN = 512
NA_GROUP = 8
NA_WINDOW = NA_GROUP + WIN_ROWS
NA_KEYS = WIN_ROWS * GRID_W
NA_LOOKAHEAD = 4

V7X_VMEM_BYTES = 64 * 1024 * 1024
VMEM_TEMP_ALLOWANCE = 12 * 1024 * 1024
DOWN_TEMP_ALLOWANCE = 6 * 1024 * 1024


def _vmem_limit(block_bytes, scratch_bytes=0):
    est = 2 * block_bytes + scratch_bytes + VMEM_TEMP_ALLOWANCE
    return int(min(est, V7X_VMEM_BYTES - 4 * 1024 * 1024))


def _nbytes(shape, dtype):
    n = 1
    for s in shape:
        n *= s
    return n * jnp.dtype(dtype).itemsize


def _params(sem, blocks, scratch=()):
    bb = sum(_nbytes(s, d) for s, d in blocks)
    sb = sum(_nbytes(s, d) for s, d in scratch)
    return pltpu.CompilerParams(dimension_semantics=sem, vmem_limit_bytes=_vmem_limit(bb, sb))


def _rms_mod(x, g, shift, scale):
    return x * lax.rsqrt(jnp.mean(x * x, axis=-1, keepdims=True) + EPS) * (g * (1.0 + scale)) + shift


TN_ADA = 1024


def _ada_kernel(ct_ref, w_ref, b_ref, o_ref, s_ref, *, batch):
    @pl.when(pl.program_id(0) == 0)
    def _():
        ct = ct_ref[...]
        s = ct * (1.0 / (1.0 + jnp.exp(-ct)))
        for b in range(batch):
            s_ref[b] = jnp.broadcast_to(s[:, b:b + 1], (D_MODEL, 128))

    row = lax.broadcasted_iota(jnp.int32, (8, 128), 0)
    for l in range(TN_ADA // 128):
        lanes = slice(l * 128, (l + 1) * 128)
        wl = w_ref[:, lanes]
        tile = jnp.zeros((8, 128), F32)
        for b in range(batch):
            tile = jnp.where(row == b, jnp.sum(wl * s_ref[b], axis=0, keepdims=True), tile)
        o_ref[:, lanes] = tile + b_ref[:, lanes]


def _ada(c, w_ada, b_ada):
    batch = c.shape[0]
    n = w_ada.shape[2]
    ct = jnp.zeros((D_MODEL, 8), F32).at[:, :batch].set(c.T)
    return pl.pallas_call(
        functools.partial(_ada_kernel, batch=batch),
        grid=(n // TN_ADA,),
        in_specs=[pl.BlockSpec((D_MODEL, 8), lambda j: (0, 0)),
                  pl.BlockSpec((None, D_MODEL, TN_ADA), lambda j: (0, 0, j)),
                  pl.BlockSpec((1, TN_ADA), lambda j: (0, j))],
        out_specs=pl.BlockSpec((8, TN_ADA), lambda j: (0, j)),
        out_shape=jax.ShapeDtypeStruct((8, n), F32),
        scratch_shapes=[pltpu.VMEM((batch, D_MODEL, 128), F32)],
        compiler_params=_params(("arbitrary",), [((D_MODEL, TN_ADA), F32), ((D_MODEL, 128), F32)],
                                [((batch, D_MODEL, 128), F32)]),
        name="ada",
    )(ct, w_ada, b_ada)


GATE_LO = 4 * W_A
GATE_HI = GATE_LO + 4 * NH_A
TM_PREP = 512
NBLK_A = 4
_NT = (((1,), (1,)), ((), ()))


def _win_prep_kernel(w_ref, o_ref):
    o_ref[...] = w_ref[...].astype(BF16)


def _win_prep(w_t):
    _, cols, d = w_t.shape
    ng = GATE_HI - GATE_LO
    n = cols - ng
    assert GATE_LO % TM_PREP == 0 and n % TM_PREP == 0 and TM_PREP % ng == 0

    def src_row(i):
        past = (i >= GATE_LO // TM_PREP).astype(jnp.int32)
        return (0, (i * (TM_PREP // ng) + past) * ng, 0)

    return pl.pallas_call(
        _win_prep_kernel,
        grid=(n // TM_PREP,),
        in_specs=[pl.BlockSpec((None, pl.Element(TM_PREP), pl.Element(d)), src_row)],
        out_specs=pl.BlockSpec((TM_PREP, d), lambda i: (i, 0)),
        out_shape=jax.ShapeDtypeStruct((n, d), BF16),
        compiler_params=_params(("parallel",), [((TM_PREP, d), F32), ((TM_PREP, d), BF16)]),
        name="w_in_prep",
    )(w_t)


def _in_kernel(x_ref, mod_ref, g_ref, w_ref, wg_ref, bg_ref, pa_ref, pb_ref, gates_ref, h_ref):
    j = pl.program_id(1)

    @pl.when(j == 0)
    def _():
        for c in range(TM_IN // ROW_CHUNK):
            rows = slice(c * ROW_CHUNK, (c + 1) * ROW_CHUNK)
            h = _rms_mod(x_ref[rows, :], g_ref[...], mod_ref[0, 0:1, :], mod_ref[0, 1:2, :])
            hb = h.astype(BF16)
            h_ref[rows, :] = hb
            pa_ref[rows, :] = lax.dot_general(hb, w_ref[...], _NT,
                                              preferred_element_type=F32).astype(BF16)
            gates_ref[rows, :] = lax.dot_general(hb, wg_ref[...], _NT,
                                                 preferred_element_type=F32) + bg_ref[...]

    @pl.when(jnp.logical_and(j > 0, j < NBLK_A))
    def _():
        pa_ref[...] = lax.dot_general(h_ref[...], w_ref[...], _NT,
                                      preferred_element_type=F32).astype(BF16)

    @pl.when(j >= NBLK_A)
    def _():
        r = lax.dot_general(h_ref[...], w_ref[...], _NT, preferred_element_type=F32)
        for head in range(NH_B):
            pb_ref[head] = r[:, head * DH_B:(head + 1) * DH_B].astype(BF16)


def _in_proj(x2, mod, g1, w_main, w_g, b_g, seq):
    t = x2.shape[0]
    n = w_main.shape[0]
    assert TN_IN == W_A == W_B and n == (NBLK_A + 3) * TN_IN
    tiles_per_seq = seq // TM_IN
    return pl.pallas_call(
        _in_kernel,
        grid=(t // TM_IN, n // TN_IN),
        in_specs=[pl.BlockSpec((TM_IN, D_MODEL), lambda i, j: (i, 0)),
                  pl.BlockSpec((1, N_MOD, D_MODEL), lambda i, j: (i // tiles_per_seq, 0, 0)),
                  pl.BlockSpec((1, D_MODEL), lambda i, j: (0, 0)),
                  pl.BlockSpec((TN_IN, D_MODEL), lambda i, j: (j, 0)),
                  pl.BlockSpec((256, D_MODEL), lambda i, j: (0, 0)),
                  pl.BlockSpec((1, 256), lambda i, j: (0, 0))],
        out_specs=[pl.BlockSpec((None, TM_IN, TN_IN), lambda i, j: (jnp.minimum(j, NBLK_A - 1), i, 0)),
                   pl.BlockSpec((NH_B, TM_IN, DH_B), lambda i, j: (jnp.maximum(j - NBLK_A, 0), i, 0)),
                   pl.BlockSpec((TM_IN, 256), lambda i, j: (i, 0))],
        out_shape=[jax.ShapeDtypeStruct((NBLK_A, t, TN_IN), BF16),
                   jax.ShapeDtypeStruct((3 * NH_B, t, DH_B), BF16),
                   jax.ShapeDtypeStruct((t, 256), F32)],
        scratch_shapes=[pltpu.VMEM((TM_IN, D_MODEL), BF16)],
        compiler_params=_params(
            ("parallel", "arbitrary"),
            [((TM_IN, D_MODEL), F32), ((D_MODEL, TN_IN), BF16), ((D_MODEL, 256), BF16),
             ((TM_IN, TN_IN), BF16), ((TM_IN, TN_IN), BF16), ((TM_IN, 256), F32)],
            [((TM_IN, D_MODEL), BF16)]),
        name="in_proj",
    )(x2, mod, g1, w_main, w_g, b_g)


def _seg_scan(x, row, op, fill, reverse):
    n = x.shape[0]
    d = 1
    while d < n:
        if reverse:
            y = pltpu.roll(x, n - d, axis=0)
            x = op(x, jnp.where(row < n - d, y, fill))
        else:
            y = pltpu.roll(x, d, axis=0)
            x = op(x, jnp.where(row >= d, y, fill))
        d *= 2
    return x


def _gate_vectors(g_ref, m_ref, reverse):
    n = CHUNK_A
    gi = g_ref[:, 0:128]
    gf = g_ref[:, 128:256]
    lf = jnp.minimum(gf, 0.0) - jnp.log(1.0 + jnp.exp(-jnp.abs(gf)))
    row = lax.broadcasted_iota(jnp.int32, (n, 128), 0)
    bc = _seg_scan(lf, row, jnp.add, 0.0, reverse)
    rb = gi - bc
    cm = _seg_scan(rb, row, jnp.maximum, -jnp.inf, reverse)
    last = 0 if reverse else n - 1
    gsum = bc[last:last + 1, :]
    m_loc = gsum + cm[last:last + 1, :]
    m = m_ref[...]
    m_inter = bc + m
    m_t = jnp.maximum(m_inter, bc + cm)
    m_new = jnp.maximum(gsum + m, m_loc)
    m_ref[...] = m_new
    col_a = bc - m_t - math.log(DH_A ** 0.5)
    a = jnp.exp(m_inter - m_t)
    e = jnp.exp(-m_t)
    w = jnp.exp(gsum + rb - m_new) * (DH_A ** -0.5)
    dec = jnp.exp(gsum + m - m_new)
    return col_a, a, e, w, dec, rb.T


def _cast_slab_specs(w, steps, imap):
    _, rows, cols = w.shape
    slab = rows // steps
    assert slab * steps == rows and slab % 16 == 0
    in_spec = pl.BlockSpec((None, slab, cols), lambda *ids: (0,) + tuple(imap(*ids)))
    out_spec = pl.BlockSpec((slab, cols), imap)
    return in_spec, out_spec, jax.ShapeDtypeStruct((rows, cols), BF16), ((slab, cols), F32)


def _mlstm_kernel(qf, kf, vf, qb, kb, vb, gf, gb, wsrc, hf, hb, wdst, ct_ref, mf_ref, mb_ref):
    @pl.when(pl.program_id(1) == 0)
    def _():
        ct_ref[...] = jnp.zeros(ct_ref.shape, F32)
        mf_ref[...] = jnp.full((1, 128), NEG_BIG, F32)
        mb_ref[...] = jnp.full((1, 128), NEG_BIG, F32)

    wdst[...] = wsrc[...].astype(BF16)

    n = CHUNK_A
    r = lax.broadcasted_iota(jnp.int32, (n, n), 0)
    c = lax.broadcasted_iota(jnp.int32, (n, n), 1)
    ones_cols = jnp.ones((n, 128), BF16)
    ones_rows = jnp.ones((128, n), BF16)
    seqs = []
    for head in range(NH_A):
        seqs.append((qf, kf, vf, False, hf, head, head, r >= c))
        seqs.append((qb, kb, vb, True, hb, head, NH_A + head, r <= c))

    loaded = []
    for q_ref, k_ref, v_ref, _, _, head, sid, _ in seqs:
        sl = slice(head * DH_A, (head + 1) * DH_A)
        q = q_ref[:, sl]
        k = k_ref[:, sl]
        v = v_ref[:, sl]
        ct_old = ct_ref[sid]
        qk = lax.dot_general(q, k, _NT, preferred_element_type=F32)
        qc = lax.dot_general(q, ct_old.astype(BF16), _NT, preferred_element_type=F32)
        loaded.append((k, v, ct_old, qk, qc))

    gate_vecs = {False: _gate_vectors(gf, mf_ref, reverse=False),
                 True: _gate_vectors(gb, mb_ref, reverse=True)}
    for (_, _, _, rev, h_ref, head, sid, mask), (k, v, ct_old, qk, qc) in zip(seqs, loaded):
        gates = gate_vecs[rev]
        sl = slice(head * DH_A, (head + 1) * DH_A)
        col_a, a, e = (g[:, sid:sid + 1] for g in gates[:3])
        row_b = gates[5][sid:sid + 1, :]
        s = qk * jnp.exp(jnp.where(mask, col_a + row_b, -jnp.inf))
        v_ext = jnp.concatenate([v, ones_cols], axis=1)
        sv = jnp.dot(s.astype(BF16), v_ext, preferred_element_type=F32) + a * qc
        inv = 1.0 / jnp.maximum(jnp.abs(sv[:, DH_A:]), e)
        h_ref[:, sl] = (sv[:, :DH_A] * jnp.concatenate([inv, inv], axis=1)).astype(BF16)

    for (_, _, _, rev, _, _, sid, _), (k, v, ct_old, _, _) in zip(seqs, loaded):
        gates = gate_vecs[rev]
        w = gates[3][:, sid:sid + 1]
        dec = gates[4][:, sid:sid + 1]
        kw = (w * k.astype(F32)).astype(BF16)
        vt_ext = jnp.concatenate([v.T, ones_rows], axis=0)
        ct_ref[sid] = dec * ct_old + jnp.dot(vt_ext, kw, preferred_element_type=F32)


def _mlstm(p, gates, w_cast, batch, seq):
    nc = seq // CHUNK_A
    t = p.shape[1]
    cast_in, cast_out, cast_shape, cast_blk = _cast_slab_specs(w_cast, batch * nc,
                                                               lambda b, k: (b * nc + k, 0))

    def blk(col, rev):
        if rev:
            return pl.BlockSpec((None, CHUNK_A, W_A), lambda b, k: (col, b * nc + nc - 1 - k, 0))
        return pl.BlockSpec((None, CHUNK_A, W_A), lambda b, k: (col, b * nc + k, 0))

    def outblk(rev):
        if rev:
            return pl.BlockSpec((CHUNK_A, W_A), lambda b, k: (b * nc + nc - 1 - k, 0))
        return pl.BlockSpec((CHUNK_A, W_A), lambda b, k: (b * nc + k, 0))

    gatespec = lambda rev: pl.BlockSpec(
        (CHUNK_A, 256), (lambda b, k: (b * nc + nc - 1 - k, 0)) if rev else (lambda b, k: (b * nc + k, 0)))
    return pl.pallas_call(
        _mlstm_kernel,
        grid=(batch, nc),
        in_specs=[blk(0, False), blk(1, False), blk(2, False),
                  blk(0, True), blk(1, True), blk(2, True),
                  gatespec(False), gatespec(True), cast_in],
        out_specs=[outblk(False), outblk(True), cast_out],
        out_shape=[jax.ShapeDtypeStruct((t, W_A), BF16), jax.ShapeDtypeStruct((t, W_A), BF16),
                   cast_shape],
        scratch_shapes=[pltpu.VMEM((2 * NH_A, DH_A + 128, DH_A), F32),
                        pltpu.VMEM((1, 128), F32), pltpu.VMEM((1, 128), F32)],
        compiler_params=_params(
            ("parallel", "arbitrary"),
            [((CHUNK_A, W_A), BF16)] * 8 + [((CHUNK_A, 256), F32)] * 2 + [cast_blk, cast_blk],
            [((2 * NH_A, DH_A + 128, DH_A), F32)]),
        name="mlstm",
    )(p, p, p, p, p, p, gates, gates, w_cast)


def _na_window_start(g, rows):
    return jnp.clip(g * NA_GROUP - WIN_ROWS // 2, 0, rows - NA_WINDOW)


def _natten_kernel(q_ref, kbuf, vbuf, bias_ref, wsrc, o_ref, wdst, *, rows):
    wdst[...] = wsrc[...].astype(BF16)
    g = pl.program_id(1)
    wstart = _na_window_start(g, rows)
    scale = DH_B ** -0.5
    starts, variants = [], []
    for i in range(NA_GROUP):
        r = g * NA_GROUP + i
        rs = jnp.clip(r - WIN_ROWS // 2, 0, rows - WIN_ROWS)
        starts.append(pl.multiple_of((rs - wstart) * GRID_W, GRID_W))
        variants.append(rs - r + (WIN_ROWS - 1))

    def scores(i, h):
        q = q_ref[h, i * GRID_W:(i + 1) * GRID_W, :]
        kw = kbuf[h, pl.ds(starts[i], WIN_ROWS * GRID_W), :]
        s = lax.dot_general(q, kw, _NT, preferred_element_type=F32)
        bias = jnp.concatenate([bias_ref[variants[i], h, l] for l in range(NA_KEYS // 128)], axis=1)
        s = s * (scale * LOG2_E) + bias
        p = jnp.exp2(s - jnp.max(s, axis=-1, keepdims=True))
        return p.astype(BF16), 1.0 / jnp.sum(p, axis=-1, keepdims=True)

    def weighted_sum(i, h, p, inv_l):
        hs = slice(h * DH_B, (h + 1) * DH_B)
        vw = vbuf[h, pl.ds(starts[i], WIN_ROWS * GRID_W), :]
        o = jnp.dot(p, vw, preferred_element_type=F32)
        o_ref[i * GRID_W:(i + 1) * GRID_W, hs] = (o * inv_l).astype(BF16)

    tiles = [(i, h) for i in range(NA_GROUP) for h in range(NH_B)]
    pending = []
    for idx in range(len(tiles) + NA_LOOKAHEAD):
        if idx < len(tiles):
            pending.append(scores(*tiles[idx]))
        if idx >= NA_LOOKAHEAD:
            weighted_sum(*tiles[idx - NA_LOOKAHEAD], *pending[idx - NA_LOOKAHEAD])
            pending[idx - NA_LOOKAHEAD] = None


def _natten(p, bias, w_cast, batch, seq):
    rows = seq // GRID_W
    ng = rows // NA_GROUP
    blk = NA_GROUP * GRID_W
    win = NA_WINDOW * GRID_W
    t = p.shape[1]
    qpart, kpart, vpart = 0, 1, 2

    def window(part):
        def imap(b, g):
            return (part * NH_B, (b * rows + _na_window_start(g, rows)) * GRID_W, 0)
        return pl.BlockSpec((pl.Element(NH_B), pl.Element(win), pl.Element(DH_B)), imap)

    step = lambda b, g: (b * ng + g, 0)
    c_in, c_out, c_shape, c_blk = _cast_slab_specs(w_cast, batch * ng, step)
    return pl.pallas_call(
        functools.partial(_natten_kernel, rows=rows),
        grid=(batch, ng),
        in_specs=[pl.BlockSpec((NH_B, blk, DH_B), lambda b, g: (qpart, b * ng + g, 0)),
                  window(kpart), window(vpart),
                  pl.BlockSpec(bias.shape, lambda b, g: (0, 0, 0, 0, 0), pipeline_mode=pl.Buffered(1)),
                  c_in],
        out_specs=[pl.BlockSpec((blk, W_B), step), c_out],
        out_shape=[jax.ShapeDtypeStruct((t, W_B), BF16), c_shape],
        compiler_params=_params(
            ("parallel", "arbitrary"),
            [((blk, W_B), BF16)] * 2 + [((win, W_B), BF16)] * 2 + [(bias.shape, F32)]
            + [c_blk, c_blk]),
        name="natten",
    )(p, p, p, bias, w_cast)


def _natten_bias(rpb):
    cols = np.arange(GRID_W)
    cstart = np.clip(cols - WIN_COLS // 2, 0, GRID_W - WIN_COLS)
    inside = (cols[None, :] >= cstart[:, None]) & (cols[None, :] < cstart[:, None] + WIN_COLS)
    dc = cols[None, :] - cols[:, None] + (WIN_COLS - 1)
    onehot = (np.arange(2 * WIN_COLS - 1)[:, None, None] == dc[None]) & inside[None]
    toep = jnp.einsum('hdj,jck->hdck', rpb.astype(F32), onehot.astype(np.float32),
                      precision=lax.Precision.HIGHEST)
    toep = toep * LOG2_E + np.where(inside, 0.0, NEG_BIG).astype(np.float32)
    rows_per_tile = 128 // GRID_W
    ntiles = NA_KEYS // 128
    tiles = []
    for variant in range(WIN_ROWS):
        t = toep[:, variant:variant + WIN_ROWS].reshape(NH_B, ntiles, rows_per_tile, GRID_W, GRID_W)
        tiles.append(t.transpose(0, 1, 3, 2, 4).reshape(NH_B, ntiles, GRID_W, 128))
    return jnp.stack(tiles, axis=0)


def _out_kernel(hf_ref, hb_ref, oa_ref, hbt_ref, x_ref, mod_ref, gh_ref, gn_ref, w_ref,
                x1_ref, h2_ref):
    for c in range(TM_OUT // ROW_CHUNK):
        rows = slice(c * ROW_CHUNK, (c + 1) * ROW_CHUNK)
        hs = hf_ref[rows, :].astype(F32) + hb_ref[rows, :].astype(F32)
        parts = []
        for head in range(NH_A):
            hh = hs[:, head * DH_A:(head + 1) * DH_A]
            parts.append(hh * lax.rsqrt(jnp.mean(hh * hh, axis=-1, keepdims=True) + EPS))
        hn = jnp.concatenate(parts, axis=-1) * gh_ref[...]
        oa = oa_ref[rows, :].astype(F32)
        ha = (hn * (1.0 / (1.0 + jnp.exp(-oa)))).astype(BF16)
        mixed = jnp.dot(jnp.concatenate([ha, hbt_ref[rows, :]], axis=1), w_ref[...],
                        preferred_element_type=F32)
        x1 = x_ref[rows, :] + mod_ref[0, 2:3, :] * mixed
        x1_ref[rows, :] = x1
        h2_ref[rows, :] = _rms_mod(x1, gn_ref[...], mod_ref[0, 3:4, :], mod_ref[0, 4:5, :]).astype(BF16)


def _out_proj(hf, hb, pa, hbt, x2, mod, gh, g2, w_o, seq):
    t = x2.shape[0]
    tiles_per_seq = seq // TM_OUT
    row = lambda i: (i, 0)
    resident = pl.Buffered(1)
    return pl.pallas_call(
        _out_kernel,
        grid=(t // TM_OUT,),
        in_specs=[pl.BlockSpec((TM_OUT, W_A), row), pl.BlockSpec((TM_OUT, W_A), row),
                  pl.BlockSpec((None, TM_OUT, W_A), lambda i: (3, i, 0)),
                  pl.BlockSpec((TM_OUT, W_B), row),
                  pl.BlockSpec((TM_OUT, D_MODEL), row),
                  pl.BlockSpec((1, N_MOD, D_MODEL), lambda i: (i // tiles_per_seq, 0, 0)),
                  pl.BlockSpec((1, W_A), lambda i: (0, 0)),
                  pl.BlockSpec((1, D_MODEL), lambda i: (0, 0)),
                  pl.BlockSpec((W_A + W_B, D_MODEL), lambda i: (0, 0), pipeline_mode=resident)],
        out_specs=[pl.BlockSpec((TM_OUT, D_MODEL), row), pl.BlockSpec((TM_OUT, D_MODEL), row)],
        out_shape=[jax.ShapeDtypeStruct((t, D_MODEL), F32), jax.ShapeDtypeStruct((t, D_MODEL), BF16)],
        compiler_params=_params(
            ("parallel",),
            [((TM_OUT, W_A), BF16)] * 4 + [((TM_OUT, D_MODEL), F32)] * 2
            + [((TM_OUT, D_MODEL), BF16)] + [((W_A, D_MODEL), BF16)]),
        name="out_proj",
    )(hf, hb, pa, hbt, x2, mod, gh, g2, w_o)


def _up_kernel(hp_ref, h_ref, hn_ref, wu_ref, wg_ref, cw_ref, cb_ref, wsrc, act_ref, wdst, hext_ref, *,
               tiles_per_seq):
    wdst[...] = wsrc[...].astype(BF16)
    pos = pl.program_id(1) % tiles_per_seq
    n = TM_UP + 2 * HALO
    hext_ref[0:HALO, :] = jnp.where(pos > 0, hp_ref[...], jnp.zeros_like(hp_ref))
    hext_ref[HALO:HALO + TM_UP, :] = h_ref[...]
    hext_ref[HALO + TM_UP:, :] = jnp.where(pos < tiles_per_seq - 1, hn_ref[...], jnp.zeros_like(hn_ref))
    for c in range(TN_UP // COL_CHUNK):
        cols = slice(c * COL_CHUNK, (c + 1) * COL_CHUNK)
        u = jnp.dot(h_ref[...], wu_ref[:, cols], preferred_element_type=F32)
        g = jnp.dot(hext_ref[...], wg_ref[:, cols], preferred_element_type=F32)
        g_prev = pltpu.roll(g, 1, axis=0)[HALO:HALO + TM_UP, :]
        g_next = pltpu.roll(g, n - 1, axis=0)[HALO:HALO + TM_UP, :]
        gc = (g_prev * cw_ref[0:1, cols] + g[HALO:HALO + TM_UP, :] * cw_ref[1:2, cols]
              + g_next * cw_ref[2:3, cols] + cb_ref[:, cols])
        gelu = 0.5 * gc * (1.0 + lax.erf(gc * (2.0 ** -0.5)))
        act_ref[:, cols] = (gelu * u).astype(BF16)


def _up_proj(h2, w_up, conv_w, conv_b, w_cast, seq):
    t = h2.shape[0]
    tiles_per_seq = seq // TM_UP
    hb = TM_UP // HALO
    nh = t // HALO
    ngroups = D_FF // TN_UP
    ntiles = t // TM_UP
    resident = pl.Buffered(1)
    c_in, c_out, c_shape, c_blk = _cast_slab_specs(w_cast, ngroups * ntiles,
                                                   lambda j, i: (j * ntiles + i, 0))
    return pl.pallas_call(
        functools.partial(_up_kernel, tiles_per_seq=tiles_per_seq),
        grid=(ngroups, t // TM_UP),
        in_specs=[pl.BlockSpec((HALO, D_MODEL), lambda j, i: (jnp.maximum(i * hb - 1, 0), 0)),
                  pl.BlockSpec((TM_UP, D_MODEL), lambda j, i: (i, 0)),
                  pl.BlockSpec((HALO, D_MODEL), lambda j, i: (jnp.minimum((i + 1) * hb, nh - 1), 0)),
                  pl.BlockSpec((D_MODEL, TN_UP), lambda j, i: (0, j), pipeline_mode=resident),
                  pl.BlockSpec((D_MODEL, TN_UP), lambda j, i: (0, ngroups + j), pipeline_mode=resident),
                  pl.BlockSpec((8, TN_UP), lambda j, i: (0, j)),
                  pl.BlockSpec((1, TN_UP), lambda j, i: (0, j)), c_in],
        out_specs=[pl.BlockSpec((TM_UP, TN_UP), lambda j, i: (i, j)), c_out],
        out_shape=[jax.ShapeDtypeStruct((t, D_FF), BF16), c_shape],
        scratch_shapes=[pltpu.VMEM((TM_UP + 2 * HALO, D_MODEL), BF16)],
        compiler_params=_params(
            ("arbitrary", "arbitrary"),
            [((TM_UP + 2 * HALO, D_MODEL), BF16), ((D_MODEL, TN_UP), BF16), ((TM_UP, TN_UP), BF16),
             c_blk, c_blk],
            [((TM_UP + 2 * HALO, D_MODEL), BF16)]),
        name="up_proj",
    )(h2, h2, h2, w_up, w_up, conv_w, conv_b, w_cast)


def _down_kernel(a_ref, w_ref, x_ref, mod_ref, g_ref, o_ref):
    for c in range(TM_DOWN // ROW_CHUNK):
        rows = slice(c * ROW_CHUNK, (c + 1) * ROW_CHUNK)
        y = x_ref[rows, :] + mod_ref[0, 5:6, :] * jnp.dot(a_ref[rows, :], w_ref[...],
                                                           preferred_element_type=F32)
        o_ref[rows, :] = y * lax.rsqrt(jnp.mean(y * y, axis=-1, keepdims=True) + EPS) * g_ref[...]


def _down_proj(act, w_d, x1, mod, gfin, seq):
    t = x1.shape[0]
    tiles_per_seq = seq // TM_DOWN
    return pl.pallas_call(
        _down_kernel,
        grid=(t // TM_DOWN,),
        in_specs=[pl.BlockSpec((TM_DOWN, D_FF), lambda i: (i, 0)),
                  pl.BlockSpec((D_FF, D_MODEL), lambda i: (0, 0), pipeline_mode=pl.Buffered(1)),
                  pl.BlockSpec((TM_DOWN, D_MODEL), lambda i: (i, 0)),
                  pl.BlockSpec((1, N_MOD, D_MODEL), lambda i: (i // tiles_per_seq, 0, 0)),
                  pl.BlockSpec((1, D_MODEL), lambda i: (0, 0))],
        out_specs=pl.BlockSpec((TM_DOWN, D_MODEL), lambda i: (i, 0)),
        out_shape=jax.ShapeDtypeStruct((t, D_MODEL), F32),
        compiler_params=pltpu.CompilerParams(
            dimension_semantics=("parallel",),
            vmem_limit_bytes=_nbytes((D_FF, D_MODEL), BF16) + 2 * (
                _nbytes((TM_DOWN, D_FF), BF16) + 2 * _nbytes((TM_DOWN, D_MODEL), F32))
            + DOWN_TEMP_ALLOWANCE),
        name="down_proj",
    )(act, w_d, x1, mod, gfin)


def kernel(x, c, w_ada, b_ada, g_norm1, w_in, b_gates, g_head_a, rpb, w_out, g_norm2, w_up, conv_w,
           conv_b, w_down, g_final):
    batch, seq, d = x.shape
    assert w_ada.shape[0] == 1
    assert d == D_MODEL and seq % CHUNK_A == 0 and seq % (NA_GROUP * GRID_W) == 0
    x2 = x.reshape(batch * seq, d)
    mod = _ada(c, w_ada, b_ada)[:batch].reshape(batch, N_MOD, d)

    w_in_t = jnp.swapaxes(w_in, 1, 2)
    w_main = _win_prep(w_in_t)
    lane_of = np.concatenate([np.arange(0, 4), np.arange(128, 132), np.arange(4, 8), np.arange(132, 136)])
    sel = np.zeros((4 * NH_A, 256), np.float32)
    sel[np.arange(4 * NH_A), lane_of] = 1.0
    w_g = jnp.dot(sel.T, w_in_t[0, GATE_LO:GATE_HI], precision=lax.Precision.HIGHEST).astype(BF16)
    b_g = jnp.dot(b_gates, sel, precision=lax.Precision.HIGHEST)

    pa, pb, gates = _in_proj(x2, mod, g_norm1, w_main, w_g, b_g, seq)
    hf, hb, w_up_b = _mlstm(pa, gates, w_up, batch, seq)
    hbt, w_out_b = _natten(pb, _natten_bias(rpb[0]), w_out, batch, seq)
    x1, h2 = _out_proj(hf, hb, pa, hbt, x2, mod, g_head_a, g_norm2, w_out_b, seq)

    cw8 = jnp.zeros((8, D_FF), F32).at[:3].set(conv_w[0])
    act, w_down_b = _up_proj(h2, w_up_b, cw8, conv_b, w_down, seq)
    out = _down_proj(act, w_down_b, x1, mod, g_final[None, :], seq)
    return out.reshape(batch, seq, d)
```

```python
import functools
import math

import jax
import jax.numpy as jnp
import numpy as np
from jax import lax
from jax.experimental import pallas as pl
from jax.experimental.pallas import tpu as pltpu

F32 = jnp.float32
BF16 = jnp.bfloat16

D_MODEL = 2048
GRID_W = 64
NH_A = 4
DH_A = 256
W_A = NH_A * DH_A
NH_B = 8
DH_B = 128
W_B = NH_B * DH_B
WIN_ROWS = 8
WIN_COLS = 16
D_FF = 5632
N_MOD = 6
EPS = 1e-6
NEG_BIG = -1e30
LOG2_E = 1.4426950408889634

CHUNK_A = 256

TM_IN = 1024
TN_IN = 1024
TM_OUT = 512
ROW_CHUNK = 256
TM_UP = 1024
TN_UP = D_FF // 2
COL_CHUNK = 256
HALO = 16
TM_DOWN = 512
NA_GROUP = 8
NA_WINDOW = NA_GROUP + WIN_ROWS
NA_KEYS = WIN_ROWS * GRID_W
NA_LOOKAHEAD = 4

V7X_VMEM_BYTES = 64 * 1024 * 1024
VMEM_TEMP_ALLOWANCE = 12 * 1024 * 1024
DOWN_TEMP_ALLOWANCE = 6 * 1024 * 1024


def _vmem_limit(block_bytes, scratch_bytes=0):
    est = 2 * block_bytes + scratch_bytes + VMEM_TEMP_ALLOWANCE
    return int(min(est, V7X_VMEM_BYTES - 4 * 1024 * 1024))


def _nbytes(shape, dtype):
    n = 1
    for s in shape:
        n *= s
    return n * jnp.dtype(dtype).itemsize


def _params(sem, blocks, scratch=()):
    bb = sum(_nbytes(s, d) for s, d in blocks)
    sb = sum(_nbytes(s, d) for s, d in scratch)
    return pltpu.CompilerParams(dimension_semantics=sem, vmem_limit_bytes=_vmem_limit(bb, sb))


def _rms_mod(x, g, shift, scale):
    return x * lax.rsqrt(jnp.mean(x * x, axis=-1, keepdims=True) + EPS) * (g * (1.0 + scale)) + shift


TN_ADA = 1024


def _ada_kernel(ct_ref, w_ref, b_ref, o_ref, s_ref, *, batch):
    @pl.when(pl.program_id(0) == 0)
    def _():
        ct = ct_ref[...]
        s = ct * (1.0 / (1.0 + jnp.exp(-ct)))
        for b in range(batch):
            s_ref[b] = jnp.broadcast_to(s[:, b:b + 1], (D_MODEL, 128))

    row = lax.broadcasted_iota(jnp.int32, (8, 128), 0)
    for l in range(TN_ADA // 128):
        lanes = slice(l * 128, (l + 1) * 128)
        wl = w_ref[:, lanes]
        tile = jnp.zeros((8, 128), F32)
        for b in range(batch):
            tile = jnp.where(row == b, jnp.sum(wl * s_ref[b], axis=0, keepdims=True), tile)
        o_ref[:, lanes] = tile + b_ref[:, lanes]


def _ada(c, w_ada, b_ada):
    batch = c.shape[0]
    n = w_ada.shape[2]
    ct = jnp.zeros((D_MODEL, 8), F32).at[:, :batch].set(c.T)
    return pl.pallas_call(
        functools.partial(_ada_kernel, batch=batch),
        grid=(n // TN_ADA,),
        in_specs=[pl.BlockSpec((D_MODEL, 8), lambda j: (0, 0)),
                  pl.BlockSpec((None, D_MODEL, TN_ADA), lambda j: (0, 0, j)),
                  pl.BlockSpec((1, TN_ADA), lambda j: (0, j))],
        out_specs=pl.BlockSpec((8, TN_ADA), lambda j: (0, j)),
        out_shape=jax.ShapeDtypeStruct((8, n), F32),
        scratch_shapes=[pltpu.VMEM((batch, D_MODEL, 128), F32)],
        compiler_params=_params(("arbitrary",), [((D_MODEL, TN_ADA), F32), ((D_MODEL, 128), F32)],
                                [((batch, D_MODEL, 128), F32)]),
        name="ada",
    )(ct, w_ada, b_ada)


GATE_LO = 4 * W_A
GATE_HI = GATE_LO + 4 * NH_A
TM_PREP = 512
NBLK_A = 4
_NT = (((1,), (1,)), ((), ()))


def _win_prep_kernel(w_ref, o_ref):
    o_ref[...] = w_ref[...].astype(BF16)


def _win_prep(w_t):
    _, cols, d = w_t.shape
    ng = GATE_HI - GATE_LO
    n = cols - ng
    assert GATE_LO % TM_PREP == 0 and n % TM_PREP == 0 and TM_PREP % ng == 0

    def src_row(i):
        past = (i >= GATE_LO // TM_PREP).astype(jnp.int32)
        return (0, (i * (TM_PREP // ng) + past) * ng, 0)

    return pl.pallas_call(
        _win_prep_kernel,
        grid=(n // TM_PREP,),
        in_specs=[pl.BlockSpec((None, pl.Element(TM_PREP), pl.Element(d)), src_row)],
        out_specs=pl.BlockSpec((TM_PREP, d), lambda i: (i, 0)),
        out_shape=jax.ShapeDtypeStruct((n, d), BF16),
        compiler_params=_params(("parallel",), [((TM_PREP, d), F32), ((TM_PREP, d), BF16)]),
        name="w_in_prep",
    )(w_t)


def _in_kernel(x_ref, mod_ref, g_ref, w_ref, wg_ref, bg_ref, pa_ref, pb_ref, gates_ref, h_ref):
    j = pl.program_id(1)

    @pl.when(j == 0)
    def _():
        for c in range(TM_IN // ROW_CHUNK):
            rows = slice(c * ROW_CHUNK, (c + 1) * ROW_CHUNK)
            h = _rms_mod(x_ref[rows, :], g_ref[...], mod_ref[0, 0:1, :], mod_ref[0, 1:2, :])
            hb = h.astype(BF16)
            h_ref[rows, :] = hb
            pa_ref[rows, :] = lax.dot_general(hb, w_ref[...], _NT,
                                              preferred_element_type=F32).astype(BF16)
            gates_ref[rows, :] = lax.dot_general(hb, wg_ref[...], _NT,
                                                 preferred_element_type=F32) + bg_ref[...]

    @pl.when(jnp.logical_and(j > 0, j < NBLK_A))
    def _():
        pa_ref[...] = lax.dot_general(h_ref[...], w_ref[...], _NT,
                                      preferred_element_type=F32).astype(BF16)

    @pl.when(j >= NBLK_A)
    def _():
        r = lax.dot_general(h_ref[...], w_ref[...], _NT, preferred_element_type=F32)
        for head in range(NH_B):
            pb_ref[head] = r[:, head * DH_B:(head + 1) * DH_B].astype(BF16)


def _in_proj(x2, mod, g1, w_main, w_g, b_g, seq):
    t = x2.shape[0]
    n = w_main.shape[0]
    assert TN_IN == W_A == W_B and n == (NBLK_A + 3) * TN_IN
    tiles_per_seq = seq // TM_IN
    return pl.pallas_call(
        _in_kernel,
        grid=(t // TM_IN, n // TN_IN),
        in_specs=[pl.BlockSpec((TM_IN, D_MODEL), lambda i, j: (i, 0)),
                  pl.BlockSpec((1, N_MOD, D_MODEL), lambda i, j: (i // tiles_per_seq, 0, 0)),
                  pl.BlockSpec((1, D_MODEL), lambda i, j: (0, 0)),
                  pl.BlockSpec((TN_IN, D_MODEL), lambda i, j: (j, 0)),
                  pl.BlockSpec((256, D_MODEL), lambda i, j: (0, 0)),
                  pl.BlockSpec((1, 256), lambda i, j: (0, 0))],
        out_specs=[pl.BlockSpec((None, TM_IN, TN_IN), lambda i, j: (jnp.minimum(j, NBLK_A - 1), i, 0)),
                   pl.BlockSpec((NH_B, TM_IN, DH_B), lambda i, j: (jnp.maximum(j - NBLK_A, 0), i, 0)),
                   pl.BlockSpec((TM_IN, 256), lambda i, j: (i, 0))],
        out_shape=[jax.ShapeDtypeStruct((NBLK_A, t, TN_IN), BF16),
                   jax.ShapeDtypeStruct((3 * NH_B, t, DH_B), BF16),
                   jax.ShapeDtypeStruct((t, 256), F32)],
        scratch_shapes=[pltpu.VMEM((TM_IN, D_MODEL), BF16)],
        compiler_params=_params(
            ("parallel", "arbitrary"),
            [((TM_IN, D_MODEL), F32), ((D_MODEL, TN_IN), BF16), ((D_MODEL, 256), BF16),
             ((TM_IN, TN_IN), BF16), ((TM_IN, TN_IN), BF16), ((TM_IN, 256), F32)],
            [((TM_IN, D_MODEL), BF16)]),
        name="in_proj",
    )(x2, mod, g1, w_main, w_g, b_g)


def _seg_scan(x, row, op, fill, reverse):
    n = x.shape[0]
    d = 1
    while d < n:
        if reverse:
            y = pltpu.roll(x, n - d, axis=0)
            x = op(x, jnp.where(row < n - d, y, fill))
        else:
            y = pltpu.roll(x, d, axis=0)
            x = op(x, jnp.where(row >= d, y, fill))
        d *= 2
    return x


def _gate_vectors(g_ref, m_ref, reverse):
    n = CHUNK_A
    gi = g_ref[:, 0:128]
    gf = g_ref[:, 128:256]
    lf = jnp.minimum(gf, 0.0) - jnp.log(1.0 + jnp.exp(-jnp.abs(gf)))
    row = lax.broadcasted_iota(jnp.int32, (n, 128), 0)
    bc = _seg_scan(lf, row, jnp.add, 0.0, reverse)
    rb = gi - bc
    cm = _seg_scan(rb, row, jnp.maximum, -jnp.inf, reverse)
    last = 0 if reverse else n - 1
    gsum = bc[last:last + 1, :]
    m_loc = gsum + cm[last:last + 1, :]
    m = m_ref[...]
    m_inter = bc + m
    m_t = jnp.maximum(m_inter, bc + cm)
    m_new = jnp.maximum(gsum + m, m_loc)
    m_ref[...] = m_new
    col_a = bc - m_t - math.log(DH_A ** 0.5)
    a = jnp.exp(m_inter - m_t)
    e = jnp.exp(-m_t)
    w = jnp.exp(gsum + rb - m_new) * (DH_A ** -0.5)
    dec = jnp.exp(gsum + m - m_new)
    return col_a, a, e, w, dec, rb.T


def _cast_slab_specs(w, steps, imap):
    _, rows, cols = w.shape
    slab = rows // steps
    assert slab * steps == rows and slab % 16 == 0
    in_spec = pl.BlockSpec((None, slab, cols), lambda *ids: (0,) + tuple(imap(*ids)))
    out_spec = pl.BlockSpec((slab, cols), imap)
    return in_spec, out_spec, jax.ShapeDtypeStruct((rows, cols), BF16), ((slab, cols), F32)


def _mlstm_kernel(qf, kf, vf, qb, kb, vb, gf, gb, wsrc, hf, hb, wdst, ct_ref, mf_ref, mb_ref):
    @pl.when(pl.program_id(1) == 0)
    def _():
        ct_ref[...] = jnp.zeros(ct_ref.shape, F32)
        mf_ref[...] = jnp.full((1, 128), NEG_BIG, F32)
        mb_ref[...] = jnp.full((1, 128), NEG_BIG, F32)

    wdst[...] = wsrc[...].astype(BF16)

    n = CHUNK_A
    r = lax.broadcasted_iota(jnp.int32, (n, n), 0)
    c = lax.broadcasted_iota(jnp.int32, (n, n), 1)
    ones_cols = jnp.ones((n, 128), BF16)
    ones_rows = jnp.ones((128, n), BF16)
    seqs = []
    for head in range(NH_A):
        seqs.append((qf, kf, vf, False, hf, head, head, r >= c))
        seqs.append((qb, kb, vb, True, hb, head, NH_A + head, r <= c))

    loaded = []
    for q_ref, k_ref, v_ref, _, _, head, sid, _ in seqs:
        sl = slice(head * DH_A, (head + 1) * DH_A)
        q = q_ref[:, sl]
        k = k_ref[:, sl]
        v = v_ref[:, sl]
        ct_old = ct_ref[sid]
        qk = lax.dot_general(q, k, _NT, preferred_element_type=F32)
        qc = lax.dot_general(q, ct_old.astype(BF16), _NT, preferred_element_type=F32)
        loaded.append((k, v, ct_old, qk, qc))

    gate_vecs = {False: _gate_vectors(gf, mf_ref, reverse=False),
                 True: _gate_vectors(gb, mb_ref, reverse=True)}
    for (_, _, _, rev, h_ref, head, sid, mask), (k, v, ct_old, qk, qc) in zip(seqs, loaded):
        gates = gate_vecs[rev]
        sl = slice(head * DH_A, (head + 1) * DH_A)
        col_a, a, e = (g[:, sid:sid + 1] for g in gates[:3])
        row_b = gates[5][sid:sid + 1, :]
        s = qk * jnp.exp(jnp.where(mask, col_a + row_b, -jnp.inf))
        v_ext = jnp.concatenate([v, ones_cols], axis=1)
        sv = jnp.dot(s.astype(BF16), v_ext, preferred_element_type=F32) + a * qc
        inv = 1.0 / jnp.maximum(jnp.abs(sv[:, DH_A:]), e)
        h_ref[:, sl] = (sv[:, :DH_A] * jnp.concatenate([inv, inv], axis=1)).astype(BF16)

    for (_, _, _, rev, _, _, sid, _), (k, v, ct_old, _, _) in zip(seqs, loaded):
        gates = gate_vecs[rev]
        w = gates[3][:, sid:sid + 1]
        dec = gates[4][:, sid:sid + 1]
        kw = (w * k.astype(F32)).astype(BF16)
        vt_ext = jnp.concatenate([v.T, ones_rows], axis=0)
        ct_ref[sid] = dec * ct_old + jnp.dot(vt_ext, kw, preferred_element_type=F32)


def _mlstm(p, gates, w_cast, batch, seq):
    nc = seq // CHUNK_A
    t = p.shape[1]
    cast_in, cast_out, cast_shape, cast_blk = _cast_slab_specs(w_cast, batch * nc,
                                                               lambda b, k: (b * nc + k, 0))

    def blk(col, rev):
        if rev:
            return pl.BlockSpec((None, CHUNK_A, W_A), lambda b, k: (col, b * nc + nc - 1 - k, 0))
        return pl.BlockSpec((None, CHUNK_A, W_A), lambda b, k: (col, b * nc + k, 0))

    def outblk(rev):
        if rev:
            return pl.BlockSpec((CHUNK_A, W_A), lambda b, k: (b * nc + nc - 1 - k, 0))
        return pl.BlockSpec((CHUNK_A, W_A), lambda b, k: (b * nc + k, 0))

    gatespec = lambda rev: pl.BlockSpec(
        (CHUNK_A, 256), (lambda b, k: (b * nc + nc - 1 - k, 0)) if rev else (lambda b, k: (b * nc + k, 0)))
    return pl.pallas_call(
        _mlstm_kernel,
        grid=(batch, nc),
        in_specs=[blk(0, False), blk(1, False), blk(2, False),
                  blk(0, True), blk(1, True), blk(2, True),
                  gatespec(False), gatespec(True), cast_in],
        out_specs=[outblk(False), outblk(True), cast_out],
        out_shape=[jax.ShapeDtypeStruct((t, W_A), BF16), jax.ShapeDtypeStruct((t, W_A), BF16),
                   cast_shape],
        scratch_shapes=[pltpu.VMEM((2 * NH_A, DH_A + 128, DH_A), F32),
                        pltpu.VMEM((1, 128), F32), pltpu.VMEM((1, 128), F32)],
        compiler_params=_params(
            ("parallel", "arbitrary"),
            [((CHUNK_A, W_A), BF16)] * 8 + [((CHUNK_A, 256), F32)] * 2 + [cast_blk, cast_blk],
            [((2 * NH_A, DH_A + 128, DH_A), F32)]),
        name="mlstm",
    )(p, p, p, p, p, p, gates, gates, w_cast)


def _na_window_start(g, rows):
    return jnp.clip(g * NA_GROUP - WIN_ROWS // 2, 0, rows - NA_WINDOW)


def _natten_kernel(q_ref, kbuf, vbuf, bias_ref, wsrc, o_ref, wdst, *, rows):
    wdst[...] = wsrc[...].astype(BF16)
    g = pl.program_id(1)
    wstart = _na_window_start(g, rows)
    scale = DH_B ** -0.5
    starts, variants = [], []
    for i in range(NA_GROUP):
        r = g * NA_GROUP + i
        rs = jnp.clip(r - WIN_ROWS // 2, 0, rows - WIN_ROWS)
        starts.append(pl.multiple_of((rs - wstart) * GRID_W, GRID_W))
        variants.append(rs - r + (WIN_ROWS - 1))

    def scores(i, h):
        q = q_ref[h, i * GRID_W:(i + 1) * GRID_W, :]
        kw = kbuf[h, pl.ds(starts[i], WIN_ROWS * GRID_W), :]
        s = lax.dot_general(q, kw, _NT, preferred_element_type=F32)
        bias = jnp.concatenate([bias_ref[variants[i], h, l] for l in range(NA_KEYS // 128)], axis=1)
        s = s * (scale * LOG2_E) + bias
        p = jnp.exp2(s - jnp.max(s, axis=-1, keepdims=True))
        return p.astype(BF16), 1.0 / jnp.sum(p, axis=-1, keepdims=True)

    def weighted_sum(i, h, p, inv_l):
        hs = slice(h * DH_B, (h + 1) * DH_B)
        vw = vbuf[h, pl.ds(starts[i], WIN_ROWS * GRID_W), :]
        o = jnp.dot(p, vw, preferred_element_type=F32)
        o_ref[i * GRID_W:(i + 1) * GRID_W, hs] = (o * inv_l).astype(BF16)

    tiles = [(i, h) for i in range(NA_GROUP) for h in range(NH_B)]
    pending = []
    for idx in range(len(tiles) + NA_LOOKAHEAD):
        if idx < len(tiles):
            pending.append(scores(*tiles[idx]))
        if idx >= NA_LOOKAHEAD:
            weighted_sum(*tiles[idx - NA_LOOKAHEAD], *pending[idx - NA_LOOKAHEAD])
            pending[idx - NA_LOOKAHEAD] = None


def _natten(p, bias, w_cast, batch, seq):
    rows = seq // GRID_W
    ng = rows // NA_GROUP
    blk = NA_GROUP * GRID_W
    win = NA_WINDOW * GRID_W
    t = p.shape[1]
    qpart, kpart, vpart = 0, 1, 2

    def window(part):
        def imap(b, g):
            return (part * NH_B, (b * rows + _na_window_start(g, rows)) * GRID_W, 0)
        return pl.BlockSpec((pl.Element(NH_B), pl.Element(win), pl.Element(DH_B)), imap)

    step = lambda b, g: (b * ng + g, 0)
    c_in, c_out, c_shape, c_blk = _cast_slab_specs(w_cast, batch * ng, step)
    return pl.pallas_call(
        functools.partial(_natten_kernel, rows=rows),
        grid=(batch, ng),
        in_specs=[pl.BlockSpec((NH_B, blk, DH_B), lambda b, g: (qpart, b * ng + g, 0)),
                  window(kpart), window(vpart),
                  pl.BlockSpec(bias.shape, lambda b, g: (0, 0, 0, 0, 0), pipeline_mode=pl.Buffered(1)),
                  c_in],
        out_specs=[pl.BlockSpec((blk, W_B), step), c_out],
        out_shape=[jax.ShapeDtypeStruct((t, W_B), BF16), c_shape],
        compiler_params=_params(
            ("parallel", "arbitrary"),
            [((blk, W_B), BF16)] * 2 + [((win, W_B), BF16)] * 2 + [(bias.shape, F32)]
            + [c_blk, c_blk]),
        name="natten",
    )(p, p, p, bias, w_cast)


def _natten_bias(rpb):
    cols = np.arange(GRID_W)
    cstart = np.clip(cols - WIN_COLS // 2, 0, GRID_W - WIN_COLS)
    inside = (cols[None, :] >= cstart[:, None]) & (cols[None, :] < cstart[:, None] + WIN_COLS)
    dc = cols[None, :] - cols[:, None] + (WIN_COLS - 1)
    onehot = (np.arange(2 * WIN_COLS - 1)[:, None, None] == dc[None]) & inside[None]
    toep = jnp.einsum('hdj,jck->hdck', rpb.astype(F32), onehot.astype(np.float32),
                      precision=lax.Precision.HIGHEST)
    toep = toep * LOG2_E + np.where(inside, 0.0, NEG_BIG).astype(np.float32)
    ntiles = NA_KEYS // 128
    return pl.pallas_call(
        _bias_tiles_kernel,
        grid=(NH_B,),
        in_specs=[pl.BlockSpec((None, 2 * WIN_ROWS - 1, GRID_W, GRID_W), lambda h: (h, 0, 0, 0))],
        out_specs=pl.BlockSpec((WIN_ROWS, None, ntiles, GRID_W, 128), lambda h: (0, h, 0, 0, 0)),
        out_shape=jax.ShapeDtypeStruct((WIN_ROWS, NH_B, ntiles, GRID_W, 128), F32),
        compiler_params=_params(("parallel",), [((2 * WIN_ROWS - 1, GRID_W, 128), F32),
                                                ((WIN_ROWS, ntiles, GRID_W, 128), F32)]),
        name="bias_tiles",
    )(toep)


def _bias_tiles_kernel(t_ref, o_ref):
    rows_per_tile = 128 // GRID_W
    for v in range(WIN_ROWS):
        for l in range(NA_KEYS // 128):
            d = v + l * rows_per_tile
            o_ref[v, l] = jnp.concatenate([t_ref[d + r] for r in range(rows_per_tile)], axis=1)


def _out_kernel(hf_ref, hb_ref, oa_ref, hbt_ref, x_ref, mod_ref, gh_ref, gn_ref, w_ref,
                x1_ref, h2_ref):
    for c in range(TM_OUT // ROW_CHUNK):
        rows = slice(c * ROW_CHUNK, (c + 1) * ROW_CHUNK)
        hs = hf_ref[rows, :].astype(F32) + hb_ref[rows, :].astype(F32)
        parts = []
        for head in range(NH_A):
            hh = hs[:, head * DH_A:(head + 1) * DH_A]
            parts.append(hh * lax.rsqrt(jnp.mean(hh * hh, axis=-1, keepdims=True) + EPS))
        hn = jnp.concatenate(parts, axis=-1) * gh_ref[...]
        oa = oa_ref[rows, :].astype(F32)
        ha = (hn * (1.0 / (1.0 + jnp.exp(-oa)))).astype(BF16)
        mixed = jnp.dot(jnp.concatenate([ha, hbt_ref[rows, :]], axis=1), w_ref[...],
                        preferred_element_type=F32)
        x1 = x_ref[rows, :] + mod_ref[0, 2:3, :] * mixed
        x1_ref[rows, :] = x1
        h2_ref[rows, :] = _rms_mod(x1, gn_ref[...], mod_ref[0, 3:4, :], mod_ref[0, 4:5, :]).astype(BF16)


def _out_proj(hf, hb, pa, hbt, x2, mod, gh, g2, w_o, seq):
    t = x2.shape[0]
    tiles_per_seq = seq // TM_OUT
    row = lambda i: (i, 0)
    resident = pl.Buffered(1)
    return pl.pallas_call(
        _out_kernel,
        grid=(t // TM_OUT,),
        in_specs=[pl.BlockSpec((TM_OUT, W_A), row), pl.BlockSpec((TM_OUT, W_A), row),
                  pl.BlockSpec((None, TM_OUT, W_A), lambda i: (3, i, 0)),
                  pl.BlockSpec((TM_OUT, W_B), row),
                  pl.BlockSpec((TM_OUT, D_MODEL), row),
                  pl.BlockSpec((1, N_MOD, D_MODEL), lambda i: (i // tiles_per_seq, 0, 0)),
                  pl.BlockSpec((1, W_A), lambda i: (0, 0)),
                  pl.BlockSpec((1, D_MODEL), lambda i: (0, 0)),
                  pl.BlockSpec((W_A + W_B, D_MODEL), lambda i: (0, 0), pipeline_mode=resident)],
        out_specs=[pl.BlockSpec((TM_OUT, D_MODEL), row), pl.BlockSpec((TM_OUT, D_MODEL), row)],
        out_shape=[jax.ShapeDtypeStruct((t, D_MODEL), F32), jax.ShapeDtypeStruct((t, D_MODEL), BF16)],
        compiler_params=_params(
            ("parallel",),
            [((TM_OUT, W_A), BF16)] * 4 + [((TM_OUT, D_MODEL), F32)] * 2
            + [((TM_OUT, D_MODEL), BF16)] + [((W_A, D_MODEL), BF16)]),
        name="out_proj",
    )(hf, hb, pa, hbt, x2, mod, gh, g2, w_o)


def _up_kernel(hp_ref, h_ref, hn_ref, wu_ref, wg_ref, cw_ref, cb_ref, wsrc, act_ref, wdst, hext_ref, *,
               tiles_per_seq):
    wdst[...] = wsrc[...].astype(BF16)
    pos = pl.program_id(1) % tiles_per_seq
    n = TM_UP + 2 * HALO
    hext_ref[0:HALO, :] = jnp.where(pos > 0, hp_ref[...], jnp.zeros_like(hp_ref))
    hext_ref[HALO:HALO + TM_UP, :] = h_ref[...]
    hext_ref[HALO + TM_UP:, :] = jnp.where(pos < tiles_per_seq - 1, hn_ref[...], jnp.zeros_like(hn_ref))
    for c in range(TN_UP // COL_CHUNK):
        cols = slice(c * COL_CHUNK, (c + 1) * COL_CHUNK)
        u = jnp.dot(h_ref[...], wu_ref[:, cols], preferred_element_type=F32)
        g = jnp.dot(hext_ref[...], wg_ref[:, cols], preferred_element_type=F32)
        g_prev = pltpu.roll(g, 1, axis=0)[HALO:HALO + TM_UP, :]
        g_next = pltpu.roll(g, n - 1, axis=0)[HALO:HALO + TM_UP, :]
        gc = (g_prev * cw_ref[0:1, cols] + g[HALO:HALO + TM_UP, :] * cw_ref[1:2, cols]
              + g_next * cw_ref[2:3, cols] + cb_ref[:, cols])
        gelu = 0.5 * gc * (1.0 + lax.erf(gc * (2.0 ** -0.5)))
        act_ref[:, cols] = (gelu * u).astype(BF16)


def _up_proj(h2, w_up, conv_w, conv_b, w_cast, seq):
    t = h2.shape[0]
    tiles_per_seq = seq // TM_UP
    hb = TM_UP // HALO
    nh = t // HALO
    ngroups = D_FF // TN_UP
    ntiles = t // TM_UP
    resident = pl.Buffered(1)
    c_in, c_out, c_shape, c_blk = _cast_slab_specs(w_cast, ngroups * ntiles,
                                                   lambda j, i: (j * ntiles + i, 0))
    return pl.pallas_call(
        functools.partial(_up_kernel, tiles_per_seq=tiles_per_seq),
        grid=(ngroups, t // TM_UP),
        in_specs=[pl.BlockSpec((HALO, D_MODEL), lambda j, i: (jnp.maximum(i * hb - 1, 0), 0)),
                  pl.BlockSpec((TM_UP, D_MODEL), lambda j, i: (i, 0)),
                  pl.BlockSpec((HALO, D_MODEL), lambda j, i: (jnp.minimum((i + 1) * hb, nh - 1), 0)),
                  pl.BlockSpec((D_MODEL, TN_UP), lambda j, i: (0, j), pipeline_mode=resident),
                  pl.BlockSpec((D_MODEL, TN_UP), lambda j, i: (0, ngroups + j), pipeline_mode=resident),
                  pl.BlockSpec((8, TN_UP), lambda j, i: (0, j)),
                  pl.BlockSpec((1, TN_UP), lambda j, i: (0, j)), c_in],
        out_specs=[pl.BlockSpec((TM_UP, TN_UP), lambda j, i: (i, j)), c_out],
        out_shape=[jax.ShapeDtypeStruct((t, D_FF), BF16), c_shape],
        scratch_shapes=[pltpu.VMEM((TM_UP + 2 * HALO, D_MODEL), BF16)],
        compiler_params=_params(
            ("arbitrary", "arbitrary"),
            [((TM_UP + 2 * HALO, D_MODEL), BF16), ((D_MODEL, TN_UP), BF16), ((TM_UP, TN_UP), BF16),
             c_blk, c_blk],
            [((TM_UP + 2 * HALO, D_MODEL), BF16)]),
        name="up_proj",
    )(h2, h2, h2, w_up, w_up, conv_w, conv_b, w_cast)


def _down_kernel(a_ref, w_ref, x_ref, mod_ref, g_ref, o_ref):
    for c in range(TM_DOWN // ROW_CHUNK):
        rows = slice(c * ROW_CHUNK, (c + 1) * ROW_CHUNK)
        y = x_ref[rows, :] + mod_ref[0, 5:6, :] * jnp.dot(a_ref[rows, :], w_ref[...],
                                                           preferred_element_type=F32)
        o_ref[rows, :] = y * lax.rsqrt(jnp.mean(y * y, axis=-1, keepdims=True) + EPS) * g_ref[...]


def _down_proj(act, w_d, x1, mod, gfin, seq):
    t = x1.shape[0]
    tiles_per_seq = seq // TM_DOWN
    return pl.pallas_call(
        _down_kernel,
        grid=(t // TM_DOWN,),
        in_specs=[pl.BlockSpec((TM_DOWN, D_FF), lambda i: (i, 0)),
                  pl.BlockSpec((D_FF, D_MODEL), lambda i: (0, 0), pipeline_mode=pl.Buffered(1)),
                  pl.BlockSpec((TM_DOWN, D_MODEL), lambda i: (i, 0)),
                  pl.BlockSpec((1, N_MOD, D_MODEL), lambda i: (i // tiles_per_seq, 0, 0)),
                  pl.BlockSpec((1, D_MODEL), lambda i: (0, 0))],
        out_specs=pl.BlockSpec((TM_DOWN, D_MODEL), lambda i: (i, 0)),
        out_shape=jax.ShapeDtypeStruct((t, D_MODEL), F32),
        compiler_params=pltpu.CompilerParams(
            dimension_semantics=("parallel",),
            vmem_limit_bytes=_nbytes((D_FF, D_MODEL), BF16) + 2 * (
                _nbytes((TM_DOWN, D_FF), BF16) + 2 * _nbytes((TM_DOWN, D_MODEL), F32))
            + DOWN_TEMP_ALLOWANCE),
        name="down_proj",
    )(act, w_d, x1, mod, gfin)


def kernel(x, c, w_ada, b_ada, g_norm1, w_in, b_gates, g_head_a, rpb, w_out, g_norm2, w_up, conv_w,
           conv_b, w_down, g_final):
    batch, seq, d = x.shape
    assert w_ada.shape[0] == 1
    assert d == D_MODEL and seq % CHUNK_A == 0 and seq % (NA_GROUP * GRID_W) == 0
    x2 = x.reshape(batch * seq, d)
    mod = _ada(c, w_ada, b_ada)[:batch].reshape(batch, N_MOD, d)

    w_in_t = jnp.swapaxes(w_in, 1, 2)
    w_main = _win_prep(w_in_t)
    lane_of = np.concatenate([np.arange(0, 4), np.arange(128, 132), np.arange(4, 8), np.arange(132, 136)])
    sel = np.zeros((4 * NH_A, 256), np.float32)
    sel[np.arange(4 * NH_A), lane_of] = 1.0
    w_g = jnp.dot(sel.T, w_in_t[0, GATE_LO:GATE_HI], precision=lax.Precision.HIGHEST).astype(BF16)
    b_g = jnp.dot(b_gates, sel, precision=lax.Precision.HIGHEST)

    pa, pb, gates = _in_proj(x2, mod, g_norm1, w_main, w_g, b_g, seq)
    hf, hb, w_up_b = _mlstm(pa, gates, w_up, batch, seq)
    hbt, w_out_b = _natten(pb, _natten_bias(rpb[0]), w_out, batch, seq)
    x1, h2 = _out_proj(hf, hb, pa, hbt, x2, mod, g_head_a, g_norm2, w_out_b, seq)

    cw8 = jnp.zeros((8, D_FF), F32).at[:3].set(conv_w[0])
    act, w_down_b = _up_proj(h2, w_up_b, cw8, conv_b, w_down, seq)
    out = _down_proj(act, w_down_b, x1, mod, g_final[None, :], seq)
    return out.reshape(batch, seq, d)
```

```python
import functools
import math

import jax
import jax.numpy as jnp
import numpy as np
from jax import lax
from jax.experimental import pallas as pl
from jax.experimental.pallas import tpu as pltpu

F32 = jnp.float32
BF16 = jnp.bfloat16

D_MODEL = 2048
GRID_W = 64
NH_A = 4
DH_A = 256
W_A = NH_A * DH_A
NH_B = 8
DH_B = 128
W_B = NH_B * DH_B
WIN_ROWS = 8
WIN_COLS = 16
D_FF = 5632
N_MOD = 6
EPS = 1e-6
NEG_BIG = -1e30
LOG2_E = 1.4426950408889634

CHUNK_A = 256

TM_IN = 1024
TN_IN = 1024
TM_OUT = 512
ROW_CHUNK = 256
TM_UP = 1024
TN_UP = D_FF // 2
COL_CHUNK = 256
HALO = 16
TM_DOWN = 512
NA_GROUP = 8
NA_WINDOW = NA_GROUP + WIN_ROWS
NA_PAIR_ROWS = WIN_ROWS + 2
NA_PAIR_KEYS = NA_PAIR_ROWS * GRID_W
NA_LOOKAHEAD = 4

V7X_VMEM_BYTES = 64 * 1024 * 1024
VMEM_TEMP_ALLOWANCE = 12 * 1024 * 1024
DOWN_TEMP_ALLOWANCE = 6 * 1024 * 1024


def _vmem_limit(block_bytes, scratch_bytes=0):
    est = 2 * block_bytes + scratch_bytes + VMEM_TEMP_ALLOWANCE
    return int(min(est, V7X_VMEM_BYTES - 4 * 1024 * 1024))


def _nbytes(shape, dtype):
    n = 1
    for s in shape:
        n *= s
    return n * jnp.dtype(dtype).itemsize


def _params(sem, blocks, scratch=()):
    bb = sum(_nbytes(s, d) for s, d in blocks)
    sb = sum(_nbytes(s, d) for s, d in scratch)
    return pltpu.CompilerParams(dimension_semantics=sem, vmem_limit_bytes=_vmem_limit(bb, sb))


def _rms_mod(x, g, shift, scale):
    return x * lax.rsqrt(jnp.mean(x * x, axis=-1, keepdims=True) + EPS) * (g * (1.0 + scale)) + shift


TN_ADA = 1024


def _ada_kernel(ct_ref, w_ref, b_ref, o_ref, s_ref, *, batch):
    @pl.when(pl.program_id(0) == 0)
    def _():
        ct = ct_ref[...]
        s = ct * (1.0 / (1.0 + jnp.exp(-ct)))
        for b in range(batch):
            s_ref[b] = jnp.broadcast_to(s[:, b:b + 1], (D_MODEL, 128))

    row = lax.broadcasted_iota(jnp.int32, (8, 128), 0)
    for l in range(TN_ADA // 128):
        lanes = slice(l * 128, (l + 1) * 128)
        wl = w_ref[:, lanes]
        tile = jnp.zeros((8, 128), F32)
        for b in range(batch):
            tile = jnp.where(row == b, jnp.sum(wl * s_ref[b], axis=0, keepdims=True), tile)
        o_ref[:, lanes] = tile + b_ref[:, lanes]


def _ada(c, w_ada, b_ada):
    batch = c.shape[0]
    n = w_ada.shape[2]
    ct = jnp.zeros((D_MODEL, 8), F32).at[:, :batch].set(c.T)
    return pl.pallas_call(
        functools.partial(_ada_kernel, batch=batch),
        grid=(n // TN_ADA,),
        in_specs=[pl.BlockSpec((D_MODEL, 8), lambda j: (0, 0)),
                  pl.BlockSpec((None, D_MODEL, TN_ADA), lambda j: (0, 0, j)),
                  pl.BlockSpec((1, TN_ADA), lambda j: (0, j))],
        out_specs=pl.BlockSpec((8, TN_ADA), lambda j: (0, j)),
        out_shape=jax.ShapeDtypeStruct((8, n), F32),
        scratch_shapes=[pltpu.VMEM((batch, D_MODEL, 128), F32)],
        compiler_params=_params(("arbitrary",), [((D_MODEL, TN_ADA), F32), ((D_MODEL, 128), F32)],
                                [((batch, D_MODEL, 128), F32)]),
        name="ada",
    )(ct, w_ada, b_ada)


GATE_LO = 4 * W_A
GATE_HI = GATE_LO + 4 * NH_A
TM_PREP = 512
NBLK_A = 4
_NT = (((1,), (1,)), ((), ()))


def _win_prep_kernel(w_ref, o_ref):
    o_ref[...] = w_ref[...].astype(BF16)


def _win_prep(w_t):
    _, cols, d = w_t.shape
    ng = GATE_HI - GATE_LO
    n = cols - ng
    assert GATE_LO % TM_PREP == 0 and n % TM_PREP == 0 and TM_PREP % ng == 0

    def src_row(i):
        past = (i >= GATE_LO // TM_PREP).astype(jnp.int32)
        return (0, (i * (TM_PREP // ng) + past) * ng, 0)

    return pl.pallas_call(
        _win_prep_kernel,
        grid=(n // TM_PREP,),
        in_specs=[pl.BlockSpec((None, pl.Element(TM_PREP), pl.Element(d)), src_row)],
        out_specs=pl.BlockSpec((TM_PREP, d), lambda i: (i, 0)),
        out_shape=jax.ShapeDtypeStruct((n, d), BF16),
        compiler_params=_params(("parallel",), [((TM_PREP, d), F32), ((TM_PREP, d), BF16)]),
        name="w_in_prep",
    )(w_t)


def _in_kernel(x_ref, mod_ref, g_ref, w_ref, wg_ref, bg_ref, pa_ref, pb_ref, gates_ref, h_ref):
    j = pl.program_id(1)

    @pl.when(j == 0)
    def _():
        for c in range(TM_IN // ROW_CHUNK):
            rows = slice(c * ROW_CHUNK, (c + 1) * ROW_CHUNK)
            h = _rms_mod(x_ref[rows, :], g_ref[...], mod_ref[0, 0:1, :], mod_ref[0, 1:2, :])
            hb = h.astype(BF16)
            h_ref[rows, :] = hb
            pa_ref[rows, :] = lax.dot_general(hb, w_ref[...], _NT,
                                              preferred_element_type=F32).astype(BF16)
            gates_ref[rows, :] = lax.dot_general(hb, wg_ref[...], _NT,
                                                 preferred_element_type=F32) + bg_ref[...]

    @pl.when(jnp.logical_and(j > 0, j < NBLK_A))
    def _():
        pa_ref[...] = lax.dot_general(h_ref[...], w_ref[...], _NT,
                                      preferred_element_type=F32).astype(BF16)

    @pl.when(j >= NBLK_A)
    def _():
        r = lax.dot_general(h_ref[...], w_ref[...], _NT, preferred_element_type=F32)
        for head in range(NH_B):
            pb_ref[head] = r[:, head * DH_B:(head + 1) * DH_B].astype(BF16)


def _in_proj(x2, mod, g1, w_main, w_g, b_g, seq):
    t = x2.shape[0]
    n = w_main.shape[0]
    assert TN_IN == W_A == W_B and n == (NBLK_A + 3) * TN_IN
    tiles_per_seq = seq // TM_IN
    return pl.pallas_call(
        _in_kernel,
        grid=(t // TM_IN, n // TN_IN),
        in_specs=[pl.BlockSpec((TM_IN, D_MODEL), lambda i, j: (i, 0)),
                  pl.BlockSpec((1, N_MOD, D_MODEL), lambda i, j: (i // tiles_per_seq, 0, 0)),
                  pl.BlockSpec((1, D_MODEL), lambda i, j: (0, 0)),
                  pl.BlockSpec((TN_IN, D_MODEL), lambda i, j: (j, 0)),
                  pl.BlockSpec((256, D_MODEL), lambda i, j: (0, 0)),
                  pl.BlockSpec((1, 256), lambda i, j: (0, 0))],
        out_specs=[pl.BlockSpec((None, TM_IN, TN_IN), lambda i, j: (jnp.minimum(j, NBLK_A - 1), i, 0)),
                   pl.BlockSpec((NH_B, TM_IN, DH_B), lambda i, j: (jnp.maximum(j - NBLK_A, 0), i, 0)),
                   pl.BlockSpec((TM_IN, 256), lambda i, j: (i, 0))],
        out_shape=[jax.ShapeDtypeStruct((NBLK_A, t, TN_IN), BF16),
                   jax.ShapeDtypeStruct((3 * NH_B, t, DH_B), BF16),
                   jax.ShapeDtypeStruct((t, 256), F32)],
        scratch_shapes=[pltpu.VMEM((TM_IN, D_MODEL), BF16)],
        compiler_params=_params(
            ("parallel", "arbitrary"),
            [((TM_IN, D_MODEL), F32), ((D_MODEL, TN_IN), BF16), ((D_MODEL, 256), BF16),
             ((TM_IN, TN_IN), BF16), ((TM_IN, TN_IN), BF16), ((TM_IN, 256), F32)],
            [((TM_IN, D_MODEL), BF16)]),
        name="in_proj",
    )(x2, mod, g1, w_main, w_g, b_g)


def _seg_scan(x, row, op, fill, reverse):
    n = x.shape[0]
    d = 1
    while d < n:
        if reverse:
            y = pltpu.roll(x, n - d, axis=0)
            x = op(x, jnp.where(row < n - d, y, fill))
        else:
            y = pltpu.roll(x, d, axis=0)
            x = op(x, jnp.where(row >= d, y, fill))
        d *= 2
    return x


def _gate_vectors(g_ref, m_ref, reverse):
    n = CHUNK_A
    gi = g_ref[:, 0:128]
    gf = g_ref[:, 128:256]
    lf = jnp.minimum(gf, 0.0) - jnp.log(1.0 + jnp.exp(-jnp.abs(gf)))
    row = lax.broadcasted_iota(jnp.int32, (n, 128), 0)
    bc = _seg_scan(lf, row, jnp.add, 0.0, reverse)
    rb = gi - bc
    cm = _seg_scan(rb, row, jnp.maximum, -jnp.inf, reverse)
    last = 0 if reverse else n - 1
    gsum = bc[last:last + 1, :]
    m_loc = gsum + cm[last:last + 1, :]
    m = m_ref[...]
    m_inter = bc + m
    m_t = jnp.maximum(m_inter, bc + cm)
    m_new = jnp.maximum(gsum + m, m_loc)
    m_ref[...] = m_new
    col_a = bc - m_t - math.log(DH_A ** 0.5)
    a = jnp.exp(m_inter - m_t)
    e = jnp.exp(-m_t)
    w = jnp.exp(gsum + rb - m_new) * (DH_A ** -0.5)
    dec = jnp.exp(gsum + m - m_new)
    return col_a, a, e, w, dec, rb.T


def _cast_slab_specs(w, steps, imap):
    _, rows, cols = w.shape
    slab = rows // steps
    assert slab * steps == rows and slab % 16 == 0
    in_spec = pl.BlockSpec((None, slab, cols), lambda *ids: (0,) + tuple(imap(*ids)))
    out_spec = pl.BlockSpec((slab, cols), imap)
    return in_spec, out_spec, jax.ShapeDtypeStruct((rows, cols), BF16), ((slab, cols), F32)


def _mlstm_kernel(qf, kf, vf, qb, kb, vb, gf, gb, wsrc, hf, hb, wdst, ct_ref, mf_ref, mb_ref):
    @pl.when(pl.program_id(1) == 0)
    def _():
        ct_ref[...] = jnp.zeros(ct_ref.shape, F32)
        mf_ref[...] = jnp.full((1, 128), NEG_BIG, F32)
        mb_ref[...] = jnp.full((1, 128), NEG_BIG, F32)

    wdst[...] = wsrc[...].astype(BF16)

    n = CHUNK_A
    r = lax.broadcasted_iota(jnp.int32, (n, n), 0)
    c = lax.broadcasted_iota(jnp.int32, (n, n), 1)
    ones_cols = jnp.ones((n, 128), BF16)
    ones_rows = jnp.ones((128, n), BF16)
    seqs = []
    for head in range(NH_A):
        seqs.append((qf, kf, vf, False, hf, head, head, r >= c))
        seqs.append((qb, kb, vb, True, hb, head, NH_A + head, r <= c))

    loaded = []
    for q_ref, k_ref, v_ref, _, _, head, sid, _ in seqs:
        sl = slice(head * DH_A, (head + 1) * DH_A)
        q = q_ref[:, sl]
        k = k_ref[:, sl]
        v = v_ref[:, sl]
        ct_old = ct_ref[sid]
        qk = lax.dot_general(q, k, _NT, preferred_element_type=F32)
        qc = lax.dot_general(q, ct_old.astype(BF16), _NT, preferred_element_type=F32)
        loaded.append((k, v, ct_old, qk, qc))

    gate_vecs = {False: _gate_vectors(gf, mf_ref, reverse=False),
                 True: _gate_vectors(gb, mb_ref, reverse=True)}
    for (_, _, _, rev, h_ref, head, sid, mask), (k, v, ct_old, qk, qc) in zip(seqs, loaded):
        gates = gate_vecs[rev]
        sl = slice(head * DH_A, (head + 1) * DH_A)
        col_a, a, e = (g[:, sid:sid + 1] for g in gates[:3])
        row_b = gates[5][sid:sid + 1, :]
        s = qk * jnp.exp(jnp.where(mask, col_a + row_b, -jnp.inf))
        v_ext = jnp.concatenate([v, ones_cols], axis=1)
        sv = jnp.dot(s.astype(BF16), v_ext, preferred_element_type=F32) + a * qc
        inv = 1.0 / jnp.maximum(jnp.abs(sv[:, DH_A:]), e)
        h_ref[:, sl] = (sv[:, :DH_A] * jnp.concatenate([inv, inv], axis=1)).astype(BF16)

    for (_, _, _, rev, _, _, sid, _), (k, v, ct_old, _, _) in zip(seqs, loaded):
        gates = gate_vecs[rev]
        w = gates[3][:, sid:sid + 1]
        dec = gates[4][:, sid:sid + 1]
        kw = (w * k.astype(F32)).astype(BF16)
        vt_ext = jnp.concatenate([v.T, ones_rows], axis=0)
        ct_ref[sid] = dec * ct_old + jnp.dot(vt_ext, kw, preferred_element_type=F32)


def _mlstm(p, gates, w_cast, batch, seq):
    nc = seq // CHUNK_A
    t = p.shape[1]
    cast_in, cast_out, cast_shape, cast_blk = _cast_slab_specs(w_cast, batch * nc,
                                                               lambda b, k: (b * nc + k, 0))

    def blk(col, rev):
        if rev:
            return pl.BlockSpec((None, CHUNK_A, W_A), lambda b, k: (col, b * nc + nc - 1 - k, 0))
        return pl.BlockSpec((None, CHUNK_A, W_A), lambda b, k: (col, b * nc + k, 0))

    def outblk(rev):
        if rev:
            return pl.BlockSpec((CHUNK_A, W_A), lambda b, k: (b * nc + nc - 1 - k, 0))
        return pl.BlockSpec((CHUNK_A, W_A), lambda b, k: (b * nc + k, 0))

    gatespec = lambda rev: pl.BlockSpec(
        (CHUNK_A, 256), (lambda b, k: (b * nc + nc - 1 - k, 0)) if rev else (lambda b, k: (b * nc + k, 0)))
    return pl.pallas_call(
        _mlstm_kernel,
        grid=(batch, nc),
        in_specs=[blk(0, False), blk(1, False), blk(2, False),
                  blk(0, True), blk(1, True), blk(2, True),
                  gatespec(False), gatespec(True), cast_in],
        out_specs=[outblk(False), outblk(True), cast_out],
        out_shape=[jax.ShapeDtypeStruct((t, W_A), BF16), jax.ShapeDtypeStruct((t, W_A), BF16),
                   cast_shape],
        scratch_shapes=[pltpu.VMEM((2 * NH_A, DH_A + 128, DH_A), F32),
                        pltpu.VMEM((1, 128), F32), pltpu.VMEM((1, 128), F32)],
        compiler_params=_params(
            ("parallel", "arbitrary"),
            [((CHUNK_A, W_A), BF16)] * 8 + [((CHUNK_A, 256), F32)] * 2 + [cast_blk, cast_blk],
            [((2 * NH_A, DH_A + 128, DH_A), F32)]),
        name="mlstm",
    )(p, p, p, p, p, p, gates, gates, w_cast)


def _na_window_start(g, rows):
    return jnp.clip(g * NA_GROUP - WIN_ROWS // 2, 0, rows - NA_WINDOW)


def _na_pair_geometry(r, rows):
    clip = (lambda v, lo, hi: max(lo, min(v, hi))) if isinstance(r, int) else jnp.clip
    rs = [clip(r + e - WIN_ROWS // 2, 0, rows - WIN_ROWS) for e in range(2)]
    ws = clip(rs[0], 0, rows - NA_PAIR_ROWS)
    return ws, [(rs[e] - ws, ws - (r + e) + WIN_ROWS - 1) for e in range(2)]


def _na_pair_variants(rows):
    half = WIN_ROWS // 2
    return [0, 2, half, rows - half, rows - half + 2]


def _na_pair_variant_index(r, rows):
    half = WIN_ROWS // 2
    top = r // 2
    bottom = 3 + (r - (rows - half)) // 2
    return jnp.where(r < half, top, jnp.where(r >= rows - half, bottom, 2))


def _natten_kernel(q_ref, kbuf, vbuf, bias_ref, wsrc, o_ref, wdst, *, rows):
    wdst[...] = wsrc[...].astype(BF16)
    g = pl.program_id(1)
    wstart = _na_window_start(g, rows)
    scale = DH_B ** -0.5
    pair_q = 2 * GRID_W
    starts, variants = [], []
    for i in range(NA_GROUP // 2):
        r = g * NA_GROUP + 2 * i
        ws, _ = _na_pair_geometry(r, rows)
        starts.append(pl.multiple_of((ws - wstart) * GRID_W, GRID_W))
        variants.append(_na_pair_variant_index(r, rows))

    def scores(i, h):
        q = q_ref[h, i * pair_q:(i + 1) * pair_q, :]
        kw = kbuf[h, pl.ds(starts[i], NA_PAIR_KEYS), :]
        s = lax.dot_general(q, kw, _NT, preferred_element_type=F32)
        bias = jnp.concatenate([bias_ref[variants[i], h, l] for l in range(NA_PAIR_KEYS // 128)], axis=1)
        s = s * (scale * LOG2_E) + bias
        p = jnp.exp2(s - jnp.max(s, axis=-1, keepdims=True))
        return p.astype(BF16), 1.0 / jnp.sum(p, axis=-1, keepdims=True)

    def weighted_sum(i, h, p, inv_l):
        hs = slice(h * DH_B, (h + 1) * DH_B)
        vw = vbuf[h, pl.ds(starts[i], NA_PAIR_KEYS), :]
        o = jnp.dot(p, vw, preferred_element_type=F32)
        o_ref[i * pair_q:(i + 1) * pair_q, hs] = (o * inv_l).astype(BF16)

    tiles = [(i, h) for i in range(NA_GROUP // 2) for h in range(NH_B)]
    pending = []
    for idx in range(len(tiles) + NA_LOOKAHEAD):
        if idx < len(tiles):
            pending.append(scores(*tiles[idx]))
        if idx >= NA_LOOKAHEAD:
            weighted_sum(*tiles[idx - NA_LOOKAHEAD], *pending[idx - NA_LOOKAHEAD])
            pending[idx - NA_LOOKAHEAD] = None


def _natten(p, bias, w_cast, batch, seq):
    rows = seq // GRID_W
    ng = rows // NA_GROUP
    blk = NA_GROUP * GRID_W
    win = NA_WINDOW * GRID_W
    t = p.shape[1]
    qpart, kpart, vpart = 0, 1, 2

    def window(part):
        def imap(b, g):
            return (part * NH_B, (b * rows + _na_window_start(g, rows)) * GRID_W, 0)
        return pl.BlockSpec((pl.Element(NH_B), pl.Element(win), pl.Element(DH_B)), imap)

    step = lambda b, g: (b * ng + g, 0)
    c_in, c_out, c_shape, c_blk = _cast_slab_specs(w_cast, batch * ng, step)
    return pl.pallas_call(
        functools.partial(_natten_kernel, rows=rows),
        grid=(batch, ng),
        in_specs=[pl.BlockSpec((NH_B, blk, DH_B), lambda b, g: (qpart, b * ng + g, 0)),
                  window(kpart), window(vpart),
                  pl.BlockSpec(bias.shape, lambda b, g: (0, 0, 0, 0, 0), pipeline_mode=pl.Buffered(1)),
                  c_in],
        out_specs=[pl.BlockSpec((blk, W_B), step), c_out],
        out_shape=[jax.ShapeDtypeStruct((t, W_B), BF16), c_shape],
        compiler_params=_params(
            ("parallel", "arbitrary"),
            [((blk, W_B), BF16)] * 2 + [((win, W_B), BF16)] * 2 + [(bias.shape, F32)]
            + [c_blk, c_blk]),
        name="natten",
    )(p, p, p, bias, w_cast)


def _natten_bias(rpb, rows):
    cols = np.arange(GRID_W)
    cstart = np.clip(cols - WIN_COLS // 2, 0, GRID_W - WIN_COLS)
    inside = (cols[None, :] >= cstart[:, None]) & (cols[None, :] < cstart[:, None] + WIN_COLS)
    dc = cols[None, :] - cols[:, None] + (WIN_COLS - 1)
    onehot = (np.arange(2 * WIN_COLS - 1)[:, None, None] == dc[None]) & inside[None]
    toep = jnp.einsum('hdj,jck->hdck', rpb.astype(F32), onehot.astype(np.float32),
                      precision=lax.Precision.HIGHEST)
    toep = toep * LOG2_E + np.where(inside, 0.0, NEG_BIG).astype(np.float32)
    nvar = len(_na_pair_variants(rows))
    ntiles = NA_PAIR_KEYS // 128
    return pl.pallas_call(
        functools.partial(_bias_tiles_kernel, rows=rows),
        grid=(NH_B,),
        in_specs=[pl.BlockSpec((None, 2 * WIN_ROWS - 1, GRID_W, GRID_W), lambda h: (h, 0, 0, 0))],
        out_specs=pl.BlockSpec((nvar, None, ntiles, 2 * GRID_W, 128), lambda h: (0, h, 0, 0, 0)),
        out_shape=jax.ShapeDtypeStruct((nvar, NH_B, ntiles, 2 * GRID_W, 128), F32),
        compiler_params=_params(("parallel",), [((2 * WIN_ROWS - 1, GRID_W, 128), F32),
                                                ((nvar, ntiles, 2 * GRID_W, 128), F32)]),
        name="bias_tiles",
    )(toep)


def _bias_tiles_kernel(t_ref, o_ref, *, rows):
    rows_per_tile = 128 // GRID_W
    masked = jnp.full((GRID_W, GRID_W), NEG_BIG, F32)
    for v, r in enumerate(_na_pair_variants(rows)):
        _, geometry = _na_pair_geometry(r, rows)
        for l in range(NA_PAIR_KEYS // 128):
            per_row = []
            for kr_lo, dr0 in geometry:
                pieces = []
                for kr in range(l * rows_per_tile, (l + 1) * rows_per_tile):
                    inside = kr_lo <= kr < kr_lo + WIN_ROWS
                    pieces.append(t_ref[kr + dr0] if inside else masked)
                per_row.append(jnp.concatenate(pieces, axis=1))
            o_ref[v, l] = jnp.concatenate(per_row, axis=0)


def _out_kernel(hf_ref, hb_ref, oa_ref, hbt_ref, x_ref, mod_ref, gh_ref, gn_ref, w_ref,
                x1_ref, h2_ref):
    for c in range(TM_OUT // ROW_CHUNK):
        rows = slice(c * ROW_CHUNK, (c + 1) * ROW_CHUNK)
        hs = hf_ref[rows, :].astype(F32) + hb_ref[rows, :].astype(F32)
        parts = []
        for head in range(NH_A):
            hh = hs[:, head * DH_A:(head + 1) * DH_A]
            parts.append(hh * lax.rsqrt(jnp.mean(hh * hh, axis=-1, keepdims=True) + EPS))
        hn = jnp.concatenate(parts, axis=-1) * gh_ref[...]
        oa = oa_ref[rows, :].astype(F32)
        ha = (hn * (1.0 / (1.0 + jnp.exp(-oa)))).astype(BF16)
        mixed = jnp.dot(jnp.concatenate([ha, hbt_ref[rows, :]], axis=1), w_ref[...],
                        preferred_element_type=F32)
        x1 = x_ref[rows, :] + mod_ref[0, 2:3, :] * mixed
        x1_ref[rows, :] = x1
        h2_ref[rows, :] = _rms_mod(x1, gn_ref[...], mod_ref[0, 3:4, :], mod_ref[0, 4:5, :]).astype(BF16)


def _out_proj(hf, hb, pa, hbt, x2, mod, gh, g2, w_o, seq):
    t = x2.shape[0]
    tiles_per_seq = seq // TM_OUT
    row = lambda i: (i, 0)
    resident = pl.Buffered(1)
    return pl.pallas_call(
        _out_kernel,
        grid=(t // TM_OUT,),
        in_specs=[pl.BlockSpec((TM_OUT, W_A), row), pl.BlockSpec((TM_OUT, W_A), row),
                  pl.BlockSpec((None, TM_OUT, W_A), lambda i: (3, i, 0)),
                  pl.BlockSpec((TM_OUT, W_B), row),
                  pl.BlockSpec((TM_OUT, D_MODEL), row),
                  pl.BlockSpec((1, N_MOD, D_MODEL), lambda i: (i // tiles_per_seq, 0, 0)),
                  pl.BlockSpec((1, W_A), lambda i: (0, 0)),
                  pl.BlockSpec((1, D_MODEL), lambda i: (0, 0)),
                  pl.BlockSpec((W_A + W_B, D_MODEL), lambda i: (0, 0), pipeline_mode=resident)],
        out_specs=[pl.BlockSpec((TM_OUT, D_MODEL), row), pl.BlockSpec((TM_OUT, D_MODEL), row)],
        out_shape=[jax.ShapeDtypeStruct((t, D_MODEL), F32), jax.ShapeDtypeStruct((t, D_MODEL), BF16)],
        compiler_params=_params(
            ("parallel",),
            [((TM_OUT, W_A), BF16)] * 4 + [((TM_OUT, D_MODEL), F32)] * 2
            + [((TM_OUT, D_MODEL), BF16)] + [((W_A, D_MODEL), BF16)]),
        name="out_proj",
    )(hf, hb, pa, hbt, x2, mod, gh, g2, w_o)


def _up_kernel(hp_ref, h_ref, hn_ref, wu_ref, wg_ref, cw_ref, cb_ref, wsrc, act_ref, wdst, hext_ref, *,
               tiles_per_seq):
    wdst[...] = wsrc[...].astype(BF16)
    pos = pl.program_id(1) % tiles_per_seq
    n = TM_UP + 2 * HALO
    hext_ref[0:HALO, :] = jnp.where(pos > 0, hp_ref[...], jnp.zeros_like(hp_ref))
    hext_ref[HALO:HALO + TM_UP, :] = h_ref[...]
    hext_ref[HALO + TM_UP:, :] = jnp.where(pos < tiles_per_seq - 1, hn_ref[...], jnp.zeros_like(hn_ref))
    for c in range(TN_UP // COL_CHUNK):
        cols = slice(c * COL_CHUNK, (c + 1) * COL_CHUNK)
        u = jnp.dot(h_ref[...], wu_ref[:, cols], preferred_element_type=F32)
        g = jnp.dot(hext_ref[...], wg_ref[:, cols], preferred_element_type=F32)
        g_prev = pltpu.roll(g, 1, axis=0)[HALO:HALO + TM_UP, :]
        g_next = pltpu.roll(g, n - 1, axis=0)[HALO:HALO + TM_UP, :]
        gc = (g_prev * cw_ref[0:1, cols] + g[HALO:HALO + TM_UP, :] * cw_ref[1:2, cols]
              + g_next * cw_ref[2:3, cols] + cb_ref[:, cols])
        gelu = 0.5 * gc * (1.0 + lax.erf(gc * (2.0 ** -0.5)))
        act_ref[:, cols] = (gelu * u).astype(BF16)


def _up_proj(h2, w_up, conv_w, conv_b, w_cast, seq):
    t = h2.shape[0]
    tiles_per_seq = seq // TM_UP
    hb = TM_UP // HALO
    nh = t // HALO
    ngroups = D_FF // TN_UP
    ntiles = t // TM_UP
    resident = pl.Buffered(1)
    c_in, c_out, c_shape, c_blk = _cast_slab_specs(w_cast, ngroups * ntiles,
                                                   lambda j, i: (j * ntiles + i, 0))
    return pl.pallas_call(
        functools.partial(_up_kernel, tiles_per_seq=tiles_per_seq),
        grid=(ngroups, t // TM_UP),
        in_specs=[pl.BlockSpec((HALO, D_MODEL), lambda j, i: (jnp.maximum(i * hb - 1, 0), 0)),
                  pl.BlockSpec((TM_UP, D_MODEL), lambda j, i: (i, 0)),
                  pl.BlockSpec((HALO, D_MODEL), lambda j, i: (jnp.minimum((i + 1) * hb, nh - 1), 0)),
                  pl.BlockSpec((D_MODEL, TN_UP), lambda j, i: (0, j), pipeline_mode=resident),
                  pl.BlockSpec((D_MODEL, TN_UP), lambda j, i: (0, ngroups + j), pipeline_mode=resident),
                  pl.BlockSpec((8, TN_UP), lambda j, i: (0, j)),
                  pl.BlockSpec((1, TN_UP), lambda j, i: (0, j)), c_in],
        out_specs=[pl.BlockSpec((TM_UP, TN_UP), lambda j, i: (i, j)), c_out],
        out_shape=[jax.ShapeDtypeStruct((t, D_FF), BF16), c_shape],
        scratch_shapes=[pltpu.VMEM((TM_UP + 2 * HALO, D_MODEL), BF16)],
        compiler_params=_params(
            ("arbitrary", "arbitrary"),
            [((TM_UP + 2 * HALO, D_MODEL), BF16), ((D_MODEL, TN_UP), BF16), ((TM_UP, TN_UP), BF16),
             c_blk, c_blk],
            [((TM_UP + 2 * HALO, D_MODEL), BF16)]),
        name="up_proj",
    )(h2, h2, h2, w_up, w_up, conv_w, conv_b, w_cast)


def _down_kernel(a_ref, w_ref, x_ref, mod_ref, g_ref, o_ref):
    for c in range(TM_DOWN // ROW_CHUNK):
        rows = slice(c * ROW_CHUNK, (c + 1) * ROW_CHUNK)
        y = x_ref[rows, :] + mod_ref[0, 5:6, :] * jnp.dot(a_ref[rows, :], w_ref[...],
                                                           preferred_element_type=F32)
        o_ref[rows, :] = y * lax.rsqrt(jnp.mean(y * y, axis=-1, keepdims=True) + EPS) * g_ref[...]


def _down_proj(act, w_d, x1, mod, gfin, seq):
    t = x1.shape[0]
    tiles_per_seq = seq // TM_DOWN
    return pl.pallas_call(
        _down_kernel,
        grid=(t // TM_DOWN,),
        in_specs=[pl.BlockSpec((TM_DOWN, D_FF), lambda i: (i, 0)),
                  pl.BlockSpec((D_FF, D_MODEL), lambda i: (0, 0), pipeline_mode=pl.Buffered(1)),
                  pl.BlockSpec((TM_DOWN, D_MODEL), lambda i: (i, 0)),
                  pl.BlockSpec((1, N_MOD, D_MODEL), lambda i: (i // tiles_per_seq, 0, 0)),
                  pl.BlockSpec((1, D_MODEL), lambda i: (0, 0))],
        out_specs=pl.BlockSpec((TM_DOWN, D_MODEL), lambda i: (i, 0)),
        out_shape=jax.ShapeDtypeStruct((t, D_MODEL), F32),
        compiler_params=pltpu.CompilerParams(
            dimension_semantics=("parallel",),
            vmem_limit_bytes=_nbytes((D_FF, D_MODEL), BF16) + 2 * (
                _nbytes((TM_DOWN, D_FF), BF16) + 2 * _nbytes((TM_DOWN, D_MODEL), F32))
            + DOWN_TEMP_ALLOWANCE),
        name="down_proj",
    )(act, w_d, x1, mod, gfin)


def kernel(x, c, w_ada, b_ada, g_norm1, w_in, b_gates, g_head_a, rpb, w_out, g_norm2, w_up, conv_w,
           conv_b, w_down, g_final):
    batch, seq, d = x.shape
    assert w_ada.shape[0] == 1
    assert d == D_MODEL and seq % CHUNK_A == 0 and seq % (NA_GROUP * GRID_W) == 0
    assert seq // GRID_W >= NA_WINDOW and NA_GROUP % 2 == 0
    x2 = x.reshape(batch * seq, d)
    mod = _ada(c, w_ada, b_ada)[:batch].reshape(batch, N_MOD, d)

    w_in_t = jnp.swapaxes(w_in, 1, 2)
    w_main = _win_prep(w_in_t)
    lane_of = np.concatenate([np.arange(0, 4), np.arange(128, 132), np.arange(4, 8), np.arange(132, 136)])
    sel = np.zeros((4 * NH_A, 256), np.float32)
    sel[np.arange(4 * NH_A), lane_of] = 1.0
    w_g = jnp.dot(sel.T, w_in_t[0, GATE_LO:GATE_HI], precision=lax.Precision.HIGHEST).astype(BF16)
    b_g = jnp.dot(b_gates, sel, precision=lax.Precision.HIGHEST)

    pa, pb, gates = _in_proj(x2, mod, g_norm1, w_main, w_g, b_g, seq)
    hf, hb, w_up_b = _mlstm(pa, gates, w_up, batch, seq)
    hbt, w_out_b = _natten(pb, _natten_bias(rpb[0], seq // GRID_W), w_out, batch, seq)
    x1, h2 = _out_proj(hf, hb, pa, hbt, x2, mod, g_head_a, g_norm2, w_out_b, seq)

    cw8 = jnp.zeros((8, D_FF), F32).at[:3].set(conv_w[0])
    act, w_down_b = _up_proj(h2, w_up_b, cw8, conv_b, w_down, seq)
    out = _down_proj(act, w_down_b, x1, mod, g_final[None, :], seq)
    return out.reshape(batch, seq, d)
```

```python
import functools
import math

import jax
import jax.numpy as jnp
import numpy as np
from jax import lax
from jax.experimental import pallas as pl
from jax.experimental.pallas import tpu as pltpu

F32 = jnp.float32
BF16 = jnp.bfloat16

D_MODEL = 2048
GRID_W = 64
NH_A = 4
DH_A = 256
W_A = NH_A * DH_A
NH_B = 8
DH_B = 128
W_B = NH_B * DH_B
WIN_ROWS = 8
WIN_COLS = 16
D_FF = 5632
N_MOD = 6
EPS = 1e-6
NEG_BIG = -1e30
LOG2_E = 1.4426950408889634

CHUNK_A = 256

TM_IN = 1024
TN_IN = 1024
TM_OUT = 512
ROW_CHUNK = 256
TM_UP = 1024
TN_UP = D_FF // 2
COL_CHUNK = 256
HALO = 16
TM_DOWN = 512
NA_GROUP = 8
NA_WINDOW = NA_GROUP + WIN_ROWS
NA_KEYS = WIN_ROWS * GRID_W
NA_LOOKAHEAD = 4

V7X_VMEM_BYTES = 64 * 1024 * 1024
VMEM_TEMP_ALLOWANCE = 12 * 1024 * 1024
DOWN_TEMP_ALLOWANCE = 6 * 1024 * 1024


def _vmem_limit(block_bytes, scratch_bytes=0):
    est = 2 * block_bytes + scratch_bytes + VMEM_TEMP_ALLOWANCE
    return int(min(est, V7X_VMEM_BYTES - 4 * 1024 * 1024))


def _nbytes(shape, dtype):
    n = 1
    for s in shape:
        n *= s
    return n * jnp.dtype(dtype).itemsize


def _params(sem, blocks, scratch=()):
    bb = sum(_nbytes(s, d) for s, d in blocks)
    sb = sum(_nbytes(s, d) for s, d in scratch)
    return pltpu.CompilerParams(dimension_semantics=sem, vmem_limit_bytes=_vmem_limit(bb, sb))


def _rms_mod(x, g, shift, scale):
    return x * lax.rsqrt(jnp.mean(x * x, axis=-1, keepdims=True) + EPS) * (g * (1.0 + scale)) + shift


TN_ADA = 1024


def _ada_kernel(ct_ref, w_ref, b_ref, o_ref, s_ref, *, batch):
    @pl.when(pl.program_id(0) == 0)
    def _():
        ct = ct_ref[...]
        s = ct * (1.0 / (1.0 + jnp.exp(-ct)))
        for b in range(batch):
            s_ref[b] = jnp.broadcast_to(s[:, b:b + 1], (D_MODEL, 128))

    row = lax.broadcasted_iota(jnp.int32, (8, 128), 0)
    for l in range(TN_ADA // 128):
        lanes = slice(l * 128, (l + 1) * 128)
        wl = w_ref[:, lanes]
        tile = jnp.zeros((8, 128), F32)
        for b in range(batch):
            tile = jnp.where(row == b, jnp.sum(wl * s_ref[b], axis=0, keepdims=True), tile)
        o_ref[:, lanes] = tile + b_ref[:, lanes]


def _ada(c, w_ada, b_ada):
    batch = c.shape[0]
    n = w_ada.shape[2]
    ct = jnp.zeros((D_MODEL, 8), F32).at[:, :batch].set(c.T)
    return pl.pallas_call(
        functools.partial(_ada_kernel, batch=batch),
        grid=(n // TN_ADA,),
        in_specs=[pl.BlockSpec((D_MODEL, 8), lambda j: (0, 0)),
                  pl.BlockSpec((None, D_MODEL, TN_ADA), lambda j: (0, 0, j)),
                  pl.BlockSpec((1, TN_ADA), lambda j: (0, j))],
        out_specs=pl.BlockSpec((8, TN_ADA), lambda j: (0, j)),
        out_shape=jax.ShapeDtypeStruct((8, n), F32),
        scratch_shapes=[pltpu.VMEM((batch, D_MODEL, 128), F32)],
        compiler_params=_params(("arbitrary",), [((D_MODEL, TN_ADA), F32), ((D_MODEL, 128), F32)],
                                [((batch, D_MODEL, 128), F32)]),
        name="ada",
    )(ct, w_ada, b_ada)


GATE_LO = 4 * W_A
GATE_HI = GATE_LO + 4 * NH_A
TM_PREP = 512
NBLK_A = 4
_NT = (((1,), (1,)), ((), ()))


def _win_prep_kernel(w_ref, o_ref):
    o_ref[...] = w_ref[...].astype(BF16)


def _win_prep(w_t):
    _, cols, d = w_t.shape
    ng = GATE_HI - GATE_LO
    n = cols - ng
    assert GATE_LO % TM_PREP == 0 and n % TM_PREP == 0 and TM_PREP % ng == 0

    def src_row(i):
        past = (i >= GATE_LO // TM_PREP).astype(jnp.int32)
        return (0, (i * (TM_PREP // ng) + past) * ng, 0)

    return pl.pallas_call(
        _win_prep_kernel,
        grid=(n // TM_PREP,),
        in_specs=[pl.BlockSpec((None, pl.Element(TM_PREP), pl.Element(d)), src_row)],
        out_specs=pl.BlockSpec((TM_PREP, d), lambda i: (i, 0)),
        out_shape=jax.ShapeDtypeStruct((n, d), BF16),
        compiler_params=_params(("parallel",), [((TM_PREP, d), F32), ((TM_PREP, d), BF16)]),
        name="w_in_prep",
    )(w_t)


def _in_kernel(x_ref, mod_ref, g_ref, w_ref, wg_ref, bg_ref, pa_ref, pb_ref, gates_ref, h_ref):
    j = pl.program_id(1)

    @pl.when(j == 0)
    def _():
        for c in range(TM_IN // ROW_CHUNK):
            rows = slice(c * ROW_CHUNK, (c + 1) * ROW_CHUNK)
            h = _rms_mod(x_ref[rows, :], g_ref[...], mod_ref[0, 0:1, :], mod_ref[0, 1:2, :])
            hb = h.astype(BF16)
            h_ref[rows, :] = hb
            pa_ref[rows, :] = lax.dot_general(hb, w_ref[...], _NT,
                                              preferred_element_type=F32).astype(BF16)
            gates_ref[rows, :] = lax.dot_general(hb, wg_ref[...], _NT,
                                                 preferred_element_type=F32) + bg_ref[...]

    @pl.when(jnp.logical_and(j > 0, j < NBLK_A))
    def _():
        pa_ref[...] = lax.dot_general(h_ref[...], w_ref[...], _NT,
                                      preferred_element_type=F32).astype(BF16)

    @pl.when(j >= NBLK_A)
    def _():
        r = lax.dot_general(h_ref[...], w_ref[...], _NT, preferred_element_type=F32)
        for head in range(NH_B):
            pb_ref[head] = r[:, head * DH_B:(head + 1) * DH_B].astype(BF16)


def _in_proj(x2, mod, g1, w_main, w_g, b_g, seq):
    t = x2.shape[0]
    n = w_main.shape[0]
    assert TN_IN == W_A == W_B and n == (NBLK_A + 3) * TN_IN
    tiles_per_seq = seq // TM_IN
    return pl.pallas_call(
        _in_kernel,
        grid=(t // TM_IN, n // TN_IN),
        in_specs=[pl.BlockSpec((TM_IN, D_MODEL), lambda i, j: (i, 0)),
                  pl.BlockSpec((1, N_MOD, D_MODEL), lambda i, j: (i // tiles_per_seq, 0, 0)),
                  pl.BlockSpec((1, D_MODEL), lambda i, j: (0, 0)),
                  pl.BlockSpec((TN_IN, D_MODEL), lambda i, j: (j, 0)),
                  pl.BlockSpec((256, D_MODEL), lambda i, j: (0, 0)),
                  pl.BlockSpec((1, 256), lambda i, j: (0, 0))],
        out_specs=[pl.BlockSpec((None, TM_IN, TN_IN), lambda i, j: (jnp.minimum(j, NBLK_A - 1), i, 0)),
                   pl.BlockSpec((NH_B, TM_IN, DH_B), lambda i, j: (jnp.maximum(j - NBLK_A, 0), i, 0)),
                   pl.BlockSpec((TM_IN, 256), lambda i, j: (i, 0))],
        out_shape=[jax.ShapeDtypeStruct((NBLK_A, t, TN_IN), BF16),
                   jax.ShapeDtypeStruct((3 * NH_B, t, DH_B), BF16),
                   jax.ShapeDtypeStruct((t, 256), F32)],
        scratch_shapes=[pltpu.VMEM((TM_IN, D_MODEL), BF16)],
        compiler_params=_params(
            ("parallel", "arbitrary"),
            [((TM_IN, D_MODEL), F32), ((D_MODEL, TN_IN), BF16), ((D_MODEL, 256), BF16),
             ((TM_IN, TN_IN), BF16), ((TM_IN, TN_IN), BF16), ((TM_IN, 256), F32)],
            [((TM_IN, D_MODEL), BF16)]),
        name="in_proj",
    )(x2, mod, g1, w_main, w_g, b_g)


def _seg_scan(x, row, op, fill, reverse):
    n = x.shape[0]
    d = 1
    while d < n:
        if reverse:
            y = pltpu.roll(x, n - d, axis=0)
            x = op(x, jnp.where(row < n - d, y, fill))
        else:
            y = pltpu.roll(x, d, axis=0)
            x = op(x, jnp.where(row >= d, y, fill))
        d *= 2
    return x


def _gate_vectors(g_ref, m_ref, reverse):
    n = CHUNK_A
    gi = g_ref[:, 0:128]
    gf = g_ref[:, 128:256]
    lf = jnp.minimum(gf, 0.0) - jnp.log(1.0 + jnp.exp(-jnp.abs(gf)))
    row = lax.broadcasted_iota(jnp.int32, (n, 128), 0)
    bc = _seg_scan(lf, row, jnp.add, 0.0, reverse)
    rb = gi - bc
    cm = _seg_scan(rb, row, jnp.maximum, -jnp.inf, reverse)
    last = 0 if reverse else n - 1
    gsum = bc[last:last + 1, :]
    m_loc = gsum + cm[last:last + 1, :]
    m = m_ref[...]
    m_inter = bc + m
    m_t = jnp.maximum(m_inter, bc + cm)
    m_new = jnp.maximum(gsum + m, m_loc)
    m_ref[...] = m_new
    col_a = bc - m_t - math.log(DH_A ** 0.5)
    a = jnp.exp(m_inter - m_t)
    e = jnp.exp(-m_t)
    w = jnp.exp(gsum + rb - m_new) * (DH_A ** -0.5)
    dec = jnp.exp(gsum + m - m_new)
    return col_a, a, e, w, dec, rb.T


def _cast_slab_specs(w, steps, imap):
    _, rows, cols = w.shape
    slab = rows // steps
    assert slab * steps == rows and slab % 16 == 0
    in_spec = pl.BlockSpec((None, slab, cols), lambda *ids: (0,) + tuple(imap(*ids)))
    out_spec = pl.BlockSpec((slab, cols), imap)
    return in_spec, out_spec, jax.ShapeDtypeStruct((rows, cols), BF16), ((slab, cols), F32)


def _mlstm_kernel(qf, kf, vf, qb, kb, vb, gf, gb, wsrc, hf, hb, wdst, ct_ref, mf_ref, mb_ref):
    @pl.when(pl.program_id(1) == 0)
    def _():
        ct_ref[...] = jnp.zeros(ct_ref.shape, F32)
        mf_ref[...] = jnp.full((1, 128), NEG_BIG, F32)
        mb_ref[...] = jnp.full((1, 128), NEG_BIG, F32)

    wdst[...] = wsrc[...].astype(BF16)

    n = CHUNK_A
    r = lax.broadcasted_iota(jnp.int32, (n, n), 0)
    c = lax.broadcasted_iota(jnp.int32, (n, n), 1)
    ones_cols = jnp.ones((n, 128), BF16)
    ones_rows = jnp.ones((128, n), BF16)
    seqs = []
    for head in range(NH_A):
        seqs.append((qf, kf, vf, False, hf, head, head, r >= c))
        seqs.append((qb, kb, vb, True, hb, head, NH_A + head, r <= c))

    loaded = []
    for q_ref, k_ref, v_ref, _, _, head, sid, _ in seqs:
        sl = slice(head * DH_A, (head + 1) * DH_A)
        q = q_ref[:, sl]
        k = k_ref[:, sl]
        v = v_ref[:, sl]
        ct_old = ct_ref[sid]
        qk = lax.dot_general(q, k, _NT, preferred_element_type=F32)
        qc = lax.dot_general(q, ct_old.astype(BF16), _NT, preferred_element_type=F32)
        loaded.append((k, v, ct_old, qk, qc))

    gate_vecs = {False: _gate_vectors(gf, mf_ref, reverse=False),
                 True: _gate_vectors(gb, mb_ref, reverse=True)}
    for (_, _, _, rev, _, _, sid, _), (k, v, ct_old, _, _) in zip(seqs, loaded):
        gates = gate_vecs[rev]
        w = gates[3][:, sid:sid + 1]
        dec = gates[4][:, sid:sid + 1]
        kw = (w * k.astype(F32)).astype(BF16)
        vt_ext = jnp.concatenate([v.T, ones_rows], axis=0)
        ct_ref[sid] = dec * ct_old + jnp.dot(vt_ext, kw, preferred_element_type=F32)

    for (_, _, _, rev, h_ref, head, sid, mask), (k, v, ct_old, qk, qc) in zip(seqs, loaded):
        gates = gate_vecs[rev]
        sl = slice(head * DH_A, (head + 1) * DH_A)
        col_a, a, e = (g[:, sid:sid + 1] for g in gates[:3])
        row_b = gates[5][sid:sid + 1, :]
        s = qk * jnp.exp(jnp.where(mask, col_a + row_b, -jnp.inf))
        v_ext = jnp.concatenate([v, ones_cols], axis=1)
        sv = jnp.dot(s.astype(BF16), v_ext, preferred_element_type=F32) + a * qc
        inv = 1.0 / jnp.maximum(jnp.abs(sv[:, DH_A:]), e)
        h_ref[:, sl] = (sv[:, :DH_A] * jnp.concatenate([inv, inv], axis=1)).astype(BF16)


def _mlstm(p, gates, w_cast, batch, seq):
    nc = seq // CHUNK_A
    t = p.shape[1]
    cast_in, cast_out, cast_shape, cast_blk = _cast_slab_specs(w_cast, batch * nc,
                                                               lambda b, k: (b * nc + k, 0))

    def blk(col, rev):
        if rev:
            return pl.BlockSpec((None, CHUNK_A, W_A), lambda b, k: (col, b * nc + nc - 1 - k, 0))
        return pl.BlockSpec((None, CHUNK_A, W_A), lambda b, k: (col, b * nc + k, 0))

    def outblk(rev):
        if rev:
            return pl.BlockSpec((CHUNK_A, W_A), lambda b, k: (b * nc + nc - 1 - k, 0))
        return pl.BlockSpec((CHUNK_A, W_A), lambda b, k: (b * nc + k, 0))

    gatespec = lambda rev: pl.BlockSpec(
        (CHUNK_A, 256), (lambda b, k: (b * nc + nc - 1 - k, 0)) if rev else (lambda b, k: (b * nc + k, 0)))
    return pl.pallas_call(
        _mlstm_kernel,
        grid=(batch, nc),
        in_specs=[blk(0, False), blk(1, False), blk(2, False),
                  blk(0, True), blk(1, True), blk(2, True),
                  gatespec(False), gatespec(True), cast_in],
        out_specs=[outblk(False), outblk(True), cast_out],
        out_shape=[jax.ShapeDtypeStruct((t, W_A), BF16), jax.ShapeDtypeStruct((t, W_A), BF16),
                   cast_shape],
        scratch_shapes=[pltpu.VMEM((2 * NH_A, DH_A + 128, DH_A), F32),
                        pltpu.VMEM((1, 128), F32), pltpu.VMEM((1, 128), F32)],
        compiler_params=_params(
            ("parallel", "arbitrary"),
            [((CHUNK_A, W_A), BF16)] * 8 + [((CHUNK_A, 256), F32)] * 2 + [cast_blk, cast_blk],
            [((2 * NH_A, DH_A + 128, DH_A), F32)]),
        name="mlstm",
    )(p, p, p, p, p, p, gates, gates, w_cast)


def _na_window_start(g, rows):
    return jnp.clip(g * NA_GROUP - WIN_ROWS // 2, 0, rows - NA_WINDOW)


def _natten_kernel(q_ref, kbuf, vbuf, bias_ref, wsrc, o_ref, wdst, *, rows):
    wdst[...] = wsrc[...].astype(BF16)
    g = pl.program_id(1)
    wstart = _na_window_start(g, rows)
    scale = DH_B ** -0.5
    starts, variants = [], []
    for i in range(NA_GROUP):
        r = g * NA_GROUP + i
        rs = jnp.clip(r - WIN_ROWS // 2, 0, rows - WIN_ROWS)
        starts.append(pl.multiple_of((rs - wstart) * GRID_W, GRID_W))
        variants.append(rs - r + (WIN_ROWS - 1))

    def scores(i, h):
        q = q_ref[h, i * GRID_W:(i + 1) * GRID_W, :]
        kw = kbuf[h, pl.ds(starts[i], WIN_ROWS * GRID_W), :]
        s = lax.dot_general(q, kw, _NT, preferred_element_type=F32)
        bias = jnp.concatenate([bias_ref[variants[i], h, l] for l in range(NA_KEYS // 128)], axis=1)
        s = s * (scale * LOG2_E) + bias
        p = jnp.exp2(s - jnp.max(s, axis=-1, keepdims=True))
        return p.astype(BF16), 1.0 / jnp.sum(p, axis=-1, keepdims=True)

    def weighted_sum(i, h, p, inv_l):
        hs = slice(h * DH_B, (h + 1) * DH_B)
        vw = vbuf[h, pl.ds(starts[i], WIN_ROWS * GRID_W), :]
        o = jnp.dot(p, vw, preferred_element_type=F32)
        o_ref[i * GRID_W:(i + 1) * GRID_W, hs] = (o * inv_l).astype(BF16)

    tiles = [(i, h) for i in range(NA_GROUP) for h in range(NH_B)]
    pending = []
    for idx in range(len(tiles) + NA_LOOKAHEAD):
        if idx < len(tiles):
            pending.append(scores(*tiles[idx]))
        if idx >= NA_LOOKAHEAD:
            weighted_sum(*tiles[idx - NA_LOOKAHEAD], *pending[idx - NA_LOOKAHEAD])
            pending[idx - NA_LOOKAHEAD] = None


def _natten(p, bias, w_cast, batch, seq):
    rows = seq // GRID_W
    ng = rows // NA_GROUP
    blk = NA_GROUP * GRID_W
    win = NA_WINDOW * GRID_W
    t = p.shape[1]
    qpart, kpart, vpart = 0, 1, 2

    def window(part):
        def imap(b, g):
            return (part * NH_B, (b * rows + _na_window_start(g, rows)) * GRID_W, 0)
        return pl.BlockSpec((pl.Element(NH_B), pl.Element(win), pl.Element(DH_B)), imap)

    step = lambda b, g: (b * ng + g, 0)
    c_in, c_out, c_shape, c_blk = _cast_slab_specs(w_cast, batch * ng, step)
    return pl.pallas_call(
        functools.partial(_natten_kernel, rows=rows),
        grid=(batch, ng),
        in_specs=[pl.BlockSpec((NH_B, blk, DH_B), lambda b, g: (qpart, b * ng + g, 0)),
                  window(kpart), window(vpart),
                  pl.BlockSpec(bias.shape, lambda b, g: (0, 0, 0, 0, 0), pipeline_mode=pl.Buffered(1)),
                  c_in],
        out_specs=[pl.BlockSpec((blk, W_B), step), c_out],
        out_shape=[jax.ShapeDtypeStruct((t, W_B), BF16), c_shape],
        compiler_params=_params(
            ("parallel", "arbitrary"),
            [((blk, W_B), BF16)] * 2 + [((win, W_B), BF16)] * 2 + [(bias.shape, F32)]
            + [c_blk, c_blk]),
        name="natten",
    )(p, p, p, bias, w_cast)


def _natten_bias(rpb):
    cols = np.arange(GRID_W)
    cstart = np.clip(cols - WIN_COLS // 2, 0, GRID_W - WIN_COLS)
    inside = (cols[None, :] >= cstart[:, None]) & (cols[None, :] < cstart[:, None] + WIN_COLS)
    dc = cols[None, :] - cols[:, None] + (WIN_COLS - 1)
    onehot = (np.arange(2 * WIN_COLS - 1)[:, None, None] == dc[None]) & inside[None]
    toep = jnp.einsum('hdj,jck->hdck', rpb.astype(F32), onehot.astype(np.float32),
                      precision=lax.Precision.HIGHEST)
    toep = toep * LOG2_E + np.where(inside, 0.0, NEG_BIG).astype(np.float32)
    ntiles = NA_KEYS // 128
    return pl.pallas_call(
        _bias_tiles_kernel,
        grid=(NH_B,),
        in_specs=[pl.BlockSpec((None, 2 * WIN_ROWS - 1, GRID_W, GRID_W), lambda h: (h, 0, 0, 0))],
        out_specs=pl.BlockSpec((WIN_ROWS, None, ntiles, GRID_W, 128), lambda h: (0, h, 0, 0, 0)),
        out_shape=jax.ShapeDtypeStruct((WIN_ROWS, NH_B, ntiles, GRID_W, 128), F32),
        compiler_params=_params(("parallel",), [((2 * WIN_ROWS - 1, GRID_W, 128), F32),
                                                ((WIN_ROWS, ntiles, GRID_W, 128), F32)]),
        name="bias_tiles",
    )(toep)


def _bias_tiles_kernel(t_ref, o_ref):
    rows_per_tile = 128 // GRID_W
    for v in range(WIN_ROWS):
        for l in range(NA_KEYS // 128):
            d = v + l * rows_per_tile
            o_ref[v, l] = jnp.concatenate([t_ref[d + r] for r in range(rows_per_tile)], axis=1)


def _out_kernel(hf_ref, hb_ref, oa_ref, hbt_ref, x_ref, mod_ref, gh_ref, gn_ref, w_ref,
                x1_ref, h2_ref):
    for c in range(TM_OUT // ROW_CHUNK):
        rows = slice(c * ROW_CHUNK, (c + 1) * ROW_CHUNK)
        hs = hf_ref[rows, :].astype(F32) + hb_ref[rows, :].astype(F32)
        parts = []
        for head in range(NH_A):
            hh = hs[:, head * DH_A:(head + 1) * DH_A]
            parts.append(hh * lax.rsqrt(jnp.mean(hh * hh, axis=-1, keepdims=True) + EPS))
        hn = jnp.concatenate(parts, axis=-1) * gh_ref[...]
        oa = oa_ref[rows, :].astype(F32)
        ha = (hn * (1.0 / (1.0 + jnp.exp(-oa)))).astype(BF16)
        mixed = jnp.dot(jnp.concatenate([ha, hbt_ref[rows, :]], axis=1), w_ref[...],
                        preferred_element_type=F32)
        x1 = x_ref[rows, :] + mod_ref[0, 2:3, :] * mixed
        x1_ref[rows, :] = x1
        h2_ref[rows, :] = _rms_mod(x1, gn_ref[...], mod_ref[0, 3:4, :], mod_ref[0, 4:5, :]).astype(BF16)


def _out_proj(hf, hb, pa, hbt, x2, mod, gh, g2, w_o, seq):
    t = x2.shape[0]
    tiles_per_seq = seq // TM_OUT
    row = lambda i: (i, 0)
    resident = pl.Buffered(1)
    return pl.pallas_call(
        _out_kernel,
        grid=(t // TM_OUT,),
        in_specs=[pl.BlockSpec((TM_OUT, W_A), row), pl.BlockSpec((TM_OUT, W_A), row),
                  pl.BlockSpec((None, TM_OUT, W_A), lambda i: (3, i, 0)),
                  pl.BlockSpec((TM_OUT, W_B), row),
                  pl.BlockSpec((TM_OUT, D_MODEL), row),
                  pl.BlockSpec((1, N_MOD, D_MODEL), lambda i: (i // tiles_per_seq, 0, 0)),
                  pl.BlockSpec((1, W_A), lambda i: (0, 0)),
                  pl.BlockSpec((1, D_MODEL), lambda i: (0, 0)),
                  pl.BlockSpec((W_A + W_B, D_MODEL), lambda i: (0, 0), pipeline_mode=resident)],
        out_specs=[pl.BlockSpec((TM_OUT, D_MODEL), row), pl.BlockSpec((TM_OUT, D_MODEL), row)],
        out_shape=[jax.ShapeDtypeStruct((t, D_MODEL), F32), jax.ShapeDtypeStruct((t, D_MODEL), BF16)],
        compiler_params=_params(
            ("parallel",),
            [((TM_OUT, W_A), BF16)] * 4 + [((TM_OUT, D_MODEL), F32)] * 2
            + [((TM_OUT, D_MODEL), BF16)] + [((W_A, D_MODEL), BF16)]),
        name="out_proj",
    )(hf, hb, pa, hbt, x2, mod, gh, g2, w_o)


def _up_kernel(hp_ref, h_ref, hn_ref, wu_ref, wg_ref, cw_ref, cb_ref, wsrc, act_ref, wdst, hext_ref, *,
               tiles_per_seq):
    wdst[...] = wsrc[...].astype(BF16)
    pos = pl.program_id(1) % tiles_per_seq
    n = TM_UP + 2 * HALO
    hext_ref[0:HALO, :] = jnp.where(pos > 0, hp_ref[...], jnp.zeros_like(hp_ref))
    hext_ref[HALO:HALO + TM_UP, :] = h_ref[...]
    hext_ref[HALO + TM_UP:, :] = jnp.where(pos < tiles_per_seq - 1, hn_ref[...], jnp.zeros_like(hn_ref))
    for c in range(TN_UP // COL_CHUNK):
        cols = slice(c * COL_CHUNK, (c + 1) * COL_CHUNK)
        u = jnp.dot(h_ref[...], wu_ref[:, cols], preferred_element_type=F32)
        g = jnp.dot(hext_ref[...], wg_ref[:, cols], preferred_element_type=F32)
        g_prev = pltpu.roll(g, 1, axis=0)[HALO:HALO + TM_UP, :]
        g_next = pltpu.roll(g, n - 1, axis=0)[HALO:HALO + TM_UP, :]
        gc = (g_prev * cw_ref[0:1, cols] + g[HALO:HALO + TM_UP, :] * cw_ref[1:2, cols]
              + g_next * cw_ref[2:3, cols] + cb_ref[:, cols])
        gelu = 0.5 * gc * (1.0 + lax.erf(gc * (2.0 ** -0.5)))
        act_ref[:, cols] = (gelu * u).astype(BF16)


def _up_proj(h2, w_up, conv_w, conv_b, w_cast, seq):
    t = h2.shape[0]
    tiles_per_seq = seq // TM_UP
    hb = TM_UP // HALO
    nh = t // HALO
    ngroups = D_FF // TN_UP
    ntiles = t // TM_UP
    resident = pl.Buffered(1)
    c_in, c_out, c_shape, c_blk = _cast_slab_specs(w_cast, ngroups * ntiles,
                                                   lambda j, i: (j * ntiles + i, 0))
    return pl.pallas_call(
        functools.partial(_up_kernel, tiles_per_seq=tiles_per_seq),
        grid=(ngroups, t // TM_UP),
        in_specs=[pl.BlockSpec((HALO, D_MODEL), lambda j, i: (jnp.maximum(i * hb - 1, 0), 0)),
                  pl.BlockSpec((TM_UP, D_MODEL), lambda j, i: (i, 0)),
                  pl.BlockSpec((HALO, D_MODEL), lambda j, i: (jnp.minimum((i + 1) * hb, nh - 1), 0)),
                  pl.BlockSpec((D_MODEL, TN_UP), lambda j, i: (0, j), pipeline_mode=resident),
                  pl.BlockSpec((D_MODEL, TN_UP), lambda j, i: (0, ngroups + j), pipeline_mode=resident),
                  pl.BlockSpec((8, TN_UP), lambda j, i: (0, j)),
                  pl.BlockSpec((1, TN_UP), lambda j, i: (0, j)), c_in],
        out_specs=[pl.BlockSpec((TM_UP, TN_UP), lambda j, i: (i, j)), c_out],
        out_shape=[jax.ShapeDtypeStruct((t, D_FF), BF16), c_shape],
        scratch_shapes=[pltpu.VMEM((TM_UP + 2 * HALO, D_MODEL), BF16)],
        compiler_params=_params(
            ("arbitrary", "arbitrary"),
            [((TM_UP + 2 * HALO, D_MODEL), BF16), ((D_MODEL, TN_UP), BF16), ((TM_UP, TN_UP), BF16),
             c_blk, c_blk],
            [((TM_UP + 2 * HALO, D_MODEL), BF16)]),
        name="up_proj",
    )(h2, h2, h2, w_up, w_up, conv_w, conv_b, w_cast)


def _down_kernel(a_ref, w_ref, x_ref, mod_ref, g_ref, o_ref):
    for c in range(TM_DOWN // ROW_CHUNK):
        rows = slice(c * ROW_CHUNK, (c + 1) * ROW_CHUNK)
        y = x_ref[rows, :] + mod_ref[0, 5:6, :] * jnp.dot(a_ref[rows, :], w_ref[...],
                                                           preferred_element_type=F32)
        o_ref[rows, :] = y * lax.rsqrt(jnp.mean(y * y, axis=-1, keepdims=True) + EPS) * g_ref[...]


def _down_proj(act, w_d, x1, mod, gfin, seq):
    t = x1.shape[0]
    tiles_per_seq = seq // TM_DOWN
    return pl.pallas_call(
        _down_kernel,
        grid=(t // TM_DOWN,),
        in_specs=[pl.BlockSpec((TM_DOWN, D_FF), lambda i: (i, 0)),
                  pl.BlockSpec((D_FF, D_MODEL), lambda i: (0, 0), pipeline_mode=pl.Buffered(1)),
                  pl.BlockSpec((TM_DOWN, D_MODEL), lambda i: (i, 0)),
                  pl.BlockSpec((1, N_MOD, D_MODEL), lambda i: (i // tiles_per_seq, 0, 0)),
                  pl.BlockSpec((1, D_MODEL), lambda i: (0, 0))],
        out_specs=pl.BlockSpec((TM_DOWN, D_MODEL), lambda i: (i, 0)),
        out_shape=jax.ShapeDtypeStruct((t, D_MODEL), F32),
        compiler_params=pltpu.CompilerParams(
            dimension_semantics=("parallel",),
            vmem_limit_bytes=_nbytes((D_FF, D_MODEL), BF16) + 2 * (
                _nbytes((TM_DOWN, D_FF), BF16) + 2 * _nbytes((TM_DOWN, D_MODEL), F32))
            + DOWN_TEMP_ALLOWANCE),
        name="down_proj",
    )(act, w_d, x1, mod, gfin)


def kernel(x, c, w_ada, b_ada, g_norm1, w_in, b_gates, g_head_a, rpb, w_out, g_norm2, w_up, conv_w,
           conv_b, w_down, g_final):
    batch, seq, d = x.shape
    assert w_ada.shape[0] == 1
    assert d == D_MODEL and seq % CHUNK_A == 0 and seq % (NA_GROUP * GRID_W) == 0
    x2 = x.reshape(batch * seq, d)
    mod = _ada(c, w_ada, b_ada)[:batch].reshape(batch, N_MOD, d)

    w_in_t = jnp.swapaxes(w_in, 1, 2)
    w_main = _win_prep(w_in_t)
    lane_of = np.concatenate([np.arange(0, 4), np.arange(128, 132), np.arange(4, 8), np.arange(132, 136)])
    sel = np.zeros((4 * NH_A, 256), np.float32)
    sel[np.arange(4 * NH_A), lane_of] = 1.0
    w_g = jnp.dot(sel.T, w_in_t[0, GATE_LO:GATE_HI], precision=lax.Precision.HIGHEST).astype(BF16)
    b_g = jnp.dot(b_gates, sel, precision=lax.Precision.HIGHEST)

    pa, pb, gates = _in_proj(x2, mod, g_norm1, w_main, w_g, b_g, seq)
    hf, hb, w_up_b = _mlstm(pa, gates, w_up, batch, seq)
    hbt, w_out_b = _natten(pb, _natten_bias(rpb[0]), w_out, batch, seq)
    x1, h2 = _out_proj(hf, hb, pa, hbt, x2, mod, g_head_a, g_norm2, w_out_b, seq)

    cw8 = jnp.zeros((8, D_FF), F32).at[:3].set(conv_w[0])
    act, w_down_b = _up_proj(h2, w_up_b, cw8, conv_b, w_down, seq)
    out = _down_proj(act, w_down_b, x1, mod, g_final[None, :], seq)
    return out.reshape(batch, seq, d)
```
